```python
import math
import jax, jax.numpy as jnp
from jax import lax
import numpy as np


D_MODEL = 1024
BATCH = 16
SEQ = 2048
DEPTH = 1
DEC_BATCH = 32
DEC_SEQ = 32
PAST_LEN = 2048

CHUNK = 64
Q_BLOCK = 128
FOX_HEADS = 8
FOX_HEAD_DIM = 64
DIFF_HEADS = 4
DIFF_HEAD_DIM = 64
FOX_WIDTH = FOX_HEADS * FOX_HEAD_DIM
DIFF_WIDTH = DIFF_HEADS * 2 * DIFF_HEAD_DIM
MIX_WIDTH = FOX_WIDTH + DIFF_WIDTH
IN_COLS = 3 * FOX_WIDTH + FOX_HEADS + 3 * DIFF_WIDTH
D_FF = 2816
CONV_WIDTH = 3
ROPE_THETA = 10000.0
LN_EPS = 1e-5
RMS_EPS = 1e-6
NEG_BIG = -1e30
FORGET_BIAS = 2.0
DEEPNORM_ALPHA = (2 * DEPTH) ** 0.25
DEEPNORM_BETA = (8 * DEPTH) ** -0.25

kernel_name = 'fox_diffattn_convffn_stream_step'


def _lambda_init(layer_idx):
    return 0.8 - 0.6 * math.exp(-0.3 * layer_idx)


def _rope(x, pos):
    d = x.shape[-1]
    inv = ROPE_THETA ** (-jnp.arange(0, d, 2, dtype=jnp.float32) / d)
    ang = pos.astype(jnp.float32)[:, None] * inv[None, :]
    cos = jnp.cos(ang)[None, :, None, :]
    sin = jnp.sin(ang)[None, :, None, :]
    x32 = x.astype(jnp.float32)
    x1, x2 = x32[..., : d // 2], x32[..., d // 2:]
    return jnp.concatenate([x1 * cos - x2 * sin, x2 * cos + x1 * sin], axis=-1).astype(x.dtype)


def _adaln(c, w_ada, b_ada):
    m = jnp.einsum('bd,de->be', jax.nn.silu(c), w_ada) + b_ada
    return [a[:, None, :] for a in jnp.split(m, 6, axis=-1)]


def _post_norm(x, h, gate, g, b):
    y = DEEPNORM_ALPHA * x.astype(jnp.float32) + gate.astype(jnp.float32) * h.astype(jnp.float32)
    mu = jnp.mean(y, axis=-1, keepdims=True)
    var = jnp.mean(jnp.square(y - mu), axis=-1, keepdims=True)
    return ((y - mu) * lax.rsqrt(var + LN_EPS) * g + b).astype(x.dtype)


def _project(u, w_in, b_f, pos):
    B, T, _ = u.shape
    z = jnp.einsum('btd,de->bte', u, w_in)
    sizes = [FOX_WIDTH, FOX_WIDTH, FOX_WIDTH, FOX_HEADS, DIFF_WIDTH, DIFF_WIDTH]
    idx = [sum(sizes[: i + 1]) for i in range(len(sizes))]
    fq, fk, fv, ff, dq, dk, dv = jnp.split(z, idx, axis=-1)
    fq = fq.reshape(B, T, FOX_HEADS, FOX_HEAD_DIM)
    fk = fk.reshape(B, T, FOX_HEADS, FOX_HEAD_DIM)
    fv = fv.reshape(B, T, FOX_HEADS, FOX_HEAD_DIM)
    logf = jax.nn.log_sigmoid((ff + b_f).astype(jnp.float32))
    dq = _rope(dq.reshape(B, T, 2 * DIFF_HEADS, DIFF_HEAD_DIM), pos)
    dk = _rope(dk.reshape(B, T, 2 * DIFF_HEADS, DIFF_HEAD_DIM), pos)
    dv = dv.reshape(B, T, DIFF_HEADS, 2 * DIFF_HEAD_DIM)
    return fq, fk, fv, logf, dq, dk, dv


def _fox_attend(q, k, v, cum_q, cum_k, q_pos, k_pos):
    s = jnp.einsum('bqhd,bkhd->bhqk', q, k).astype(jnp.float32) * (FOX_HEAD_DIM ** -0.5)
    s = s + jnp.transpose(cum_q, (0, 2, 1))[:, :, :, None] - jnp.transpose(cum_k, (0, 2, 1))[:, :, None, :]
    mask = k_pos[None, :] <= q_pos[:, None]
    p = jax.nn.softmax(jnp.where(mask, s, NEG_BIG), axis=-1)
    return jnp.einsum('bhqk,bkhd->bqhd', p.astype(v.dtype), v)


def _diff_attend(q, k, v, lam, q_pos, k_pos):
    B, Tq = q.shape[0], q.shape[1]
    S = k.shape[1]
    s = jnp.einsum('bqhd,bkhd->bhqk', q, k).astype(jnp.float32) * (DIFF_HEAD_DIM ** -0.5)
    mask = (k_pos[None, :] // CHUNK) <= (q_pos[:, None] // CHUNK)
    p = jax.nn.softmax(jnp.where(mask, s, NEG_BIG), axis=-1).reshape(B, DIFF_HEADS, 2, Tq, S)
    w = p[:, :, 0] - lam * p[:, :, 1]
    return jnp.einsum('bhqk,bkhe->bqhe', w.astype(v.dtype), v)


def _mix_output(fo, do, w_o, subln_g, lam_init):
    B, T = fo.shape[0], fo.shape[1]
    d32 = do.astype(jnp.float32)
    d32 = d32 * lax.rsqrt(jnp.mean(jnp.square(d32), axis=-1, keepdims=True) + RMS_EPS) * subln_g * (1.0 - lam_init)
    o = jnp.concatenate([fo.reshape(B, T, FOX_WIDTH), d32.astype(fo.dtype).reshape(B, T, DIFF_WIDTH)], axis=-1)
    return jnp.einsum('bte,ed->btd', o, w_o)


def _conv_ffn(u, conv_prev, w_up, conv_w, conv_b, w_down):
    T = u.shape[1]
    a, b = jnp.split(jnp.einsum('btd,df->btf', u, w_up), 2, axis=-1)
    a_pad = jnp.concatenate([conv_prev.astype(a.dtype), a], axis=1)
    conv = conv_b
    for j in range(CONV_WIDTH):
        conv = conv + a_pad[:, j:j + T] * conv_w[j]
    h = jax.nn.silu(conv) * b
    return jnp.einsum('btf,fd->btd', h, w_down), a_pad[:, -(CONV_WIDTH - 1):]


def _layer(x, c, past_fk, past_fv, past_logf, past_dk, past_dv, conv_prev,
           w_ada, b_ada, w_in, b_f, lambda_vecs, subln_g, w_o, ln1_g, ln1_b,
           w_up, conv_w, conv_b, w_down, ln2_g, ln2_b, lam_init):
    B, T, _ = x.shape
    P = past_fk.shape[1]
    q_pos = P + jnp.arange(T)
    k_pos = jnp.arange(P + T)
    sh1, sc1, g1, sh2, sc2, g2 = _adaln(c, w_ada, b_ada)
    u = x * (1.0 + sc1) + sh1
    fq, fk, fv, logf, dq, dk, dv = _project(u, w_in, b_f, q_pos)
    fk_all = jnp.concatenate([past_fk.astype(fk.dtype), fk], axis=1)
    fv_all = jnp.concatenate([past_fv.astype(fv.dtype), fv], axis=1)
    dk_all = jnp.concatenate([past_dk.astype(dk.dtype), dk], axis=1)
    dv_all = jnp.concatenate([past_dv.astype(dv.dtype), dv], axis=1)
    cum_k = jnp.cumsum(jnp.concatenate([past_logf.astype(jnp.float32), logf], axis=1), axis=1)
    cum_q = cum_k[:, P:]
    lv = lambda_vecs.astype(jnp.float32)
    lam = jnp.exp(jnp.sum(lv[0] * lv[1])) - jnp.exp(jnp.sum(lv[2] * lv[3])) + lam_init

    def attend(fq_b, dq_b, cq_b, qp_b):
        return (_fox_attend(fq_b, fk_all, fv_all, cq_b, cum_k, qp_b, k_pos),
                _diff_attend(dq_b, dk_all, dv_all, lam, qp_b, k_pos))

    if T > Q_BLOCK:
        nb = T // Q_BLOCK

        def blk(a):
            return a.reshape((B, nb, Q_BLOCK) + a.shape[2:]).swapaxes(0, 1)

        def unblk(a):
            return a.swapaxes(0, 1).reshape((B, T) + a.shape[3:])

        fo, do = lax.map(lambda args: attend(*args),
                         (blk(fq), blk(dq), blk(cum_q), q_pos.reshape(nb, Q_BLOCK)))
        fo, do = unblk(fo), unblk(do)
    else:
        fo, do = attend(fq, dq, cum_q, q_pos)

    h = _mix_output(fo, do, w_o, subln_g, lam_init)
    x = _post_norm(x, h, g1, ln1_g, ln1_b)
    u2 = x * (1.0 + sc2) + sh2
    f_out, conv_state = _conv_ffn(u2, conv_prev, w_up, conv_w, conv_b, w_down)
    x = _post_norm(x, f_out, g2, ln2_g, ln2_b)
    return x, (fk, fv, logf, dk, dv, conv_state)


def setup_inputs(seed: int = 0) -> dict:
    key = jax.random.key(seed)
    ks = jax.random.split(key, 32)
    f32 = jnp.float32
    n = lambda k, s: jax.random.normal(k, s, f32)
    L = DEPTH
    col_scale = jnp.concatenate([
        jnp.ones((2 * FOX_WIDTH,), f32), jnp.full((FOX_WIDTH,), DEEPNORM_BETA, f32),
        jnp.ones((FOX_HEADS + 2 * DIFF_WIDTH,), f32), jnp.full((DIFF_WIDTH,), DEEPNORM_BETA, f32)])
    return {
        'x_prompt': n(ks[0], (BATCH, SEQ, D_MODEL)),
        'x_sample': n(ks[1], (DEC_BATCH, DEC_SEQ, D_MODEL)),
        'c_prompt': n(ks[2], (BATCH, D_MODEL)),
        'c_sample': n(ks[3], (DEC_BATCH, D_MODEL)),
        'cache_fox_k': n(ks[4], (L, DEC_BATCH, PAST_LEN, FOX_HEADS, FOX_HEAD_DIM)),
        'cache_fox_v': n(ks[5], (L, DEC_BATCH, PAST_LEN, FOX_HEADS, FOX_HEAD_DIM)),
        'cache_fox_logf': jax.nn.log_sigmoid(FORGET_BIAS + n(ks[6], (L, DEC_BATCH, PAST_LEN, FOX_HEADS))),
        'cache_diff_k': n(ks[7], (L, DEC_BATCH, PAST_LEN, 2 * DIFF_HEADS, DIFF_HEAD_DIM)),
        'cache_diff_v': n(ks[8], (L, DEC_BATCH, PAST_LEN, DIFF_HEADS, 2 * DIFF_HEAD_DIM)),
        'state_ffn_conv': n(ks[9], (L, DEC_BATCH, CONV_WIDTH - 1, D_FF)),
        'w_ada': n(ks[10], (L, D_MODEL, 6 * D_MODEL)) * (0.5 * D_MODEL ** -0.5),
        'b_ada': 0.01 * n(ks[11], (L, 6 * D_MODEL)),
        'w_in': n(ks[12], (L, D_MODEL, IN_COLS)) * (D_MODEL ** -0.5) * col_scale,
        'b_f': FORGET_BIAS + 0.5 * n(ks[13], (L, FOX_HEADS)),
        'lambda_vecs': 0.1 * n(ks[14], (L, 4, DIFF_HEAD_DIM)),
        'subln_g': 1.0 + 0.02 * n(ks[15], (L, 2 * DIFF_HEAD_DIM)),
        'w_o': n(ks[16], (L, MIX_WIDTH, D_MODEL)) * (MIX_WIDTH ** -0.5) * DEEPNORM_BETA,
        'ln1_g': 1.0 + 0.02 * n(ks[17], (L, D_MODEL)),
        'ln1_b': 0.02 * n(ks[18], (L, D_MODEL)),
        'w_up': n(ks[19], (L, D_MODEL, 2 * D_FF)) * (D_MODEL ** -0.5),
        'conv_w': n(ks[20], (L, CONV_WIDTH, D_FF)) * (CONV_WIDTH ** -0.5),
        'conv_b': 0.02 * n(ks[21], (L, D_FF)),
        'w_down': n(ks[22], (L, D_FF, D_MODEL)) * (D_FF ** -0.5) * DEEPNORM_BETA,
        'ln2_g': 1.0 + 0.02 * n(ks[23], (L, D_MODEL)),
        'ln2_b': 0.02 * n(ks[24], (L, D_MODEL)),
    }


def reference(x_prompt, x_sample, c_prompt, c_sample, cache_fox_k, cache_fox_v, cache_fox_logf,
              cache_diff_k, cache_diff_v, state_ffn_conv, w_ada, b_ada, w_in, b_f, lambda_vecs,
              subln_g, w_o, ln1_g, ln1_b, w_up, conv_w, conv_b, w_down, ln2_g, ln2_b):
    B = x_prompt.shape[0]
    dt = x_prompt.dtype
    e_fk = jnp.zeros((B, 0, FOX_HEADS, FOX_HEAD_DIM), dt)
    e_logf = jnp.zeros((B, 0, FOX_HEADS), jnp.float32)
    e_dk = jnp.zeros((B, 0, 2 * DIFF_HEADS, DIFF_HEAD_DIM), dt)
    e_dv = jnp.zeros((B, 0, DIFF_HEADS, 2 * DIFF_HEAD_DIM), dt)
    e_conv = jnp.zeros((B, CONV_WIDTH - 1, D_FF), dt)
    yp, ys = x_prompt, x_sample
    p_states, s_states = [], []
    for l in range(DEPTH):
        lam_init = _lambda_init(l)
        wts = (w_ada[l], b_ada[l], w_in[l], b_f[l], lambda_vecs[l], subln_g[l], w_o[l],
               ln1_g[l], ln1_b[l], w_up[l], conv_w[l], conv_b[l], w_down[l], ln2_g[l], ln2_b[l])
        yp, st_p = _layer(yp, c_prompt, e_fk, e_fk, e_logf, e_dk, e_dv, e_conv, *wts, lam_init)
        ys, st_s = _layer(ys, c_sample, cache_fox_k[l], cache_fox_v[l], cache_fox_logf[l],
                          cache_diff_k[l], cache_diff_v[l], state_ffn_conv[l], *wts, lam_init)
        p_states.append(st_p)
        s_states.append(st_s)
    p_fk, p_fv, p_logf, p_dk, p_dv, p_conv = [jnp.stack(a, axis=0) for a in zip(*p_states)]
    s_fk, s_fv, s_logf, s_dk, s_dv, s_conv = [jnp.stack(a, axis=0) for a in zip(*s_states)]
    return (yp, ys, p_fk, p_fv, p_logf, p_dk, p_dv, p_conv, s_fk, s_fv, s_logf, s_dk, s_dv, s_conv)
```

```python
import functools
import math

import jax
import jax.numpy as jnp
from jax import lax
from jax.experimental import pallas as pl
from jax.experimental.pallas import tpu as pltpu

F32 = jnp.float32
BF16 = jnp.bfloat16

D_MODEL = 1024
CHUNK = 64
FOX_HEADS = 8
DIFF_HEADS = 4
HEAD_DIM = 64
WIDTH = 512
PAIR = 2 * HEAD_DIM
N_PAIRS = WIDTH // PAIR
D_FF = 2816
CONV_WIDTH = 3
ROPE_THETA = 10000.0
LN_EPS = 1e-5
RMS_EPS = 1e-6
NEG_BIG = -1e30
Q_SCALE = HEAD_DIM ** -0.5

LANES = 128
MXU_DIM = 256
VMEM_LIMIT_BYTES = 56 * 1024 * 1024

GATE_COLS = LANES
OFF_FQ, OFF_FK, OFF_FV = 0, WIDTH, 2 * WIDTH
OFF_FF = 3 * WIDTH
OFF_DQ = OFF_FF + GATE_COLS
OFF_DK, OFF_DV = OFF_DQ + WIDTH, OFF_DQ + 2 * WIDTH
IN_COLS_PADDED = OFF_DV + WIDTH

KV_TILE = 256
TOKEN_TILE = 512
FF_TILE = 256


def _params(*sem):
    return pltpu.CompilerParams(dimension_semantics=sem, vmem_limit_bytes=VMEM_LIMIT_BYTES)


def _resident(shape):
    nd = len(shape)
    return pl.BlockSpec(shape, lambda *_: (0,) * nd, pipeline_mode=pl.Buffered(1))


def _ada_kernel(c_ref, w_ref, b_ref, o_ref):
    c = c_ref[...]
    s = c * (1.0 / (1.0 + jnp.exp(-c)))
    o_ref[...] = jnp.dot(s.astype(BF16), w_ref[...].astype(BF16),
                         preferred_element_type=F32) + b_ref[...]


def _ada(c, w_ada, b_ada):
    n, d = c.shape
    cols = w_ada.shape[1]
    tn = cols // 4
    return pl.pallas_call(
        _ada_kernel,
        grid=(cols // tn,),
        in_specs=[pl.BlockSpec((n, d), lambda j: (0, 0)),
                  pl.BlockSpec((d, tn), lambda j: (0, j)),
                  pl.BlockSpec((1, tn), lambda j: (0, j))],
        out_specs=pl.BlockSpec((n, tn), lambda j: (0, j)),
        out_shape=jax.ShapeDtypeStruct((n, cols), F32),
        compiler_params=_params("arbitrary"),
        name="ada",
    )(c, w_ada, b_ada.reshape(1, cols))


def _inproj_kernel(x_ref, mod_ref, w_ref, bf_ref, cos_ref, sin_ref,
                   fk_o, fv_o, lf_o, dk_o, dv_o,
                   fqb_o, fkb_o, fvb_o, dqb_o, dkb_o, dvb_o):
    bb, tt, d = x_ref.shape
    m = mod_ref[...]
    u = x_ref[...] * (1.0 + m[:, 1:2, :]) + m[:, 0:1, :]
    u = u.reshape(bb * tt, d).astype(BF16)

    def proj(off, width):
        return jnp.dot(u, w_ref[:, off:off + width], preferred_element_type=F32)

    def put(o_ref, v):
        o_ref[...] = v.reshape(o_ref.shape).astype(o_ref.dtype)

    fq = proj(OFF_FQ, WIDTH)
    put(fqb_o, fq * Q_SCALE)
    fk = proj(OFF_FK, WIDTH)
    put(fk_o, fk)
    put(fkb_o, fk)
    fv = proj(OFF_FV, WIDTH)
    put(fv_o, fv)
    put(fvb_o, fv)

    zf = proj(OFF_FF, GATE_COLS) + bf_ref[...]
    lf = jnp.minimum(zf, 0.0) - jnp.log1p(jnp.exp(-jnp.abs(zf)))
    put(lf_o, lf[:, :FOX_HEADS])

    cos = cos_ref[...]
    sin = sin_ref[...]
    lane = lax.broadcasted_iota(jnp.int32, (1, WIDTH), 1)
    first_half = (lane % HEAD_DIM) < (HEAD_DIM // 2)

    def rope(v):
        partner = jnp.where(first_half, pltpu.roll(v, WIDTH - HEAD_DIM // 2, 1),
                            pltpu.roll(v, HEAD_DIM // 2, 1))
        return v * cos + partner * sin

    dq = rope(proj(OFF_DQ, WIDTH))
    put(dqb_o, dq * Q_SCALE)
    dk = rope(proj(OFF_DK, WIDTH))
    put(dk_o, dk)
    put(dkb_o, dk)
    dv = proj(OFF_DV, WIDTH)
    put(dv_o, dv)
    put(dvb_o, dv)


def _inproj(x, mod, w_pad, bf_pad, cos_t, sin_t, bb, tt):
    B, T, d = x.shape
    rows = bb * tt
    grid = (B // bb, T // tt)
    tok = lambda w: pl.BlockSpec((bb, tt, w), lambda b, t: (b, t, 0))
    tab = pl.BlockSpec((rows, WIDTH), (lambda b, t: (t, 0)) if bb == 1 else (lambda b, t: (0, 0)))
    sds = lambda w, dt: jax.ShapeDtypeStruct((B, T, w), dt)
    return pl.pallas_call(
        _inproj_kernel,
        grid=grid,
        in_specs=[tok(d),
                  pl.BlockSpec((bb, 6, d), lambda b, t: (b, 0, 0)),
                  _resident(w_pad.shape),
                  _resident(bf_pad.shape),
                  tab, tab],
        out_specs=[tok(WIDTH), tok(WIDTH), tok(FOX_HEADS), tok(WIDTH), tok(WIDTH)] + [tok(WIDTH)] * 6,
        out_shape=[sds(WIDTH, F32), sds(WIDTH, F32), sds(FOX_HEADS, F32), sds(WIDTH, F32), sds(WIDTH, F32)]
                  + [sds(WIDTH, BF16)] * 6,
        compiler_params=_params("arbitrary", "arbitrary"),
        name="inproj",
    )(x, mod, w_pad, bf_pad, cos_t, sin_t)


def _cum_kernel(x_ref, o_ref):
    S = x_ref.shape[2]
    r = lax.broadcasted_iota(jnp.int32, (MXU_DIM, MXU_DIM), 0)
    c = lax.broadcasted_iota(jnp.int32, (MXU_DIM, MXU_DIM), 1)
    tri = jnp.where(r <= c, 1.0, 0.0).astype(BF16)
    carry = jnp.zeros((FOX_HEADS, 1), F32)
    for k in range(S // MXU_DIM):
        sl = slice(k * MXU_DIM, (k + 1) * MXU_DIM)
        xb = x_ref[0, :, sl]
        hi = xb.astype(BF16)
        r1 = xb - hi.astype(F32)
        mid = r1.astype(BF16)
        lo = (r1 - mid.astype(F32)).astype(BF16)
        part = (jnp.dot(hi, tri, preferred_element_type=F32)
                + jnp.dot(mid, tri, preferred_element_type=F32)
                + jnp.dot(lo, tri, preferred_element_type=F32))
        part = part + carry
        o_ref[0, :, sl] = part
        carry = part[:, MXU_DIM - 1:MXU_DIM]


def _cum(lf_t):
    B, H, S = lf_t.shape
    spec = pl.BlockSpec((1, H, S), lambda b: (b, 0, 0))
    return pl.pallas_call(
        _cum_kernel, grid=(B,), in_specs=[spec], out_specs=spec,
        out_shape=jax.ShapeDtypeStruct((B, H, S), F32),
        compiler_params=_params("arbitrary"),
        name="cum",
    )(lf_t)


def _flash_step(qm, k, v, cq, ck, mask, carry):
    m, l, acc = carry
    s = lax.dot_general(qm, k, (((1,), (1,)), ((), ())), preferred_element_type=F32)
    if cq is not None:
        s = (s + cq) - ck
    if mask is not None:
        s = jnp.where(mask, s, NEG_BIG)
    m_new = jnp.maximum(m, jnp.max(s, axis=-1, keepdims=True))
    alpha = jnp.exp(m - m_new)
    p = jnp.exp(s - m_new)
    l = alpha * l + jnp.sum(p, axis=-1, keepdims=True)
    acc = alpha * acc + jnp.dot(p.astype(BF16), v, preferred_element_type=F32)
    return m_new, l, acc


def _flash_init(tq):
    return (jnp.full((tq, 1), NEG_BIG, F32), jnp.zeros((tq, 1), F32), jnp.zeros((tq, PAIR), F32))


def _lambda_value(lv_ref, lam_init):
    lv = lv_ref[...]
    a = jnp.sum(lv[0:1] * lv[1:2], axis=1, keepdims=True)
    b = jnp.sum(lv[2:3] * lv[3:4], axis=1, keepdims=True)
    return jnp.exp(a) - jnp.exp(b) + lam_init


def _subnorm(a1, l1, a2, l2, lam, g_ref, lam_init):
    d = a1 / l1 - lam * (a2 / l2)
    ms = jnp.mean(d * d, axis=-1, keepdims=True)
    return d * lax.rsqrt(ms + RMS_EPS) * g_ref[...] * (1.0 - lam_init)


def _attn_prompt_kernel(fq_ref, fk_ref, fv_ref, dq_ref, dk_ref, dv_ref, cq_ref, ck_ref,
                        lv_ref, g_ref, o_ref, *, lam_init):
    tq = fq_ref.shape[1]
    i = pl.program_id(1)
    low = lax.broadcasted_iota(jnp.int32, (1, PAIR), 1) < HEAD_DIM
    row = lax.broadcasted_iota(jnp.int32, (tq, tq), 0)
    col = lax.broadcasted_iota(jnp.int32, (tq, tq), 1)
    causal = col <= row
    chunk_causal = (col // CHUNK) <= (row // CHUNK)
    diag = pl.multiple_of(i * tq, tq)

    def run(qm, k_ref, v_ref, lanes, head, mask):
        cq = None if head is None else cq_ref[0, :, head:head + 1]

        def tile(start, msk, carry):
            k = k_ref[0, pl.ds(start, tq), lanes]
            v = v_ref[0, pl.ds(start, tq), lanes]
            ck = None if head is None else ck_ref[0, head:head + 1, pl.ds(start, tq)]
            return _flash_step(qm, k, v, cq, ck, msk, carry)

        carry = lax.fori_loop(
            0, i, lambda j, c: tile(pl.multiple_of(j * tq, tq), None, c), _flash_init(tq))
        return tile(diag, mask, carry)

    for pair in range(N_PAIRS):
        lanes = slice(pair * PAIR, (pair + 1) * PAIR)
        qp = fq_ref[0, :, lanes]
        zero = jnp.zeros_like(qp)
        _, l0, a0 = run(jnp.where(low, qp, zero), fk_ref, fv_ref, lanes, 2 * pair, causal)
        _, l1, a1 = run(jnp.where(low, zero, qp), fk_ref, fv_ref, lanes, 2 * pair + 1, causal)
        o_ref[0, :, lanes] = jnp.where(low, a0 / l0, a1 / l1).astype(o_ref.dtype)

    lam = _lambda_value(lv_ref, lam_init)
    for hd in range(DIFF_HEADS):
        lanes = slice(hd * PAIR, (hd + 1) * PAIR)
        qp = dq_ref[0, :, lanes]
        zero = jnp.zeros_like(qp)
        _, l1, a1 = run(jnp.where(low, qp, zero), dk_ref, dv_ref, lanes, None, chunk_causal)
        _, l2, a2 = run(jnp.where(low, zero, qp), dk_ref, dv_ref, lanes, None, chunk_causal)
        dn = _subnorm(a1, l1, a2, l2, lam, g_ref, lam_init)
        o_ref[0, :, WIDTH + hd * PAIR:WIDTH + (hd + 1) * PAIR] = dn.astype(o_ref.dtype)


def _attn_prompt(fq, fk, fv, dq, dk, dv, cq, ck_t, lambda_vecs, subln_g, lam_init):
    B, T, _ = fq.shape
    tq = KV_TILE
    qspec = pl.BlockSpec((1, tq, WIDTH), lambda b, i: (b, i, 0))
    kvspec = pl.BlockSpec((1, T, WIDTH), lambda b, i: (b, 0, 0))
    return pl.pallas_call(
        functools.partial(_attn_prompt_kernel, lam_init=lam_init),
        grid=(B, T // tq),
        in_specs=[qspec, kvspec, kvspec, qspec, kvspec, kvspec,
                  pl.BlockSpec((1, tq, FOX_HEADS), lambda b, i: (b, i, 0)),
                  pl.BlockSpec((1, FOX_HEADS, T), lambda b, i: (b, 0, 0)),
                  pl.BlockSpec(lambda_vecs.shape, lambda b, i: (0, 0)),
                  pl.BlockSpec(subln_g.shape, lambda b, i: (0, 0))],
        out_specs=pl.BlockSpec((1, tq, 2 * WIDTH), lambda b, i: (b, i, 0)),
        out_shape=jax.ShapeDtypeStruct((B, T, 2 * WIDTH), BF16),
        compiler_params=_params("arbitrary", "arbitrary"),
        name="attn_prompt",
    )(fq, fk, fv, dq, dk, dv, cq, ck_t, lambda_vecs, subln_g)


def _attn_sample_kernel(fq_ref, fk_ref, fv_ref, dq_ref, dk_ref, dv_ref,
                        pfk_ref, pfv_ref, pdk_ref, pdv_ref, cq_ref, ck_ref,
                        lv_ref, g_ref, o_ref, *, lam_init):
    tq = fq_ref.shape[1]
    P = pfk_ref.shape[1]
    tk = KV_TILE
    low = lax.broadcasted_iota(jnp.int32, (1, PAIR), 1) < HEAD_DIM
    row = lax.broadcasted_iota(jnp.int32, (tq, tq), 0)
    col = lax.broadcasted_iota(jnp.int32, (tq, tq), 1)
    causal = col <= row
    chunk_causal = ((col + P) // CHUNK) <= ((row + P) // CHUNK)

    def run(qm, pk_ref, pv_ref, k_ref, v_ref, lanes, head, mask):
        cq = None if head is None else cq_ref[0, :, head:head + 1]

        def past(j, carry):
            start = pl.multiple_of(j * tk, tk)
            k = pk_ref[0, pl.ds(start, tk), lanes].astype(BF16)
            v = pv_ref[0, pl.ds(start, tk), lanes].astype(BF16)
            ck = None if head is None else ck_ref[0, head:head + 1, pl.ds(start, tk)]
            return _flash_step(qm, k, v, cq, ck, None, carry)

        carry = lax.fori_loop(0, P // tk, past, _flash_init(tq))
        ck = None if head is None else ck_ref[0, head:head + 1, P:P + tq]
        return _flash_step(qm, k_ref[0, :, lanes], v_ref[0, :, lanes], cq, ck, mask, carry)

    for pair in range(N_PAIRS):
        lanes = slice(pair * PAIR, (pair + 1) * PAIR)
        qp = fq_ref[0, :, lanes]
        zero = jnp.zeros_like(qp)
        _, l0, a0 = run(jnp.where(low, qp, zero), pfk_ref, pfv_ref, fk_ref, fv_ref, lanes, 2 * pair, causal)
        _, l1, a1 = run(jnp.where(low, zero, qp), pfk_ref, pfv_ref, fk_ref, fv_ref, lanes, 2 * pair + 1, causal)
        o_ref[0, :, lanes] = jnp.where(low, a0 / l0, a1 / l1).astype(o_ref.dtype)

    lam = _lambda_value(lv_ref, lam_init)
    for hd in range(DIFF_HEADS):
        lanes = slice(hd * PAIR, (hd + 1) * PAIR)
        qp = dq_ref[0, :, lanes]
        zero = jnp.zeros_like(qp)
        _, l1, a1 = run(jnp.where(low, qp, zero), pdk_ref, pdv_ref, dk_ref, dv_ref, lanes, None, chunk_causal)
        _, l2, a2 = run(jnp.where(low, zero, qp), pdk_ref, pdv_ref, dk_ref, dv_ref, lanes, None, chunk_causal)
        dn = _subnorm(a1, l1, a2, l2, lam, g_ref, lam_init)
        o_ref[0, :, WIDTH + hd * PAIR:WIDTH + (hd + 1) * PAIR] = dn.astype(o_ref.dtype)


def _attn_sample(fq, fk, fv, dq, dk, dv, pfk, pfv, pdk, pdv, cq, ck_t, lambda_vecs, subln_g, lam_init):
    B, T, _ = fq.shape
    P = pfk.shape[1]
    S = ck_t.shape[2]
    new = pl.BlockSpec((1, T, WIDTH), lambda b: (b, 0, 0))
    past = pl.BlockSpec((1, P, WIDTH), lambda b: (b, 0, 0))
    return pl.pallas_call(
        functools.partial(_attn_sample_kernel, lam_init=lam_init),
        grid=(B,),
        in_specs=[new] * 6 + [past] * 4 + [
            pl.BlockSpec((1, T, FOX_HEADS), lambda b: (b, 0, 0)),
            pl.BlockSpec((1, FOX_HEADS, S), lambda b: (b, 0, 0)),
            pl.BlockSpec(lambda_vecs.shape, lambda b: (0, 0)),
            pl.BlockSpec(subln_g.shape, lambda b: (0, 0))],
        out_specs=pl.BlockSpec((1, T, 2 * WIDTH), lambda b: (b, 0, 0)),
        out_shape=jax.ShapeDtypeStruct((B, T, 2 * WIDTH), BF16),
        compiler_params=_params("arbitrary"),
        name="attn_sample",
    )(fq, fk, fv, dq, dk, dv, pfk, pfv, pdk, pdv, cq, ck_t, lambda_vecs, subln_g)


def _post_norm(x, h, gate, g, b, alpha):
    y = alpha * x + gate * h
    mu = jnp.mean(y, axis=-1, keepdims=True)
    yc = y - mu
    var = jnp.mean(yc * yc, axis=-1, keepdims=True)
    return yc * lax.rsqrt(var + LN_EPS) * g + b


def _post_kernel(x_ref, o_ref, mod_ref, prev_ref, wo_ref, ln1g_ref, ln1b_ref,
                 wup_ref, cw_ref, cb_ref, wdn_ref, ln2g_ref, ln2b_ref,
                 y_ref, conv_ref, carry_ref, acc_ref, *, alpha):
    bb, tt, d = x_ref.shape
    rows = bb * tt
    keep = CONV_WIDTH - 1
    t = pl.program_id(1)

    @pl.when(t == 0)
    def _():
        carry_ref[...] = prev_ref[...]

    m = mod_ref[...]

    def rows3(v):
        return jnp.broadcast_to(v, (bb, tt, v.shape[-1])).reshape(rows, v.shape[-1])

    x = x_ref[...].reshape(rows, d)
    h = jnp.dot(o_ref[...].reshape(rows, d), wo_ref[...], preferred_element_type=F32)
    x1 = _post_norm(x, h, rows3(m[:, 2:3, :]), ln1g_ref[...], ln1b_ref[...], alpha)
    u2 = (x1 * (1.0 + rows3(m[:, 4:5, :])) + rows3(m[:, 3:4, :])).astype(BF16)

    tpos = lax.broadcasted_iota(jnp.int32, (bb, tt, 1), 1).reshape(rows, 1)
    acc_ref[...] = jnp.zeros_like(acc_ref)
    for c in range(D_FF // FF_TILE):
        cols = slice(c * FF_TILE, (c + 1) * FF_TILE)
        a = jnp.dot(u2, wup_ref[:, cols], preferred_element_type=F32)
        g = jnp.dot(u2, wup_ref[:, D_FF + c * FF_TILE:D_FF + (c + 1) * FF_TILE],
                    preferred_element_type=F32)
        prev = carry_ref[:, :, cols]
        p2 = rows3(prev[:, 0:1, :])
        p1 = rows3(prev[:, 1:2, :])
        am1 = jnp.where(tpos == 0, p1, pltpu.roll(a, 1, 0))
        am2 = jnp.where(tpos == 0, p2, jnp.where(tpos == 1, p1, pltpu.roll(a, 2, 0)))
        cw = cw_ref[:, cols]
        conv = cb_ref[:, cols] + am2 * cw[0:1] + am1 * cw[1:2] + a * cw[2:3]
        hid = conv * (1.0 / (1.0 + jnp.exp(-conv))) * g
        acc_ref[...] += jnp.dot(hid.astype(BF16), wdn_ref[cols, :], preferred_element_type=F32)
        last = a.reshape(bb, tt, FF_TILE)[:, tt - keep:, :]
        carry_ref[:, :, cols] = last
        conv_ref[:, :, cols] = last

    y = _post_norm(x1, acc_ref[...], rows3(m[:, 5:6, :]), ln2g_ref[...], ln2b_ref[...], alpha)
    y_ref[...] = y.reshape(bb, tt, d)


def _post(x, o, mod, conv_prev, wo, ln1g, ln1b, wup, cw, cb, wdn, ln2g, ln2b, bb, tt, alpha):
    B, T, d = x.shape
    keep = CONV_WIDTH - 1
    tok = pl.BlockSpec((bb, tt, d), lambda b, t: (b, t, 0))
    per_b = lambda r, w: pl.BlockSpec((bb, r, w), lambda b, t: (b, 0, 0))
    return pl.pallas_call(
        functools.partial(_post_kernel, alpha=alpha),
        grid=(B // bb, T // tt),
        in_specs=[tok, tok, per_b(6, d), per_b(keep, D_FF),
                  _resident(wo.shape), _resident(ln1g.shape), _resident(ln1b.shape),
                  _resident(wup.shape), _resident(cw.shape), _resident(cb.shape),
                  _resident(wdn.shape), _resident(ln2g.shape), _resident(ln2b.shape)],
        out_specs=[tok, per_b(keep, D_FF)],
        out_shape=[jax.ShapeDtypeStruct((B, T, d), F32),
                   jax.ShapeDtypeStruct((B, keep, D_FF), F32)],
        scratch_shapes=[pltpu.VMEM((bb, keep, D_FF), F32),
                        pltpu.VMEM((bb * tt, d), F32)],
        compiler_params=_params("arbitrary", "arbitrary"),
        name="post",
    )(x, o, mod, conv_prev, wo, ln1g, ln1b, wup, cw, cb, wdn, ln2g, ln2b)


def _rope_tables(pos0, T):
    half = HEAD_DIM // 2
    inv = ROPE_THETA ** (-jnp.arange(0, HEAD_DIM, 2, dtype=F32) / HEAD_DIM)
    ang = (pos0 + jnp.arange(T)).astype(F32)[:, None] * inv[None, :]
    cos, sin = jnp.cos(ang), jnp.sin(ang)
    reps = WIDTH // HEAD_DIM
    return (jnp.tile(jnp.concatenate([cos, cos], axis=1), (1, reps)),
            jnp.tile(jnp.concatenate([-sin, sin], axis=1), (1, reps)))


def _round_up(n, k):
    return -(-n // k) * k


def _layer(x, mod, past, w, lam_init, alpha):
    B, T, d = x.shape
    P = 0 if past is None else past[0].shape[1]
    if T % TOKEN_TILE == 0:
        bb, tt = 1, TOKEN_TILE
    else:
        tt = T
        bb = math.gcd(B, max(1, TOKEN_TILE // T))
    cos_t, sin_t = _rope_tables(P, T)
    if bb > 1:
        cos_t, sin_t = jnp.tile(cos_t, (bb, 1)), jnp.tile(sin_t, (bb, 1))
    (fk, fv, lf, dk, dv, fqb, fkb, fvb, dqb, dkb, dvb) = _inproj(
        x, mod, w["w_in"], w["b_f"], cos_t, sin_t, bb, tt)

    lf_t = jnp.swapaxes(lf, 1, 2)
    if past is None:
        ck_t = _cum(lf_t)
        cq = jnp.swapaxes(ck_t, 1, 2)
        o = _attn_prompt(fqb, fkb, fvb, dqb, dkb, dvb, cq, ck_t,
                         w["lambda_vecs"], w["subln_g"], lam_init)
        conv_prev = jnp.zeros((B, CONV_WIDTH - 1, D_FF), F32)
    else:
        pfk, pfv, plf, pdk, pdv, conv_prev = past
        S = _round_up(P + T, MXU_DIM)
        lf_all = jnp.concatenate(
            [jnp.swapaxes(plf, 1, 2), lf_t, jnp.zeros((B, FOX_HEADS, S - P - T), F32)], axis=2)
        ck_t = _cum(lf_all)
        cq = jnp.swapaxes(ck_t[:, :, P:P + T], 1, 2)
        o = _attn_sample(fqb, fkb, fvb, dqb, dkb, dvb,
                         pfk.reshape(B, P, WIDTH), pfv.reshape(B, P, WIDTH),
                         pdk.reshape(B, P, WIDTH), pdv.reshape(B, P, WIDTH),
                         cq, ck_t, w["lambda_vecs"], w["subln_g"], lam_init)

    y, conv = _post(x, o, mod, conv_prev, w["w_o"], w["ln1_g"], w["ln1_b"], w["w_up"],
                    w["conv_w"], w["conv_b"], w["w_down"], w["ln2_g"], w["ln2_b"], bb, tt, alpha)
    state = (fk.reshape(B, T, FOX_HEADS, HEAD_DIM), fv.reshape(B, T, FOX_HEADS, HEAD_DIM), lf,
             dk.reshape(B, T, 2 * DIFF_HEADS, HEAD_DIM), dv.reshape(B, T, DIFF_HEADS, 2 * HEAD_DIM), conv)
    return y, state


def kernel(x_prompt, x_sample, c_prompt, c_sample, cache_fox_k, cache_fox_v, cache_fox_logf, cache_diff_k, cache_diff_v, state_ffn_conv, w_ada, b_ada, w_in, b_f, lambda_vecs, subln_g, w_o, ln1_g, ln1_b, w_up, conv_w, conv_b, w_down, ln2_g, ln2_b):
    depth = w_ada.shape[0]
    alpha = (2 * depth) ** 0.25
    nb = c_prompt.shape[0]
    yp, ys = x_prompt, x_sample
    c_all = jnp.concatenate([c_prompt, c_sample], axis=0)
    p_states, s_states = [], []
    for l in range(depth):
        lam_init = 0.8 - 0.6 * math.exp(-0.3 * l)
        w_in_l = w_in[l]
        gate_w = jnp.pad(w_in_l[:, OFF_FF:OFF_FF + FOX_HEADS], ((0, 0), (0, GATE_COLS - FOX_HEADS)))
        w = {
            "w_in": jnp.concatenate(
                [w_in_l[:, :OFF_FF], gate_w, w_in_l[:, OFF_FF + FOX_HEADS:]], axis=1).astype(BF16),
            "b_f": jnp.pad(b_f[l], (0, GATE_COLS - FOX_HEADS)).reshape(1, GATE_COLS),
            "lambda_vecs": lambda_vecs[l],
            "subln_g": subln_g[l].reshape(1, PAIR),
            "w_o": w_o[l].astype(BF16),
            "ln1_g": ln1_g[l].reshape(1, D_MODEL), "ln1_b": ln1_b[l].reshape(1, D_MODEL),
            "w_up": w_up[l].astype(BF16),
            "conv_w": conv_w[l], "conv_b": conv_b[l].reshape(1, D_FF),
            "w_down": w_down[l].astype(BF16),
            "ln2_g": ln2_g[l].reshape(1, D_MODEL), "ln2_b": ln2_b[l].reshape(1, D_MODEL),
        }
        mod = _ada(c_all, w_ada[l], b_ada[l]).reshape(c_all.shape[0], 6, D_MODEL)
        yp, st_p = _layer(yp, mod[:nb], None, w, lam_init, alpha)
        past = (cache_fox_k[l], cache_fox_v[l], cache_fox_logf[l], cache_diff_k[l], cache_diff_v[l],
                state_ffn_conv[l])
        ys, st_s = _layer(ys, mod[nb:], past, w, lam_init, alpha)
        p_states.append(st_p)
        s_states.append(st_s)
    p_out = [jnp.stack(a, axis=0) for a in zip(*p_states)]
    s_out = [jnp.stack(a, axis=0) for a in zip(*s_states)]
    return (yp, ys, *p_out, *s_out)
```

```python
import functools
import math

import jax
import jax.numpy as jnp
from jax import lax
from jax.experimental import pallas as pl
from jax.experimental.pallas import tpu as pltpu

F32 = jnp.float32
BF16 = jnp.bfloat16

D_MODEL = 1024
CHUNK = 64
FOX_HEADS = 8
DIFF_HEADS = 4
HEAD_DIM = 64
WIDTH = 512
PAIR = 2 * HEAD_DIM
N_PAIRS = WIDTH // PAIR
D_FF = 2816
CONV_WIDTH = 3
ROPE_THETA = 10000.0
LN_EPS = 1e-5
RMS_EPS = 1e-6
NEG_BIG = -1e30
Q_SCALE = HEAD_DIM ** -0.5
LOG2E = math.log2(math.e)

LANES = 128
MXU_DIM = 256
VMEM_LIMIT_BYTES = 56 * 1024 * 1024

GATE_COLS = LANES
OFF_FQ, OFF_FK, OFF_FV = 0, WIDTH, 2 * WIDTH
OFF_FF = 3 * WIDTH
OFF_DQ = OFF_FF + GATE_COLS
OFF_DK, OFF_DV = OFF_DQ + WIDTH, OFF_DQ + 2 * WIDTH
IN_COLS_PADDED = OFF_DV + WIDTH

KV_TILE = 256
TOKEN_TILE = 512
FF_TILE = 256


def _params(*sem):
    return pltpu.CompilerParams(dimension_semantics=sem, vmem_limit_bytes=VMEM_LIMIT_BYTES)


def _resident(shape):
    nd = len(shape)
    return pl.BlockSpec(shape, lambda *_: (0,) * nd, pipeline_mode=pl.Buffered(1))


def _ada_kernel(c_ref, w_ref, b_ref, o_ref):
    c = c_ref[...]
    s = c * (1.0 / (1.0 + jnp.exp(-c)))
    o_ref[...] = jnp.dot(s.astype(BF16), w_ref[...].astype(BF16),
                         preferred_element_type=F32) + b_ref[...]


def _ada(c, w_ada, b_ada):
    n, d = c.shape
    cols = w_ada.shape[1]
    tn = cols // 4
    return pl.pallas_call(
        _ada_kernel,
        grid=(cols // tn,),
        in_specs=[pl.BlockSpec((n, d), lambda j: (0, 0)),
                  pl.BlockSpec((d, tn), lambda j: (0, j)),
                  pl.BlockSpec((1, tn), lambda j: (0, j))],
        out_specs=pl.BlockSpec((n, tn), lambda j: (0, j)),
        out_shape=jax.ShapeDtypeStruct((n, cols), F32),
        compiler_params=_params("arbitrary"),
        name="ada",
    )(c, w_ada, b_ada.reshape(1, cols))


def _inproj_kernel(x_ref, mod_ref, w_ref, bf_ref, cos_ref, sin_ref,
                   fk_o, fv_o, lf_o, dk_o, dv_o,
                   fqb_o, fkb_o, fvb_o, dqb_o, dkb_o, dvb_o, *, transposed):
    bb, tt, d = x_ref.shape
    m = mod_ref[...]
    u = x_ref[...] * (1.0 + m[:, 1:2, :]) + m[:, 0:1, :]
    u = u.reshape(bb * tt, d).astype(BF16)
    q_scale = Q_SCALE * LOG2E if transposed else Q_SCALE

    def proj(off, width):
        return jnp.dot(u, w_ref[:, off:off + width], preferred_element_type=F32)

    def put(o_ref, v):
        o_ref[...] = v.reshape(o_ref.shape).astype(o_ref.dtype)

    def put_qv(o_ref, v):
        if transposed:
            for c in range(tt // KV_TILE):
                o_ref[0, c] = v[c * KV_TILE:(c + 1) * KV_TILE, :].T.astype(o_ref.dtype)
        else:
            put(o_ref, v)

    fq = proj(OFF_FQ, WIDTH)
    put_qv(fqb_o, fq * q_scale)
    fk = proj(OFF_FK, WIDTH)
    put(fk_o, fk)
    put(fkb_o, fk)
    fv = proj(OFF_FV, WIDTH)
    put(fv_o, fv)
    put_qv(fvb_o, fv)

    zf = proj(OFF_FF, GATE_COLS) + bf_ref[...]
    lf = jnp.minimum(zf, 0.0) - jnp.log1p(jnp.exp(-jnp.abs(zf)))
    put(lf_o, lf[:, :FOX_HEADS])

    cos = cos_ref[...]
    sin = sin_ref[...]
    lane = lax.broadcasted_iota(jnp.int32, (1, WIDTH), 1)
    first_half = (lane % HEAD_DIM) < (HEAD_DIM // 2)

    def rope(v):
        partner = jnp.where(first_half, pltpu.roll(v, WIDTH - HEAD_DIM // 2, 1),
                            pltpu.roll(v, HEAD_DIM // 2, 1))
        return v * cos + partner * sin

    dq = rope(proj(OFF_DQ, WIDTH))
    put_qv(dqb_o, dq * q_scale)
    dk = rope(proj(OFF_DK, WIDTH))
    put(dk_o, dk)
    put(dkb_o, dk)
    dv = proj(OFF_DV, WIDTH)
    put(dv_o, dv)
    put_qv(dvb_o, dv)


def _inproj(x, mod, w_pad, bf_pad, cos_t, sin_t, bb, tt, transposed):
    B, T, d = x.shape
    rows = bb * tt
    grid = (B // bb, T // tt)
    assert not transposed or (bb == 1 and tt % KV_TILE == 0)
    tok = lambda w: pl.BlockSpec((bb, tt, w), lambda b, t: (b, t, 0))
    tab = pl.BlockSpec((rows, WIDTH), (lambda b, t: (t, 0)) if bb == 1 else (lambda b, t: (0, 0)))
    sds = lambda w, dt: jax.ShapeDtypeStruct((B, T, w), dt)
    if transposed:
        qv = pl.BlockSpec((1, tt // KV_TILE, WIDTH, KV_TILE), lambda b, t: (b, t, 0, 0))
        qv_sds = jax.ShapeDtypeStruct((B, T // KV_TILE, WIDTH, KV_TILE), BF16)
    else:
        qv, qv_sds = tok(WIDTH), sds(WIDTH, BF16)
    return pl.pallas_call(
        functools.partial(_inproj_kernel, transposed=transposed),
        grid=grid,
        in_specs=[tok(d),
                  pl.BlockSpec((bb, 6, d), lambda b, t: (b, 0, 0)),
                  _resident(w_pad.shape),
                  _resident(bf_pad.shape),
                  tab, tab],
        out_specs=[tok(WIDTH), tok(WIDTH), tok(FOX_HEADS), tok(WIDTH), tok(WIDTH),
                   qv, tok(WIDTH), qv, qv, tok(WIDTH), qv],
        out_shape=[sds(WIDTH, F32), sds(WIDTH, F32), sds(FOX_HEADS, F32), sds(WIDTH, F32), sds(WIDTH, F32),
                   qv_sds, sds(WIDTH, BF16), qv_sds, qv_sds, sds(WIDTH, BF16), qv_sds],
        compiler_params=_params("arbitrary", "arbitrary"),
        name="inproj",
    )(x, mod, w_pad, bf_pad, cos_t, sin_t)


def _split3(x):
    hi = x.astype(BF16).astype(F32)
    r1 = x - hi
    mid = r1.astype(BF16).astype(F32)
    lo = (r1 - mid).astype(BF16).astype(F32)
    return hi, mid, lo


def _cum_kernel(x_ref, o_ref, *maybe_e_ref):
    S = x_ref.shape[2]
    r = lax.broadcasted_iota(jnp.int32, (MXU_DIM, MXU_DIM), 0)
    c = lax.broadcasted_iota(jnp.int32, (MXU_DIM, MXU_DIM), 1)
    tri = jnp.where(r <= c, 1.0, 0.0).astype(BF16)
    carry = jnp.zeros((FOX_HEADS, 1), F32)
    for k in range(S // MXU_DIM):
        sl = slice(k * MXU_DIM, (k + 1) * MXU_DIM)
        hi, mid, lo = _split3(x_ref[0, :, sl])
        part = (jnp.dot(hi.astype(BF16), tri, preferred_element_type=F32)
                + jnp.dot(mid.astype(BF16), tri, preferred_element_type=F32)
                + jnp.dot(lo.astype(BF16), tri, preferred_element_type=F32))
        part = part + carry
        carry = part[:, MXU_DIM - 1:MXU_DIM]
        if not maybe_e_ref:
            o_ref[0, :, sl] = part
            continue
        scaled = part * LOG2E
        o_ref[0, :, sl] = scaled
        pieces = _split3(-scaled)
        pad = jnp.zeros((LANES - 3 * FOX_HEADS, MXU_DIM), F32)
        maybe_e_ref[0][0, sl, :] = jnp.concatenate(pieces + (pad,), axis=0).T.astype(BF16)


def _cum(lf_t, with_bias):
    B, H, S = lf_t.shape
    spec = pl.BlockSpec((1, H, S), lambda b: (b, 0, 0))
    out_specs, out_shape = [spec], [jax.ShapeDtypeStruct((B, H, S), F32)]
    if with_bias:
        out_specs.append(pl.BlockSpec((1, S, LANES), lambda b: (b, 0, 0)))
        out_shape.append(jax.ShapeDtypeStruct((B, S, LANES), BF16))
    out = pl.pallas_call(
        _cum_kernel, grid=(B,), in_specs=[spec], out_specs=out_specs, out_shape=out_shape,
        compiler_params=_params("arbitrary"),
        name="cum",
    )(lf_t)
    return out if with_bias else out[0]


def _flash_step(qm, k, v, cq, ck, mask, carry):
    m, l, acc = carry
    s = lax.dot_general(qm, k, (((1,), (1,)), ((), ())), preferred_element_type=F32)
    if cq is not None:
        s = (s + cq) - ck
    if mask is not None:
        s = jnp.where(mask, s, NEG_BIG)
    m_new = jnp.maximum(m, jnp.max(s, axis=-1, keepdims=True))
    alpha = jnp.exp(m - m_new)
    p = jnp.exp(s - m_new)
    l = alpha * l + jnp.sum(p, axis=-1, keepdims=True)
    acc = alpha * acc + jnp.dot(p.astype(BF16), v, preferred_element_type=F32)
    return m_new, l, acc


def _flash_init(tq):
    return (jnp.full((tq, 1), NEG_BIG, F32), jnp.zeros((tq, 1), F32), jnp.zeros((tq, PAIR), F32))


def _lambda_value(lv_ref, lam_init):
    lv = lv_ref[...]
    a = jnp.sum(lv[0:1] * lv[1:2], axis=1, keepdims=True)
    b = jnp.sum(lv[2:3] * lv[3:4], axis=1, keepdims=True)
    return jnp.exp(a) - jnp.exp(b) + lam_init


def _subnorm(a1, l1, a2, l2, lam, g_ref, lam_init):
    d = a1 / l1 - lam * (a2 / l2)
    ms = jnp.mean(d * d, axis=-1, keepdims=True)
    return d * lax.rsqrt(ms + RMS_EPS) * g_ref[...] * (1.0 - lam_init)


def _softmax_step_t(s, cq, mask, vt, carry):
    m, l, acc = carry
    if mask is not None:
        s = jnp.where(mask, s, NEG_BIG)
    mb = jnp.max(s, axis=0, keepdims=True)
    m_new = jnp.maximum(m, mb if cq is None else mb + cq)
    p = jnp.exp2(s - (m_new if cq is None else m_new - cq))
    alpha = jnp.exp2(m - m_new)
    l = alpha * l + jnp.sum(p, axis=0, keepdims=True)
    acc = alpha * acc + jnp.dot(vt, p.astype(BF16), preferred_element_type=F32)
    return m_new, l, acc


def _attn_prompt_kernel(fq_ref, fk_ref, e_ref, fv_ref, dq_ref, dk_ref, dv_ref, cq_ref,
                        lv_ref, g_ref, o_ref, s_ref, acc_ref, *, lam_init):
    tq = tk = KV_TILE
    i = pl.program_id(1)
    srow = lax.broadcasted_iota(jnp.int32, (PAIR, tq), 0)
    first = srow < HEAD_DIM
    krow = lax.broadcasted_iota(jnp.int32, (tk, tq), 0)
    qcol = lax.broadcasted_iota(jnp.int32, (tk, tq), 1)
    causal = krow <= qcol
    chunk_causal = (krow // CHUNK) <= (qcol // CHUNK)

    def attend(units, mask, rows):
        chains = []
        for wa, wb, cqa, cqb, keys, vta, vtb in units:
            chains += [(wa, cqa, keys, vta), (wb, cqb, keys, vtb)]
        n = len(chains)

        def score(c, j):
            w, _, keys, _ = chains[c]
            s = jnp.dot(keys(j), w, preferred_element_type=F32)
            s_ref[c] = s
            return jnp.max(s, axis=0, keepdims=True)

        def consume(c, j, mb, msk, m, l):
            _, cq, _, vt = chains[c]
            s = s_ref[c]
            if msk is not None:
                s = jnp.where(msk, s, NEG_BIG)
                mb = jnp.max(s, axis=0, keepdims=True)
            m_new = jnp.maximum(m, mb if cq is None else mb + cq)
            p = jnp.exp2(s - (m_new if cq is None else m_new - cq))
            alpha = jnp.exp2(m - m_new)
            acc_ref[c, :rows] = alpha * acc_ref[c, :rows] + jnp.dot(
                vt(j), p.astype(BF16), preferred_element_type=F32)
            return m_new, alpha * l + jnp.sum(p, axis=0, keepdims=True)

        def body(j, carry):
            ms, ls, mbs = carry
            ms, ls, mbs = list(ms), list(ls), list(mbs)
            for c in range(n):
                ms[c], ls[c] = consume(c, j, mbs[c], None, ms[c], ls[c])
                mbs[c] = score(c, j + 1)
            return tuple(ms), tuple(ls), tuple(mbs)

        acc_ref[...] = jnp.zeros_like(acc_ref)
        ms, ls, _ = lax.fori_loop(
            0, i, body,
            (tuple(jnp.full((1, tq), NEG_BIG, F32) for _ in range(n)),
             tuple(jnp.zeros((1, tq), F32) for _ in range(n)),
             tuple(score(c, 0) for c in range(n))))
        ls = [consume(c, i, None, mask, ms[c], ls[c])[1] for c in range(n)]
        outs = [acc_ref[c, :rows] / ls[c] for c in range(n)]
        return list(zip(outs[0::2], outs[1::2]))

    def rows_of(j):
        return pl.ds(pl.multiple_of(j * tk, tk), tk)

    def select(head):
        return jnp.where((srow % FOX_HEADS == head) & (srow < 3 * FOX_HEADS), 1.0, 0.0).astype(BF16)

    def pair_lanes(p):
        return slice(p * PAIR, (p + 1) * PAIR)

    zero = jnp.zeros((PAIR, tq), BF16)

    def fox_unit(pair):
        lanes = pair_lanes(pair)
        qt = fq_ref[0, 0, lanes, :]
        return (jnp.concatenate([jnp.where(first, qt, zero), select(2 * pair)], axis=0),
                jnp.concatenate([jnp.where(first, zero, qt), select(2 * pair + 1)], axis=0),
                cq_ref[0, 2 * pair:2 * pair + 1, :], cq_ref[0, 2 * pair + 1:2 * pair + 2, :],
                lambda j: jnp.concatenate([fk_ref[0, rows_of(j), lanes], e_ref[0, rows_of(j), :]], axis=1),
                lambda j: fv_ref[0, j, pair * PAIR:pair * PAIR + HEAD_DIM, :],
                lambda j: fv_ref[0, j, pair * PAIR + HEAD_DIM:(pair + 1) * PAIR, :])

    def diff_unit(hd):
        lanes = pair_lanes(hd)
        qt = dq_ref[0, 0, lanes, :]
        vt = lambda j: dv_ref[0, j, lanes, :]
        return (jnp.where(first, qt, zero), jnp.where(first, zero, qt), None, None,
                lambda j: dk_ref[0, rows_of(j), lanes], vt, vt)

    fox = attend([fox_unit(p) for p in range(N_PAIRS)], causal, HEAD_DIM)
    for pair, (oa, ob) in enumerate(fox):
        o_ref[0, :, pair_lanes(pair)] = jnp.concatenate([oa, ob], axis=0).T.astype(o_ref.dtype)

    lam = _lambda_value(lv_ref, lam_init)
    diff = attend([diff_unit(hd) for hd in range(DIFF_HEADS)], chunk_causal, PAIR)
    for hd, (o1, o2) in enumerate(diff):
        d = o1 - lam * o2
        ms = jnp.mean(d * d, axis=0, keepdims=True)
        dn = (d * lax.rsqrt(ms + RMS_EPS)).T * g_ref[...] * (1.0 - lam_init)
        o_ref[0, :, WIDTH + hd * PAIR:WIDTH + (hd + 1) * PAIR] = dn.astype(o_ref.dtype)


def _attn_prompt(fq_t, fk, e, fv_t, dq_t, dk, dv_t, cq_t, lambda_vecs, subln_g, lam_init):
    B, T, _ = fk.shape
    tq = KV_TILE
    nt = T // tq
    qspec = pl.BlockSpec((1, 1, WIDTH, tq), lambda b, i: (b, i, 0, 0))
    vspec = pl.BlockSpec((1, nt, WIDTH, tq), lambda b, i: (b, 0, 0, 0))
    kspec = pl.BlockSpec((1, T, WIDTH), lambda b, i: (b, 0, 0))
    return pl.pallas_call(
        functools.partial(_attn_prompt_kernel, lam_init=lam_init),
        grid=(B, nt),
        in_specs=[qspec, kspec, pl.BlockSpec((1, T, LANES), lambda b, i: (b, 0, 0)), vspec,
                  qspec, kspec, vspec,
                  pl.BlockSpec((1, FOX_HEADS, tq), lambda b, i: (b, 0, i)),
                  pl.BlockSpec(lambda_vecs.shape, lambda b, i: (0, 0)),
                  pl.BlockSpec(subln_g.shape, lambda b, i: (0, 0))],
        out_specs=pl.BlockSpec((1, tq, 2 * WIDTH), lambda b, i: (b, i, 0)),
        out_shape=jax.ShapeDtypeStruct((B, T, 2 * WIDTH), BF16),
        scratch_shapes=[pltpu.VMEM((2 * N_PAIRS, tq, tq), F32),
                        pltpu.VMEM((2 * N_PAIRS, PAIR, tq), F32)],
        compiler_params=_params("arbitrary", "arbitrary"),
        name="attn_prompt",
    )(fq_t, fk, e, fv_t, dq_t, dk, dv_t, cq_t, lambda_vecs, subln_g)


def _attn_sample_kernel(fq_ref, fk_ref, fv_ref, dq_ref, dk_ref, dv_ref,
                        pfk_ref, pfv_ref, pdk_ref, pdv_ref, cq_ref, ck_ref,
                        lv_ref, g_ref, o_ref, *, lam_init):
    tq = fq_ref.shape[1]
    P = pfk_ref.shape[1]
    tk = KV_TILE
    low = lax.broadcasted_iota(jnp.int32, (1, PAIR), 1) < HEAD_DIM
    row = lax.broadcasted_iota(jnp.int32, (tq, tq), 0)
    col = lax.broadcasted_iota(jnp.int32, (tq, tq), 1)
    causal = col <= row
    chunk_causal = ((col + P) // CHUNK) <= ((row + P) // CHUNK)

    def run(qm, pk_ref, pv_ref, k_ref, v_ref, lanes, head, mask):
        cq = None if head is None else cq_ref[0, :, head:head + 1]

        def past(j, carry):
            start = pl.multiple_of(j * tk, tk)
            k = pk_ref[0, pl.ds(start, tk), lanes].astype(BF16)
            v = pv_ref[0, pl.ds(start, tk), lanes].astype(BF16)
            ck = None if head is None else ck_ref[0, head:head + 1, pl.ds(start, tk)]
            return _flash_step(qm, k, v, cq, ck, None, carry)

        carry = lax.fori_loop(0, P // tk, past, _flash_init(tq))
        ck = None if head is None else ck_ref[0, head:head + 1, P:P + tq]
        return _flash_step(qm, k_ref[0, :, lanes], v_ref[0, :, lanes], cq, ck, mask, carry)

    for pair in range(N_PAIRS):
        lanes = slice(pair * PAIR, (pair + 1) * PAIR)
        qp = fq_ref[0, :, lanes]
        zero = jnp.zeros_like(qp)
        _, l0, a0 = run(jnp.where(low, qp, zero), pfk_ref, pfv_ref, fk_ref, fv_ref, lanes, 2 * pair, causal)
        _, l1, a1 = run(jnp.where(low, zero, qp), pfk_ref, pfv_ref, fk_ref, fv_ref, lanes, 2 * pair + 1, causal)
        o_ref[0, :, lanes] = jnp.where(low, a0 / l0, a1 / l1).astype(o_ref.dtype)

    lam = _lambda_value(lv_ref, lam_init)
    for hd in range(DIFF_HEADS):
        lanes = slice(hd * PAIR, (hd + 1) * PAIR)
        qp = dq_ref[0, :, lanes]
        zero = jnp.zeros_like(qp)
        _, l1, a1 = run(jnp.where(low, qp, zero), pdk_ref, pdv_ref, dk_ref, dv_ref, lanes, None, chunk_causal)
        _, l2, a2 = run(jnp.where(low, zero, qp), pdk_ref, pdv_ref, dk_ref, dv_ref, lanes, None, chunk_causal)
        dn = _subnorm(a1, l1, a2, l2, lam, g_ref, lam_init)
        o_ref[0, :, WIDTH + hd * PAIR:WIDTH + (hd + 1) * PAIR] = dn.astype(o_ref.dtype)


def _attn_sample(fq, fk, fv, dq, dk, dv, pfk, pfv, pdk, pdv, cq, ck_t, lambda_vecs, subln_g, lam_init):
    B, T, _ = fq.shape
    P = pfk.shape[1]
    S = ck_t.shape[2]
    new = pl.BlockSpec((1, T, WIDTH), lambda b: (b, 0, 0))
    past = pl.BlockSpec((1, P, WIDTH), lambda b: (b, 0, 0))
    return pl.pallas_call(
        functools.partial(_attn_sample_kernel, lam_init=lam_init),
        grid=(B,),
        in_specs=[new] * 6 + [past] * 4 + [
            pl.BlockSpec((1, T, FOX_HEADS), lambda b: (b, 0, 0)),
            pl.BlockSpec((1, FOX_HEADS, S), lambda b: (b, 0, 0)),
            pl.BlockSpec(lambda_vecs.shape, lambda b: (0, 0)),
            pl.BlockSpec(subln_g.shape, lambda b: (0, 0))],
        out_specs=pl.BlockSpec((1, T, 2 * WIDTH), lambda b: (b, 0, 0)),
        out_shape=jax.ShapeDtypeStruct((B, T, 2 * WIDTH), BF16),
        compiler_params=_params("arbitrary"),
        name="attn_sample",
    )(fq, fk, fv, dq, dk, dv, pfk, pfv, pdk, pdv, cq, ck_t, lambda_vecs, subln_g)


def _post_norm(x, h, gate, g, b, alpha):
    y = alpha * x + gate * h
    mu = jnp.mean(y, axis=-1, keepdims=True)
    yc = y - mu
    var = jnp.mean(yc * yc, axis=-1, keepdims=True)
    return yc * lax.rsqrt(var + LN_EPS) * g + b


def _post_kernel(x_ref, o_ref, mod_ref, prev_ref, wo_ref, ln1g_ref, ln1b_ref,
                 wup_ref, cw_ref, cb_ref, wdn_ref, ln2g_ref, ln2b_ref,
                 y_ref, conv_ref, carry_ref, acc_ref, *, alpha):
    bb, tt, d = x_ref.shape
    rows = bb * tt
    keep = CONV_WIDTH - 1
    t = pl.program_id(1)

    @pl.when(t == 0)
    def _():
        carry_ref[...] = prev_ref[...]

    m = mod_ref[...]

    def rows3(v):
        return jnp.broadcast_to(v, (bb, tt, v.shape[-1])).reshape(rows, v.shape[-1])

    x = x_ref[...].reshape(rows, d)
    h = jnp.dot(o_ref[...].reshape(rows, d), wo_ref[...], preferred_element_type=F32)
    x1 = _post_norm(x, h, rows3(m[:, 2:3, :]), ln1g_ref[...], ln1b_ref[...], alpha)
    u2 = (x1 * (1.0 + rows3(m[:, 4:5, :])) + rows3(m[:, 3:4, :])).astype(BF16)

    tpos = lax.broadcasted_iota(jnp.int32, (bb, tt, 1), 1).reshape(rows, 1)
    acc_ref[...] = jnp.zeros_like(acc_ref)
    for c in range(D_FF // FF_TILE):
        cols = slice(c * FF_TILE, (c + 1) * FF_TILE)
        a = jnp.dot(u2, wup_ref[:, cols], preferred_element_type=F32)
        g = jnp.dot(u2, wup_ref[:, D_FF + c * FF_TILE:D_FF + (c + 1) * FF_TILE],
                    preferred_element_type=F32)
        prev = carry_ref[:, :, cols]
        p2 = rows3(prev[:, 0:1, :])
        p1 = rows3(prev[:, 1:2, :])
        am1 = jnp.where(tpos == 0, p1, pltpu.roll(a, 1, 0))
        am2 = jnp.where(tpos == 0, p2, jnp.where(tpos == 1, p1, pltpu.roll(a, 2, 0)))
        cw = cw_ref[:, cols]
        conv = cb_ref[:, cols] + am2 * cw[0:1] + am1 * cw[1:2] + a * cw[2:3]
        hid = conv * (1.0 / (1.0 + jnp.exp(-conv))) * g
        acc_ref[...] += jnp.dot(hid.astype(BF16), wdn_ref[cols, :], preferred_element_type=F32)
        last = a.reshape(bb, tt, FF_TILE)[:, tt - keep:, :]
        carry_ref[:, :, cols] = last
        conv_ref[:, :, cols] = last

    y = _post_norm(x1, acc_ref[...], rows3(m[:, 5:6, :]), ln2g_ref[...], ln2b_ref[...], alpha)
    y_ref[...] = y.reshape(bb, tt, d)


def _post(x, o, mod, conv_prev, wo, ln1g, ln1b, wup, cw, cb, wdn, ln2g, ln2b, bb, tt, alpha):
    B, T, d = x.shape
    keep = CONV_WIDTH - 1
    tok = pl.BlockSpec((bb, tt, d), lambda b, t: (b, t, 0))
    per_b = lambda r, w: pl.BlockSpec((bb, r, w), lambda b, t: (b, 0, 0))
    return pl.pallas_call(
        functools.partial(_post_kernel, alpha=alpha),
        grid=(B // bb, T // tt),
        in_specs=[tok, tok, per_b(6, d), per_b(keep, D_FF),
                  _resident(wo.shape), _resident(ln1g.shape), _resident(ln1b.shape),
                  _resident(wup.shape), _resident(cw.shape), _resident(cb.shape),
                  _resident(wdn.shape), _resident(ln2g.shape), _resident(ln2b.shape)],
        out_specs=[tok, per_b(keep, D_FF)],
        out_shape=[jax.ShapeDtypeStruct((B, T, d), F32),
                   jax.ShapeDtypeStruct((B, keep, D_FF), F32)],
        scratch_shapes=[pltpu.VMEM((bb, keep, D_FF), F32),
                        pltpu.VMEM((bb * tt, d), F32)],
        compiler_params=_params("arbitrary", "arbitrary"),
        name="post",
    )(x, o, mod, conv_prev, wo, ln1g, ln1b, wup, cw, cb, wdn, ln2g, ln2b)


def _rope_tables(pos0, T):
    half = HEAD_DIM // 2
    inv = ROPE_THETA ** (-jnp.arange(0, HEAD_DIM, 2, dtype=F32) / HEAD_DIM)
    ang = (pos0 + jnp.arange(T)).astype(F32)[:, None] * inv[None, :]
    cos, sin = jnp.cos(ang), jnp.sin(ang)
    reps = WIDTH // HEAD_DIM
    return (jnp.tile(jnp.concatenate([cos, cos], axis=1), (1, reps)),
            jnp.tile(jnp.concatenate([-sin, sin], axis=1), (1, reps)))


def _round_up(n, k):
    return -(-n // k) * k


def _layer(x, mod, past, w, lam_init, alpha):
    B, T, d = x.shape
    P = 0 if past is None else past[0].shape[1]
    if T % TOKEN_TILE == 0:
        bb, tt = 1, TOKEN_TILE
    else:
        tt = T
        bb = math.gcd(B, max(1, TOKEN_TILE // T))
    cos_t, sin_t = _rope_tables(P, T)
    if bb > 1:
        cos_t, sin_t = jnp.tile(cos_t, (bb, 1)), jnp.tile(sin_t, (bb, 1))
    (fk, fv, lf, dk, dv, fqb, fkb, fvb, dqb, dkb, dvb) = _inproj(
        x, mod, w["w_in"], w["b_f"], cos_t, sin_t, bb, tt, transposed=past is None)

    lf_t = jnp.swapaxes(lf, 1, 2)
    if past is None:
        cq_t, e = _cum(lf_t, with_bias=True)
        o = _attn_prompt(fqb, fkb, e, fvb, dqb, dkb, dvb, cq_t,
                         w["lambda_vecs"], w["subln_g"], lam_init)
        conv_prev = jnp.zeros((B, CONV_WIDTH - 1, D_FF), F32)
    else:
        pfk, pfv, plf, pdk, pdv, conv_prev = past
        S = _round_up(P + T, MXU_DIM)
        lf_all = jnp.concatenate(
            [jnp.swapaxes(plf, 1, 2), lf_t, jnp.zeros((B, FOX_HEADS, S - P - T), F32)], axis=2)
        ck_t = _cum(lf_all, with_bias=False)
        cq = jnp.swapaxes(ck_t[:, :, P:P + T], 1, 2)
        o = _attn_sample(fqb, fkb, fvb, dqb, dkb, dvb,
                         pfk.reshape(B, P, WIDTH), pfv.reshape(B, P, WIDTH),
                         pdk.reshape(B, P, WIDTH), pdv.reshape(B, P, WIDTH),
                         cq, ck_t, w["lambda_vecs"], w["subln_g"], lam_init)

    y, conv = _post(x, o, mod, conv_prev, w["w_o"], w["ln1_g"], w["ln1_b"], w["w_up"],
                    w["conv_w"], w["conv_b"], w["w_down"], w["ln2_g"], w["ln2_b"], bb, tt, alpha)
    state = (fk.reshape(B, T, FOX_HEADS, HEAD_DIM), fv.reshape(B, T, FOX_HEADS, HEAD_DIM), lf,
             dk.reshape(B, T, 2 * DIFF_HEADS, HEAD_DIM), dv.reshape(B, T, DIFF_HEADS, 2 * HEAD_DIM), conv)
    return y, state


def kernel(x_prompt, x_sample, c_prompt, c_sample, cache_fox_k, cache_fox_v, cache_fox_logf, cache_diff_k, cache_diff_v, state_ffn_conv, w_ada, b_ada, w_in, b_f, lambda_vecs, subln_g, w_o, ln1_g, ln1_b, w_up, conv_w, conv_b, w_down, ln2_g, ln2_b):
    depth = w_ada.shape[0]
    alpha = (2 * depth) ** 0.25
    nb = c_prompt.shape[0]
    yp, ys = x_prompt, x_sample
    c_all = jnp.concatenate([c_prompt, c_sample], axis=0)
    p_states, s_states = [], []
    for l in range(depth):
        lam_init = 0.8 - 0.6 * math.exp(-0.3 * l)
        w_in_l = w_in[l]
        gate_w = jnp.pad(w_in_l[:, OFF_FF:OFF_FF + FOX_HEADS], ((0, 0), (0, GATE_COLS - FOX_HEADS)))
        w = {
            "w_in": jnp.concatenate(
                [w_in_l[:, :OFF_FF], gate_w, w_in_l[:, OFF_FF + FOX_HEADS:]], axis=1).astype(BF16),
            "b_f": jnp.pad(b_f[l], (0, GATE_COLS - FOX_HEADS)).reshape(1, GATE_COLS),
            "lambda_vecs": lambda_vecs[l],
            "subln_g": subln_g[l].reshape(1, PAIR),
            "w_o": w_o[l].astype(BF16),
            "ln1_g": ln1_g[l].reshape(1, D_MODEL), "ln1_b": ln1_b[l].reshape(1, D_MODEL),
            "w_up": w_up[l].astype(BF16),
            "conv_w": conv_w[l], "conv_b": conv_b[l].reshape(1, D_FF),
            "w_down": w_down[l].astype(BF16),
            "ln2_g": ln2_g[l].reshape(1, D_MODEL), "ln2_b": ln2_b[l].reshape(1, D_MODEL),
        }
        mod = _ada(c_all, w_ada[l], b_ada[l]).reshape(c_all.shape[0], 6, D_MODEL)
        yp, st_p = _layer(yp, mod[:nb], None, w, lam_init, alpha)
        past = (cache_fox_k[l], cache_fox_v[l], cache_fox_logf[l], cache_diff_k[l], cache_diff_v[l],
                state_ffn_conv[l])
        ys, st_s = _layer(ys, mod[nb:], past, w, lam_init, alpha)
        p_states.append(st_p)
        s_states.append(st_s)
    p_out = [jnp.stack(a, axis=0) for a in zip(*p_states)]
    s_out = [jnp.stack(a, axis=0) for a in zip(*s_states)]
    return (yp, ys, *p_out, *s_out)
```

```python
import functools
import math

import jax
import jax.numpy as jnp
from jax import lax
from jax.experimental import pallas as pl
from jax.experimental.pallas import tpu as pltpu

F32 = jnp.float32
BF16 = jnp.bfloat16

D_MODEL = 1024
CHUNK = 64
FOX_HEADS = 8
DIFF_HEADS = 4
HEAD_DIM = 64
WIDTH = 512
PAIR = 2 * HEAD_DIM
N_PAIRS = WIDTH // PAIR
D_FF = 2816
CONV_WIDTH = 3
ROPE_THETA = 10000.0
LN_EPS = 1e-5
RMS_EPS = 1e-6
NEG_BIG = -1e30
Q_SCALE = HEAD_DIM ** -0.5
LOG2E = math.log2(math.e)

LANES = 128
MXU_DIM = 256
VMEM_LIMIT_BYTES = 56 * 1024 * 1024

GATE_COLS = LANES
OFF_FQ, OFF_FK, OFF_FV = 0, WIDTH, 2 * WIDTH
OFF_FF = 3 * WIDTH
OFF_DQ = OFF_FF + GATE_COLS
OFF_DK, OFF_DV = OFF_DQ + WIDTH, OFF_DQ + 2 * WIDTH
IN_COLS_PADDED = OFF_DV + WIDTH

KV_TILE = 256
TOKEN_TILE = 512
FF_TILE = 256


def _params(*sem):
    return pltpu.CompilerParams(dimension_semantics=sem, vmem_limit_bytes=VMEM_LIMIT_BYTES)


def _resident(shape):
    nd = len(shape)
    return pl.BlockSpec(shape, lambda *_: (0,) * nd, pipeline_mode=pl.Buffered(1))


def _ada_kernel(c_ref, w_ref, b_ref, o_ref):
    c = c_ref[...]
    s = c * (1.0 / (1.0 + jnp.exp(-c)))
    o_ref[...] = jnp.dot(s.astype(BF16), w_ref[...].astype(BF16),
                         preferred_element_type=F32) + b_ref[...]


def _ada(c, w_ada, b_ada):
    n, d = c.shape
    cols = w_ada.shape[1]
    tn = cols // 4
    return pl.pallas_call(
        _ada_kernel,
        grid=(cols // tn,),
        in_specs=[pl.BlockSpec((n, d), lambda j: (0, 0)),
                  pl.BlockSpec((d, tn), lambda j: (0, j)),
                  pl.BlockSpec((1, tn), lambda j: (0, j))],
        out_specs=pl.BlockSpec((n, tn), lambda j: (0, j)),
        out_shape=jax.ShapeDtypeStruct((n, cols), F32),
        compiler_params=_params("arbitrary"),
        name="ada",
    )(c, w_ada, b_ada.reshape(1, cols))


def _inproj_kernel(x_ref, mod_ref, w_ref, bf_ref, cos_ref, sin_ref,
                   fk_o, fv_o, lf_o, dk_o, dv_o,
                   fqb_o, fkb_o, fvb_o, dqb_o, dkb_o, dvb_o, *, transposed):
    bb, tt, d = x_ref.shape
    m = mod_ref[...]
    u = x_ref[...] * (1.0 + m[:, 1:2, :]) + m[:, 0:1, :]
    u = u.reshape(bb * tt, d).astype(BF16)
    q_scale = Q_SCALE * LOG2E

    def proj(off, width):
        return jnp.dot(u, w_ref[:, off:off + width], preferred_element_type=F32)

    def put(o_ref, v):
        o_ref[...] = v.reshape(o_ref.shape).astype(o_ref.dtype)

    def put_qv(o_ref, v):
        if transposed:
            for c in range(tt // KV_TILE):
                o_ref[0, c] = v[c * KV_TILE:(c + 1) * KV_TILE, :].T.astype(o_ref.dtype)
        else:
            put(o_ref, v)

    fq = proj(OFF_FQ, WIDTH)
    put_qv(fqb_o, fq * q_scale)
    fk = proj(OFF_FK, WIDTH)
    put(fk_o, fk)
    put(fkb_o, fk)
    fv = proj(OFF_FV, WIDTH)
    put(fv_o, fv)
    put_qv(fvb_o, fv)

    zf = proj(OFF_FF, GATE_COLS) + bf_ref[...]
    lf = jnp.minimum(zf, 0.0) - jnp.log1p(jnp.exp(-jnp.abs(zf)))
    put(lf_o, lf[:, :FOX_HEADS])

    cos = cos_ref[...]
    sin = sin_ref[...]
    lane = lax.broadcasted_iota(jnp.int32, (1, WIDTH), 1)
    first_half = (lane % HEAD_DIM) < (HEAD_DIM // 2)

    def rope(v):
        partner = jnp.where(first_half, pltpu.roll(v, WIDTH - HEAD_DIM // 2, 1),
                            pltpu.roll(v, HEAD_DIM // 2, 1))
        return v * cos + partner * sin

    dq = rope(proj(OFF_DQ, WIDTH))
    put_qv(dqb_o, dq * q_scale)
    dk = rope(proj(OFF_DK, WIDTH))
    put(dk_o, dk)
    put(dkb_o, dk)
    dv = proj(OFF_DV, WIDTH)
    put(dv_o, dv)
    put_qv(dvb_o, dv)


def _inproj(x, mod, w_pad, bf_pad, cos_t, sin_t, bb, tt, transposed):
    B, T, d = x.shape
    rows = bb * tt
    grid = (B // bb, T // tt)
    assert not transposed or (bb == 1 and tt % KV_TILE == 0)
    tok = lambda w: pl.BlockSpec((bb, tt, w), lambda b, t: (b, t, 0))
    tab = pl.BlockSpec((rows, WIDTH), (lambda b, t: (t, 0)) if bb == 1 else (lambda b, t: (0, 0)))
    sds = lambda w, dt: jax.ShapeDtypeStruct((B, T, w), dt)
    if transposed:
        qv = pl.BlockSpec((1, tt // KV_TILE, WIDTH, KV_TILE), lambda b, t: (b, t, 0, 0))
        qv_sds = jax.ShapeDtypeStruct((B, T // KV_TILE, WIDTH, KV_TILE), BF16)
    else:
        qv, qv_sds = tok(WIDTH), sds(WIDTH, BF16)
    return pl.pallas_call(
        functools.partial(_inproj_kernel, transposed=transposed),
        grid=grid,
        in_specs=[tok(d),
                  pl.BlockSpec((bb, 6, d), lambda b, t: (b, 0, 0)),
                  _resident(w_pad.shape),
                  _resident(bf_pad.shape),
                  tab, tab],
        out_specs=[tok(WIDTH), tok(WIDTH), tok(FOX_HEADS), tok(WIDTH), tok(WIDTH),
                   qv, tok(WIDTH), qv, qv, tok(WIDTH), qv],
        out_shape=[sds(WIDTH, F32), sds(WIDTH, F32), sds(FOX_HEADS, F32), sds(WIDTH, F32), sds(WIDTH, F32),
                   qv_sds, sds(WIDTH, BF16), qv_sds, qv_sds, sds(WIDTH, BF16), qv_sds],
        compiler_params=_params("arbitrary", "arbitrary"),
        name="inproj",
    )(x, mod, w_pad, bf_pad, cos_t, sin_t)


def _split3(x):
    hi = x.astype(BF16).astype(F32)
    r1 = x - hi
    mid = r1.astype(BF16).astype(F32)
    lo = (r1 - mid).astype(BF16).astype(F32)
    return hi, mid, lo


def _cum_kernel(x_ref, o_ref, *maybe_e_ref):
    S = x_ref.shape[2]
    r = lax.broadcasted_iota(jnp.int32, (MXU_DIM, MXU_DIM), 0)
    c = lax.broadcasted_iota(jnp.int32, (MXU_DIM, MXU_DIM), 1)
    tri = jnp.where(r <= c, 1.0, 0.0).astype(BF16)
    carry = jnp.zeros((FOX_HEADS, 1), F32)
    for k in range(S // MXU_DIM):
        sl = slice(k * MXU_DIM, (k + 1) * MXU_DIM)
        hi, mid, lo = _split3(x_ref[0, :, sl])
        part = (jnp.dot(hi.astype(BF16), tri, preferred_element_type=F32)
                + jnp.dot(mid.astype(BF16), tri, preferred_element_type=F32)
                + jnp.dot(lo.astype(BF16), tri, preferred_element_type=F32))
        part = part + carry
        carry = part[:, MXU_DIM - 1:MXU_DIM]
        if not maybe_e_ref:
            o_ref[0, :, sl] = part
            continue
        scaled = part * LOG2E
        o_ref[0, :, sl] = scaled
        pieces = _split3(-scaled)
        pad = jnp.zeros((LANES - 3 * FOX_HEADS, MXU_DIM), F32)
        maybe_e_ref[0][0, sl, :] = jnp.concatenate(pieces + (pad,), axis=0).T.astype(BF16)


def _cum(lf_t, with_bias):
    B, H, S = lf_t.shape
    spec = pl.BlockSpec((1, H, S), lambda b: (b, 0, 0))
    out_specs, out_shape = [spec], [jax.ShapeDtypeStruct((B, H, S), F32)]
    if with_bias:
        out_specs.append(pl.BlockSpec((1, S, LANES), lambda b: (b, 0, 0)))
        out_shape.append(jax.ShapeDtypeStruct((B, S, LANES), BF16))
    out = pl.pallas_call(
        _cum_kernel, grid=(B,), in_specs=[spec], out_specs=out_specs, out_shape=out_shape,
        compiler_params=_params("arbitrary"),
        name="cum",
    )(lf_t)
    return out if with_bias else out[0]


def _lambda_value(lv_ref, lam_init):
    lv = lv_ref[...]
    a = jnp.sum(lv[0:1] * lv[1:2], axis=1, keepdims=True)
    b = jnp.sum(lv[2:3] * lv[3:4], axis=1, keepdims=True)
    return jnp.exp(a) - jnp.exp(b) + lam_init


def _attn_prompt_kernel(fq_ref, fk_ref, e_ref, fv_ref, dq_ref, dk_ref, dv_ref, cq_ref,
                        lv_ref, g_ref, o_ref, s_ref, acc_ref, *, lam_init):
    tq = tk = KV_TILE
    i = pl.program_id(1)
    srow = lax.broadcasted_iota(jnp.int32, (PAIR, tq), 0)
    first = srow < HEAD_DIM
    krow = lax.broadcasted_iota(jnp.int32, (tk, tq), 0)
    qcol = lax.broadcasted_iota(jnp.int32, (tk, tq), 1)
    causal = krow <= qcol
    chunk_causal = (krow // CHUNK) <= (qcol // CHUNK)

    def attend(units, mask, rows):
        chains = []
        for wa, wb, cqa, cqb, keys, vta, vtb in units:
            chains += [(wa, cqa, keys, vta), (wb, cqb, keys, vtb)]
        n = len(chains)

        def score(c, j):
            w, _, keys, _ = chains[c]
            s = jnp.dot(keys(j), w, preferred_element_type=F32)
            s_ref[c] = s
            return jnp.max(s, axis=0, keepdims=True)

        def consume(c, j, mb, msk, m, l):
            _, cq, _, vt = chains[c]
            s = s_ref[c]
            if msk is not None:
                s = jnp.where(msk, s, NEG_BIG)
                mb = jnp.max(s, axis=0, keepdims=True)
            m_new = jnp.maximum(m, mb if cq is None else mb + cq)
            p = jnp.exp2(s - (m_new if cq is None else m_new - cq))
            alpha = jnp.exp2(m - m_new)
            acc_ref[c, :rows] = alpha * acc_ref[c, :rows] + jnp.dot(
                vt(j), p.astype(BF16), preferred_element_type=F32)
            return m_new, alpha * l + jnp.sum(p, axis=0, keepdims=True)

        def body(j, carry):
            ms, ls, mbs = carry
            ms, ls, mbs = list(ms), list(ls), list(mbs)
            for c in range(n):
                ms[c], ls[c] = consume(c, j, mbs[c], None, ms[c], ls[c])
                mbs[c] = score(c, j + 1)
            return tuple(ms), tuple(ls), tuple(mbs)

        acc_ref[...] = jnp.zeros_like(acc_ref)
        ms, ls, _ = lax.fori_loop(
            0, i, body,
            (tuple(jnp.full((1, tq), NEG_BIG, F32) for _ in range(n)),
             tuple(jnp.zeros((1, tq), F32) for _ in range(n)),
             tuple(score(c, 0) for c in range(n))))
        ls = [consume(c, i, None, mask, ms[c], ls[c])[1] for c in range(n)]
        outs = [acc_ref[c, :rows] / ls[c] for c in range(n)]
        return list(zip(outs[0::2], outs[1::2]))

    def rows_of(j):
        return pl.ds(pl.multiple_of(j * tk, tk), tk)

    def select(head):
        return jnp.where((srow % FOX_HEADS == head) & (srow < 3 * FOX_HEADS), 1.0, 0.0).astype(BF16)

    def pair_lanes(p):
        return slice(p * PAIR, (p + 1) * PAIR)

    zero = jnp.zeros((PAIR, tq), BF16)

    def fox_unit(pair):
        lanes = pair_lanes(pair)
        qt = fq_ref[0, 0, lanes, :]
        return (jnp.concatenate([jnp.where(first, qt, zero), select(2 * pair)], axis=0),
                jnp.concatenate([jnp.where(first, zero, qt), select(2 * pair + 1)], axis=0),
                cq_ref[0, 2 * pair:2 * pair + 1, :], cq_ref[0, 2 * pair + 1:2 * pair + 2, :],
                lambda j: jnp.concatenate([fk_ref[0, rows_of(j), lanes], e_ref[0, rows_of(j), :]], axis=1),
                lambda j: fv_ref[0, j, pair * PAIR:pair * PAIR + HEAD_DIM, :],
                lambda j: fv_ref[0, j, pair * PAIR + HEAD_DIM:(pair + 1) * PAIR, :])

    def diff_unit(hd):
        lanes = pair_lanes(hd)
        qt = dq_ref[0, 0, lanes, :]
        vt = lambda j: dv_ref[0, j, lanes, :]
        return (jnp.where(first, qt, zero), jnp.where(first, zero, qt), None, None,
                lambda j: dk_ref[0, rows_of(j), lanes], vt, vt)

    fox = attend([fox_unit(p) for p in range(N_PAIRS)], causal, HEAD_DIM)
    for pair, (oa, ob) in enumerate(fox):
        o_ref[0, :, pair_lanes(pair)] = jnp.concatenate([oa, ob], axis=0).T.astype(o_ref.dtype)

    lam = _lambda_value(lv_ref, lam_init)
    diff = attend([diff_unit(hd) for hd in range(DIFF_HEADS)], chunk_causal, PAIR)
    for hd, (o1, o2) in enumerate(diff):
        d = o1 - lam * o2
        ms = jnp.mean(d * d, axis=0, keepdims=True)
        dn = (d * lax.rsqrt(ms + RMS_EPS)).T * g_ref[...] * (1.0 - lam_init)
        o_ref[0, :, WIDTH + hd * PAIR:WIDTH + (hd + 1) * PAIR] = dn.astype(o_ref.dtype)


def _attn_prompt(fq_t, fk, e, fv_t, dq_t, dk, dv_t, cq_t, lambda_vecs, subln_g, lam_init):
    B, T, _ = fk.shape
    tq = KV_TILE
    nt = T // tq
    qspec = pl.BlockSpec((1, 1, WIDTH, tq), lambda b, i: (b, i, 0, 0))
    vspec = pl.BlockSpec((1, nt, WIDTH, tq), lambda b, i: (b, 0, 0, 0))
    kspec = pl.BlockSpec((1, T, WIDTH), lambda b, i: (b, 0, 0))
    return pl.pallas_call(
        functools.partial(_attn_prompt_kernel, lam_init=lam_init),
        grid=(B, nt),
        in_specs=[qspec, kspec, pl.BlockSpec((1, T, LANES), lambda b, i: (b, 0, 0)), vspec,
                  qspec, kspec, vspec,
                  pl.BlockSpec((1, FOX_HEADS, tq), lambda b, i: (b, 0, i)),
                  pl.BlockSpec(lambda_vecs.shape, lambda b, i: (0, 0)),
                  pl.BlockSpec(subln_g.shape, lambda b, i: (0, 0))],
        out_specs=pl.BlockSpec((1, tq, 2 * WIDTH), lambda b, i: (b, i, 0)),
        out_shape=jax.ShapeDtypeStruct((B, T, 2 * WIDTH), BF16),
        scratch_shapes=[pltpu.VMEM((2 * N_PAIRS, tq, tq), F32),
                        pltpu.VMEM((2 * N_PAIRS, PAIR, tq), F32)],
        compiler_params=_params("arbitrary", "arbitrary"),
        name="attn_prompt",
    )(fq_t, fk, e, fv_t, dq_t, dk, dv_t, cq_t, lambda_vecs, subln_g)


def _attn_sample_kernel(fq_ref, fk_ref, fv_ref, dq_ref, dk_ref, dv_ref,
                        pfk_ref, pfv_ref, pdk_ref, pdv_ref, e_ref, cq_ref,
                        lv_ref, g_ref, o_ref, s_ref, snew_ref, acc_ref, *, lam_init):
    T = fq_ref.shape[1]
    P = pfk_ref.shape[1]
    tk = KV_TILE
    n_past = P // tk
    ncol = FOX_HEADS * T
    half = ncol // 2
    new_rows = LANES
    f_idx, d_idx = 0, 1

    row = lambda shape: lax.broadcasted_iota(jnp.int32, shape, 0)
    col = lambda shape: lax.broadcasted_iota(jnp.int32, shape, 1)

    spread = jnp.where(row((LANES, ncol)) == col((LANES, ncol)) % T, 1.0, 0.0).astype(BF16)
    own_head = row((WIDTH, ncol)) // HEAD_DIM == col((WIDTH, ncol)) // T
    select = jnp.where((row((LANES, ncol)) % FOX_HEADS == col((LANES, ncol)) // T)
                       & (row((LANES, ncol)) < 3 * FOX_HEADS), 1.0, 0.0).astype(BF16)

    def pad_rows(x):
        return jnp.concatenate([x, jnp.zeros((new_rows - T, x.shape[1]), x.dtype)], axis=0)

    def transposed(x):
        return pad_rows(x).astype(F32).T.astype(BF16)

    def block_diag(q_ref):
        full = jnp.dot(transposed(q_ref[0]), spread, preferred_element_type=F32)
        return jnp.where(own_head, full, 0.0).astype(BF16)

    w_f = jnp.concatenate([block_diag(fq_ref), select], axis=0)
    w_d = block_diag(dq_ref)
    cq = cq_ref[0]

    def past_scores(j, carry):
        mf, md = carry
        rows = pl.ds(pl.multiple_of(j * tk, tk), tk)
        sf = jnp.dot(jnp.concatenate([pfk_ref[0, rows, :], e_ref[0, rows, :]], axis=1), w_f,
                     preferred_element_type=F32)
        sd = jnp.dot(pdk_ref[0, rows, :], w_d, preferred_element_type=F32)
        s_ref[f_idx, j] = sf
        s_ref[d_idx, j] = sd
        return (jnp.maximum(mf, jnp.max(sf, axis=0, keepdims=True)),
                jnp.maximum(md, jnp.max(sd, axis=0, keepdims=True)))

    neg = jnp.full((1, ncol), NEG_BIG, F32)
    mf, md = lax.fori_loop(0, n_past, past_scores, (neg, neg), unroll=2)

    krow = row((new_rows, ncol))
    qpos = col((new_rows, ncol)) % T
    valid = krow < T
    causal = valid & (krow <= qpos)
    chunk_causal = valid & ((krow + P) // CHUNK <= (qpos + P) // CHUNK)
    sf = jnp.dot(jnp.concatenate([pad_rows(fk_ref[0]), e_ref[0, P:P + new_rows, :]], axis=1), w_f,
                 preferred_element_type=F32)
    sf = jnp.where(causal, sf, NEG_BIG)
    sd = jnp.where(chunk_causal, jnp.dot(pad_rows(dk_ref[0]), w_d, preferred_element_type=F32), NEG_BIG)
    snew_ref[f_idx] = sf
    snew_ref[d_idx] = sd
    mf = jnp.maximum(mf, jnp.max(sf, axis=0, keepdims=True)) + cq
    shift_f = mf - cq
    shift_d = jnp.maximum(md, jnp.max(sd, axis=0, keepdims=True))

    def apply(idx, s, shift, vt, l):
        p = jnp.exp2(s - shift)
        pb = p.astype(BF16)
        acc_ref[idx, 0] += jnp.dot(vt[:WIDTH // 2], pb[:, :half], preferred_element_type=F32)
        acc_ref[idx, 1] += jnp.dot(vt[WIDTH // 2:], pb[:, half:], preferred_element_type=F32)
        return l + jnp.sum(p, axis=0, keepdims=True)

    def past_apply(j, carry):
        lf, ld = carry
        lf = apply(f_idx, s_ref[f_idx, j], shift_f, pfv_ref[0, j], lf)
        ld = apply(d_idx, s_ref[d_idx, j], shift_d, pdv_ref[0, j], ld)
        return lf, ld

    acc_ref[...] = jnp.zeros_like(acc_ref)
    zero = jnp.zeros((1, ncol), F32)
    lf, ld = lax.fori_loop(0, n_past, past_apply, (zero, zero), unroll=2)
    lf = apply(f_idx, snew_ref[f_idx], shift_f, transposed(fv_ref[0]), lf)
    ld = apply(d_idx, snew_ref[d_idx], shift_d, transposed(dv_ref[0]), ld)

    fold = jnp.where(row((half, LANES)) % T == col((half, LANES)), 1.0, 0.0).astype(BF16)
    hrow = row((WIDTH // 2, half))
    hcol = col((WIDTH // 2, half))

    def finish(kept_halves, lane0):
        out_t = jnp.concatenate(
            [jnp.dot(k.astype(BF16), fold, preferred_element_type=F32) for k in kept_halves], axis=0)
        o_ref[0, :, lane0:lane0 + WIDTH] = out_t.T[:T].astype(o_ref.dtype)

    inv_f = 1.0 / lf
    finish([jnp.where(hrow // HEAD_DIM == hcol // T, acc_ref[f_idx, h] * inv_f[:, h * half:(h + 1) * half], 0.0)
            for h in range(2)], 0)

    lam = _lambda_value(lv_ref, lam_init)
    inv_d = 1.0 / ld
    g_col = g_ref[...]
    kept = []
    for h in range(2):
        a = acc_ref[d_idx, h] * inv_d[:, h * half:(h + 1) * half]
        d = a - lam * pltpu.roll(a, half - T, 1)
        d3 = d.reshape(WIDTH // 2 // PAIR, PAIR, half)
        ms = jnp.mean(d3 * d3, axis=1, keepdims=True)
        dn = (d3 * lax.rsqrt(ms + RMS_EPS)).reshape(WIDTH // 2, half) * g_col * (1.0 - lam_init)
        kept.append(jnp.where(hcol // T == 2 * (hrow // PAIR), dn, 0.0))
    finish(kept, WIDTH)


def _attn_sample(fq, fk, fv, dq, dk, dv, pfk, pfv_t, pdk, pdv_t, e, cq, lambda_vecs, g_col, lam_init):
    B, T, _ = fq.shape
    P = pfk.shape[1]
    tk = KV_TILE
    ncol = FOX_HEADS * T
    assert ncol == 2 * LANES and P % tk == 0 and e.shape[1] >= P + LANES
    new = pl.BlockSpec((1, T, WIDTH), lambda b: (b, 0, 0))
    keys = pl.BlockSpec((1, P, WIDTH), lambda b: (b, 0, 0))
    vals = pl.BlockSpec((1, P // tk, WIDTH, tk), lambda b: (b, 0, 0, 0))
    return pl.pallas_call(
        functools.partial(_attn_sample_kernel, lam_init=lam_init),
        grid=(B,),
        in_specs=[new] * 6 + [keys, vals, keys, vals,
                              pl.BlockSpec((1,) + e.shape[1:], lambda b: (b, 0, 0)),
                              pl.BlockSpec((1, 1, ncol), lambda b: (b, 0, 0)),
                              pl.BlockSpec(lambda_vecs.shape, lambda b: (0, 0)),
                              pl.BlockSpec(g_col.shape, lambda b: (0, 0))],
        out_specs=pl.BlockSpec((1, T, 2 * WIDTH), lambda b: (b, 0, 0)),
        out_shape=jax.ShapeDtypeStruct((B, T, 2 * WIDTH), BF16),
        scratch_shapes=[pltpu.VMEM((2, P // tk, tk, ncol), F32),
                        pltpu.VMEM((2, LANES, ncol), F32),
                        pltpu.VMEM((2, 2, WIDTH // 2, ncol // 2), F32)],
        compiler_params=_params("arbitrary"),
        name="attn_sample",
    )(fq, fk, fv, dq, dk, dv, pfk, pfv_t, pdk, pdv_t, e, cq, lambda_vecs, g_col)


def _post_norm(x, h, gate, g, b, alpha):
    y = alpha * x + gate * h
    mu = jnp.mean(y, axis=-1, keepdims=True)
    yc = y - mu
    var = jnp.mean(yc * yc, axis=-1, keepdims=True)
    return yc * lax.rsqrt(var + LN_EPS) * g + b


def _post_kernel(x_ref, o_ref, mod_ref, prev_ref, wo_ref, ln1g_ref, ln1b_ref,
                 wup_ref, cw_ref, cb_ref, wdn_ref, ln2g_ref, ln2b_ref,
                 y_ref, conv_ref, carry_ref, acc_ref, *, alpha):
    bb, tt, d = x_ref.shape
    rows = bb * tt
    keep = CONV_WIDTH - 1
    t = pl.program_id(1)

    @pl.when(t == 0)
    def _():
        carry_ref[...] = prev_ref[...]

    m = mod_ref[...]

    def rows3(v):
        return jnp.broadcast_to(v, (bb, tt, v.shape[-1])).reshape(rows, v.shape[-1])

    x = x_ref[...].reshape(rows, d)
    h = jnp.dot(o_ref[...].reshape(rows, d), wo_ref[...], preferred_element_type=F32)
    x1 = _post_norm(x, h, rows3(m[:, 2:3, :]), ln1g_ref[...], ln1b_ref[...], alpha)
    u2 = (x1 * (1.0 + rows3(m[:, 4:5, :])) + rows3(m[:, 3:4, :])).astype(BF16)

    tpos = lax.broadcasted_iota(jnp.int32, (bb, tt, 1), 1).reshape(rows, 1)
    acc_ref[...] = jnp.zeros_like(acc_ref)
    for c in range(D_FF // FF_TILE):
        cols = slice(c * FF_TILE, (c + 1) * FF_TILE)
        a = jnp.dot(u2, wup_ref[:, cols], preferred_element_type=F32)
        g = jnp.dot(u2, wup_ref[:, D_FF + c * FF_TILE:D_FF + (c + 1) * FF_TILE],
                    preferred_element_type=F32)
        prev = carry_ref[:, :, cols]
        p2 = rows3(prev[:, 0:1, :])
        p1 = rows3(prev[:, 1:2, :])
        am1 = jnp.where(tpos == 0, p1, pltpu.roll(a, 1, 0))
        am2 = jnp.where(tpos == 0, p2, jnp.where(tpos == 1, p1, pltpu.roll(a, 2, 0)))
        cw = cw_ref[:, cols]
        conv = cb_ref[:, cols] + am2 * cw[0:1] + am1 * cw[1:2] + a * cw[2:3]
        hid = conv * (1.0 / (1.0 + jnp.exp(-conv))) * g
        acc_ref[...] += jnp.dot(hid.astype(BF16), wdn_ref[cols, :], preferred_element_type=F32)
        last = a.reshape(bb, tt, FF_TILE)[:, tt - keep:, :]
        carry_ref[:, :, cols] = last
        conv_ref[:, :, cols] = last

    y = _post_norm(x1, acc_ref[...], rows3(m[:, 5:6, :]), ln2g_ref[...], ln2b_ref[...], alpha)
    y_ref[...] = y.reshape(bb, tt, d)


def _post(x, o, mod, conv_prev, wo, ln1g, ln1b, wup, cw, cb, wdn, ln2g, ln2b, bb, tt, alpha):
    B, T, d = x.shape
    keep = CONV_WIDTH - 1
    tok = pl.BlockSpec((bb, tt, d), lambda b, t: (b, t, 0))
    per_b = lambda r, w: pl.BlockSpec((bb, r, w), lambda b, t: (b, 0, 0))
    return pl.pallas_call(
        functools.partial(_post_kernel, alpha=alpha),
        grid=(B // bb, T // tt),
        in_specs=[tok, tok, per_b(6, d), per_b(keep, D_FF),
                  _resident(wo.shape), _resident(ln1g.shape), _resident(ln1b.shape),
                  _resident(wup.shape), _resident(cw.shape), _resident(cb.shape),
                  _resident(wdn.shape), _resident(ln2g.shape), _resident(ln2b.shape)],
        out_specs=[tok, per_b(keep, D_FF)],
        out_shape=[jax.ShapeDtypeStruct((B, T, d), F32),
                   jax.ShapeDtypeStruct((B, keep, D_FF), F32)],
        scratch_shapes=[pltpu.VMEM((bb, keep, D_FF), F32),
                        pltpu.VMEM((bb * tt, d), F32)],
        compiler_params=_params("arbitrary", "arbitrary"),
        name="post",
    )(x, o, mod, conv_prev, wo, ln1g, ln1b, wup, cw, cb, wdn, ln2g, ln2b)


def _rope_tables(pos0, T):
    half = HEAD_DIM // 2
    inv = ROPE_THETA ** (-jnp.arange(0, HEAD_DIM, 2, dtype=F32) / HEAD_DIM)
    ang = (pos0 + jnp.arange(T)).astype(F32)[:, None] * inv[None, :]
    cos, sin = jnp.cos(ang), jnp.sin(ang)
    reps = WIDTH // HEAD_DIM
    return (jnp.tile(jnp.concatenate([cos, cos], axis=1), (1, reps)),
            jnp.tile(jnp.concatenate([-sin, sin], axis=1), (1, reps)))


def _round_up(n, k):
    return -(-n // k) * k


def _layer(x, mod, past, w, lam_init, alpha):
    B, T, d = x.shape
    P = 0 if past is None else past[0].shape[1]
    if T % TOKEN_TILE == 0:
        bb, tt = 1, TOKEN_TILE
    else:
        tt = T
        bb = math.gcd(B, max(1, TOKEN_TILE // T))
    cos_t, sin_t = _rope_tables(P, T)
    if bb > 1:
        cos_t, sin_t = jnp.tile(cos_t, (bb, 1)), jnp.tile(sin_t, (bb, 1))
    (fk, fv, lf, dk, dv, fqb, fkb, fvb, dqb, dkb, dvb) = _inproj(
        x, mod, w["w_in"], w["b_f"], cos_t, sin_t, bb, tt, transposed=past is None)

    lf_t = jnp.swapaxes(lf, 1, 2)
    if past is None:
        cq_t, e = _cum(lf_t, with_bias=True)
        o = _attn_prompt(fqb, fkb, e, fvb, dqb, dkb, dvb, cq_t,
                         w["lambda_vecs"], w["subln_g"], lam_init)
        conv_prev = jnp.zeros((B, CONV_WIDTH - 1, D_FF), F32)
    else:
        pfk, pfv, plf, pdk, pdv, conv_prev = past
        S = _round_up(P + LANES, MXU_DIM)
        lf_all = jnp.concatenate(
            [jnp.swapaxes(plf, 1, 2), lf_t, jnp.zeros((B, FOX_HEADS, S - P - T), F32)], axis=2)
        cq_t, e = _cum(lf_all, with_bias=True)
        cq = cq_t[:, :, P:P + T].reshape(B, 1, FOX_HEADS * T)
        keys = lambda c: c.reshape(B, P, WIDTH).astype(BF16)
        vals_t = lambda c: jnp.swapaxes(c.reshape(B, P // KV_TILE, KV_TILE, WIDTH).astype(BF16), 2, 3)
        g_col = jnp.tile(w["subln_g"].reshape(PAIR), WIDTH // 2 // PAIR).reshape(WIDTH // 2, 1)
        o = _attn_sample(fqb, fkb, fvb, dqb, dkb, dvb, keys(pfk), vals_t(pfv), keys(pdk), vals_t(pdv),
                         e, cq, w["lambda_vecs"], g_col, lam_init)

    y, conv = _post(x, o, mod, conv_prev, w["w_o"], w["ln1_g"], w["ln1_b"], w["w_up"],
                    w["conv_w"], w["conv_b"], w["w_down"], w["ln2_g"], w["ln2_b"], bb, tt, alpha)
    state = (fk.reshape(B, T, FOX_HEADS, HEAD_DIM), fv.reshape(B, T, FOX_HEADS, HEAD_DIM), lf,
             dk.reshape(B, T, 2 * DIFF_HEADS, HEAD_DIM), dv.reshape(B, T, DIFF_HEADS, 2 * HEAD_DIM), conv)
    return y, state


def kernel(x_prompt, x_sample, c_prompt, c_sample, cache_fox_k, cache_fox_v, cache_fox_logf, cache_diff_k, cache_diff_v, state_ffn_conv, w_ada, b_ada, w_in, b_f, lambda_vecs, subln_g, w_o, ln1_g, ln1_b, w_up, conv_w, conv_b, w_down, ln2_g, ln2_b):
    depth = w_ada.shape[0]
    alpha = (2 * depth) ** 0.25
    nb = c_prompt.shape[0]
    yp, ys = x_prompt, x_sample
    c_all = jnp.concatenate([c_prompt, c_sample], axis=0)
    p_states, s_states = [], []
    for l in range(depth):
        lam_init = 0.8 - 0.6 * math.exp(-0.3 * l)
        w_in_l = w_in[l]
        gate_w = jnp.pad(w_in_l[:, OFF_FF:OFF_FF + FOX_HEADS], ((0, 0), (0, GATE_COLS - FOX_HEADS)))
        w = {
            "w_in": jnp.concatenate(
                [w_in_l[:, :OFF_FF], gate_w, w_in_l[:, OFF_FF + FOX_HEADS:]], axis=1).astype(BF16),
            "b_f": jnp.pad(b_f[l], (0, GATE_COLS - FOX_HEADS)).reshape(1, GATE_COLS),
            "lambda_vecs": lambda_vecs[l],
            "subln_g": subln_g[l].reshape(1, PAIR),
            "w_o": w_o[l].astype(BF16),
            "ln1_g": ln1_g[l].reshape(1, D_MODEL), "ln1_b": ln1_b[l].reshape(1, D_MODEL),
            "w_up": w_up[l].astype(BF16),
            "conv_w": conv_w[l], "conv_b": conv_b[l].reshape(1, D_FF),
            "w_down": w_down[l].astype(BF16),
            "ln2_g": ln2_g[l].reshape(1, D_MODEL), "ln2_b": ln2_b[l].reshape(1, D_MODEL),
        }
        mod = _ada(c_all, w_ada[l], b_ada[l]).reshape(c_all.shape[0], 6, D_MODEL)
        yp, st_p = _layer(yp, mod[:nb], None, w, lam_init, alpha)
        past = (cache_fox_k[l], cache_fox_v[l], cache_fox_logf[l], cache_diff_k[l], cache_diff_v[l],
                state_ffn_conv[l])
        ys, st_s = _layer(ys, mod[nb:], past, w, lam_init, alpha)
        p_states.append(st_p)
        s_states.append(st_s)
    p_out = [jnp.stack(a, axis=0) for a in zip(*p_states)]
    s_out = [jnp.stack(a, axis=0) for a in zip(*s_states)]
    return (yp, ys, *p_out, *s_out)
```

```python
import functools
import math

import jax
import jax.numpy as jnp
from jax import lax
from jax.experimental import pallas as pl
from jax.experimental.pallas import tpu as pltpu

F32 = jnp.float32
BF16 = jnp.bfloat16

D_MODEL = 1024
CHUNK = 64
FOX_HEADS = 8
DIFF_HEADS = 4
HEAD_DIM = 64
WIDTH = 512
PAIR = 2 * HEAD_DIM
N_PAIRS = WIDTH // PAIR
D_FF = 2816
CONV_WIDTH = 3
ROPE_THETA = 10000.0
LN_EPS = 1e-5
RMS_EPS = 1e-6
NEG_BIG = -1e30
Q_SCALE = HEAD_DIM ** -0.5
LOG2E = math.log2(math.e)

LANES = 128
MXU_DIM = 256
VMEM_LIMIT_BYTES = 56 * 1024 * 1024

GATE_COLS = LANES
OFF_FQ, OFF_FK, OFF_FV = 0, WIDTH, 2 * WIDTH
OFF_FF = 3 * WIDTH
OFF_DQ = OFF_FF + GATE_COLS
OFF_DK, OFF_DV = OFF_DQ + WIDTH, OFF_DQ + 2 * WIDTH
IN_COLS_PADDED = OFF_DV + WIDTH

KV_TILE = 256
TOKEN_TILE = 512
FF_TILE = 256


def _params(*sem):
    return pltpu.CompilerParams(dimension_semantics=sem, vmem_limit_bytes=VMEM_LIMIT_BYTES)


def _resident(shape):
    nd = len(shape)
    return pl.BlockSpec(shape, lambda *_: (0,) * nd, pipeline_mode=pl.Buffered(1))


def _ada_kernel(c_ref, w_ref, b_ref, o_ref):
    c = c_ref[...]
    s = c * (1.0 / (1.0 + jnp.exp(-c)))
    o_ref[...] = jnp.dot(s.astype(BF16), w_ref[...].astype(BF16),
                         preferred_element_type=F32) + b_ref[...]


def _ada(c, w_ada, b_ada):
    n, d = c.shape
    cols = w_ada.shape[1]
    tn = cols // 4
    return pl.pallas_call(
        _ada_kernel,
        grid=(cols // tn,),
        in_specs=[pl.BlockSpec((n, d), lambda j: (0, 0)),
                  pl.BlockSpec((d, tn), lambda j: (0, j)),
                  pl.BlockSpec((1, tn), lambda j: (0, j))],
        out_specs=pl.BlockSpec((n, tn), lambda j: (0, j)),
        out_shape=jax.ShapeDtypeStruct((n, cols), F32),
        compiler_params=_params("arbitrary"),
        name="ada",
    )(c, w_ada, b_ada.reshape(1, cols))


def _inproj_kernel(x_ref, mod_ref, w_ref, bf_ref, cos_ref, sin_ref,
                   fk_o, fv_o, lf_o, dk_o, dv_o,
                   fqb_o, fkb_o, fvb_o, dqb_o, dkb_o, dvb_o, *, transposed):
    bb, tt, d = x_ref.shape
    m = mod_ref[...]
    u = x_ref[...] * (1.0 + m[:, 1:2, :]) + m[:, 0:1, :]
    u = u.reshape(bb * tt, d).astype(BF16)
    q_scale = Q_SCALE * LOG2E

    def proj(off, width):
        return jnp.dot(u, w_ref[:, off:off + width], preferred_element_type=F32)

    def put(o_ref, v):
        o_ref[...] = v.reshape(o_ref.shape).astype(o_ref.dtype)

    def put_qv(o_ref, v):
        if transposed:
            for c in range(tt // KV_TILE):
                o_ref[0, c] = v[c * KV_TILE:(c + 1) * KV_TILE, :].T.astype(o_ref.dtype)
        else:
            put(o_ref, v)

    fq = proj(OFF_FQ, WIDTH)
    put_qv(fqb_o, fq * q_scale)
    fk = proj(OFF_FK, WIDTH)
    put(fk_o, fk)
    put(fkb_o, fk)
    fv = proj(OFF_FV, WIDTH)
    put(fv_o, fv)
    put_qv(fvb_o, fv)

    zf = proj(OFF_FF, GATE_COLS) + bf_ref[...]
    lf = jnp.minimum(zf, 0.0) - jnp.log1p(jnp.exp(-jnp.abs(zf)))
    put(lf_o, lf[:, :FOX_HEADS])

    cos = cos_ref[...]
    sin = sin_ref[...]
    lane = lax.broadcasted_iota(jnp.int32, (1, WIDTH), 1)
    first_half = (lane % HEAD_DIM) < (HEAD_DIM // 2)

    def rope(v):
        partner = jnp.where(first_half, pltpu.roll(v, WIDTH - HEAD_DIM // 2, 1),
                            pltpu.roll(v, HEAD_DIM // 2, 1))
        return v * cos + partner * sin

    dq = rope(proj(OFF_DQ, WIDTH))
    put_qv(dqb_o, dq * q_scale)
    dk = rope(proj(OFF_DK, WIDTH))
    put(dk_o, dk)
    put(dkb_o, dk)
    dv = proj(OFF_DV, WIDTH)
    put(dv_o, dv)
    put_qv(dvb_o, dv)


def _inproj(x, mod, w_pad, bf_pad, cos_t, sin_t, bb, tt, transposed):
    B, T, d = x.shape
    rows = bb * tt
    grid = (B // bb, T // tt)
    assert not transposed or (bb == 1 and tt % KV_TILE == 0)
    tok = lambda w: pl.BlockSpec((bb, tt, w), lambda b, t: (b, t, 0))
    tab = pl.BlockSpec((rows, WIDTH), (lambda b, t: (t, 0)) if bb == 1 else (lambda b, t: (0, 0)))
    sds = lambda w, dt: jax.ShapeDtypeStruct((B, T, w), dt)
    if transposed:
        qv = pl.BlockSpec((1, tt // KV_TILE, WIDTH, KV_TILE), lambda b, t: (b, t, 0, 0))
        qv_sds = jax.ShapeDtypeStruct((B, T // KV_TILE, WIDTH, KV_TILE), BF16)
    else:
        qv, qv_sds = tok(WIDTH), sds(WIDTH, BF16)
    return pl.pallas_call(
        functools.partial(_inproj_kernel, transposed=transposed),
        grid=grid,
        in_specs=[tok(d),
                  pl.BlockSpec((bb, 6, d), lambda b, t: (b, 0, 0)),
                  _resident(w_pad.shape),
                  _resident(bf_pad.shape),
                  tab, tab],
        out_specs=[tok(WIDTH), tok(WIDTH), tok(FOX_HEADS), tok(WIDTH), tok(WIDTH),
                   qv, tok(WIDTH), qv, qv, tok(WIDTH), qv],
        out_shape=[sds(WIDTH, F32), sds(WIDTH, F32), sds(FOX_HEADS, F32), sds(WIDTH, F32), sds(WIDTH, F32),
                   qv_sds, sds(WIDTH, BF16), qv_sds, qv_sds, sds(WIDTH, BF16), qv_sds],
        compiler_params=_params("arbitrary", "arbitrary"),
        name="inproj",
    )(x, mod, w_pad, bf_pad, cos_t, sin_t)


def _split3(x):
    hi = x.astype(BF16).astype(F32)
    r1 = x - hi
    mid = r1.astype(BF16).astype(F32)
    lo = (r1 - mid).astype(BF16).astype(F32)
    return hi, mid, lo


def _cum_kernel(x_ref, o_ref, e_ref, *, keys_on_lanes):
    S = x_ref.shape[2]
    r = lax.broadcasted_iota(jnp.int32, (MXU_DIM, MXU_DIM), 0)
    c = lax.broadcasted_iota(jnp.int32, (MXU_DIM, MXU_DIM), 1)
    tri = jnp.where(r <= c, 1.0, 0.0).astype(BF16)
    carry = jnp.zeros((FOX_HEADS, 1), F32)
    for k in range(S // MXU_DIM):
        sl = slice(k * MXU_DIM, (k + 1) * MXU_DIM)
        hi, mid, lo = _split3(x_ref[0, :, sl])
        part = (jnp.dot(hi.astype(BF16), tri, preferred_element_type=F32)
                + jnp.dot(mid.astype(BF16), tri, preferred_element_type=F32)
                + jnp.dot(lo.astype(BF16), tri, preferred_element_type=F32))
        part = part + carry
        carry = part[:, MXU_DIM - 1:MXU_DIM]
        scaled = part * LOG2E
        o_ref[0, :, sl] = scaled
        pieces = _split3(-scaled)
        pad = jnp.zeros((LANES - 3 * FOX_HEADS, MXU_DIM), F32)
        bias = jnp.concatenate(pieces + (pad,), axis=0)
        if keys_on_lanes:
            e_ref[0, :, sl] = bias.astype(BF16)
        else:
            e_ref[0, sl, :] = bias.T.astype(BF16)


def _cum(lf_t, bias_layout):
    B, H, S = lf_t.shape
    spec = pl.BlockSpec((1, H, S), lambda b: (b, 0, 0))
    e_shape = (B, LANES, S) if bias_layout == "lanes" else (B, S, LANES)
    return pl.pallas_call(
        functools.partial(_cum_kernel, keys_on_lanes=bias_layout == "lanes"),
        grid=(B,), in_specs=[spec],
        out_specs=[spec, pl.BlockSpec((1,) + e_shape[1:], lambda b: (b, 0, 0))],
        out_shape=[jax.ShapeDtypeStruct((B, H, S), F32), jax.ShapeDtypeStruct(e_shape, BF16)],
        compiler_params=_params("arbitrary"),
        name="cum",
    )(lf_t)


def _lambda_value(lv_ref, lam_init):
    lv = lv_ref[...]
    a = jnp.sum(lv[0:1] * lv[1:2], axis=1, keepdims=True)
    b = jnp.sum(lv[2:3] * lv[3:4], axis=1, keepdims=True)
    return jnp.exp(a) - jnp.exp(b) + lam_init


def _attn_prompt_kernel(fq_ref, fk_ref, e_ref, fv_ref, dq_ref, dk_ref, dv_ref, cq_ref,
                        lv_ref, g_ref, o_ref, s_ref, acc_ref, *, lam_init):
    tq = tk = KV_TILE
    i = pl.program_id(1)
    srow = lax.broadcasted_iota(jnp.int32, (PAIR, tq), 0)
    first = srow < HEAD_DIM
    krow = lax.broadcasted_iota(jnp.int32, (tk, tq), 0)
    qcol = lax.broadcasted_iota(jnp.int32, (tk, tq), 1)
    causal = krow <= qcol
    chunk_causal = (krow // CHUNK) <= (qcol // CHUNK)

    def attend(units, mask, rows):
        chains = []
        for wa, wb, cqa, cqb, keys, vta, vtb in units:
            chains += [(wa, cqa, keys, vta), (wb, cqb, keys, vtb)]
        n = len(chains)

        def score(c, j):
            w, _, keys, _ = chains[c]
            s = jnp.dot(keys(j), w, preferred_element_type=F32)
            s_ref[c] = s
            return jnp.max(s, axis=0, keepdims=True)

        def consume(c, j, mb, msk, m, l):
            _, cq, _, vt = chains[c]
            s = s_ref[c]
            if msk is not None:
                s = jnp.where(msk, s, NEG_BIG)
                mb = jnp.max(s, axis=0, keepdims=True)
            m_new = jnp.maximum(m, mb if cq is None else mb + cq)
            p = jnp.exp2(s - (m_new if cq is None else m_new - cq))
            alpha = jnp.exp2(m - m_new)
            acc_ref[c, :rows] = alpha * acc_ref[c, :rows] + jnp.dot(
                vt(j), p.astype(BF16), preferred_element_type=F32)
            return m_new, alpha * l + jnp.sum(p, axis=0, keepdims=True)

        def body(j, carry):
            ms, ls, mbs = carry
            ms, ls, mbs = list(ms), list(ls), list(mbs)
            for c in range(n):
                ms[c], ls[c] = consume(c, j, mbs[c], None, ms[c], ls[c])
                mbs[c] = score(c, j + 1)
            return tuple(ms), tuple(ls), tuple(mbs)

        acc_ref[...] = jnp.zeros_like(acc_ref)
        ms, ls, _ = lax.fori_loop(
            0, i, body,
            (tuple(jnp.full((1, tq), NEG_BIG, F32) for _ in range(n)),
             tuple(jnp.zeros((1, tq), F32) for _ in range(n)),
             tuple(score(c, 0) for c in range(n))))
        ls = [consume(c, i, None, mask, ms[c], ls[c])[1] for c in range(n)]
        outs = [acc_ref[c, :rows] / ls[c] for c in range(n)]
        return list(zip(outs[0::2], outs[1::2]))

    def rows_of(j):
        return pl.ds(pl.multiple_of(j * tk, tk), tk)

    def select(head):
        return jnp.where((srow % FOX_HEADS == head) & (srow < 3 * FOX_HEADS), 1.0, 0.0).astype(BF16)

    def pair_lanes(p):
        return slice(p * PAIR, (p + 1) * PAIR)

    zero = jnp.zeros((PAIR, tq), BF16)

    def fox_unit(pair):
        lanes = pair_lanes(pair)
        qt = fq_ref[0, 0, lanes, :]
        return (jnp.concatenate([jnp.where(first, qt, zero), select(2 * pair)], axis=0),
                jnp.concatenate([jnp.where(first, zero, qt), select(2 * pair + 1)], axis=0),
                cq_ref[0, 2 * pair:2 * pair + 1, :], cq_ref[0, 2 * pair + 1:2 * pair + 2, :],
                lambda j: jnp.concatenate([fk_ref[0, rows_of(j), lanes], e_ref[0, rows_of(j), :]], axis=1),
                lambda j: fv_ref[0, j, pair * PAIR:pair * PAIR + HEAD_DIM, :],
                lambda j: fv_ref[0, j, pair * PAIR + HEAD_DIM:(pair + 1) * PAIR, :])

    def diff_unit(hd):
        lanes = pair_lanes(hd)
        qt = dq_ref[0, 0, lanes, :]
        vt = lambda j: dv_ref[0, j, lanes, :]
        return (jnp.where(first, qt, zero), jnp.where(first, zero, qt), None, None,
                lambda j: dk_ref[0, rows_of(j), lanes], vt, vt)

    fox = attend([fox_unit(p) for p in range(N_PAIRS)], causal, HEAD_DIM)
    for pair, (oa, ob) in enumerate(fox):
        o_ref[0, :, pair_lanes(pair)] = jnp.concatenate([oa, ob], axis=0).T.astype(o_ref.dtype)

    lam = _lambda_value(lv_ref, lam_init)
    diff = attend([diff_unit(hd) for hd in range(DIFF_HEADS)], chunk_causal, PAIR)
    for hd, (o1, o2) in enumerate(diff):
        d = o1 - lam * o2
        ms = jnp.mean(d * d, axis=0, keepdims=True)
        dn = (d * lax.rsqrt(ms + RMS_EPS)).T * g_ref[...] * (1.0 - lam_init)
        o_ref[0, :, WIDTH + hd * PAIR:WIDTH + (hd + 1) * PAIR] = dn.astype(o_ref.dtype)


def _attn_prompt(fq_t, fk, e, fv_t, dq_t, dk, dv_t, cq_t, lambda_vecs, subln_g, lam_init):
    B, T, _ = fk.shape
    tq = KV_TILE
    nt = T // tq
    qspec = pl.BlockSpec((1, 1, WIDTH, tq), lambda b, i: (b, i, 0, 0))
    vspec = pl.BlockSpec((1, nt, WIDTH, tq), lambda b, i: (b, 0, 0, 0))
    kspec = pl.BlockSpec((1, T, WIDTH), lambda b, i: (b, 0, 0))
    return pl.pallas_call(
        functools.partial(_attn_prompt_kernel, lam_init=lam_init),
        grid=(B, nt),
        in_specs=[qspec, kspec, pl.BlockSpec((1, T, LANES), lambda b, i: (b, 0, 0)), vspec,
                  qspec, kspec, vspec,
                  pl.BlockSpec((1, FOX_HEADS, tq), lambda b, i: (b, 0, i)),
                  pl.BlockSpec(lambda_vecs.shape, lambda b, i: (0, 0)),
                  pl.BlockSpec(subln_g.shape, lambda b, i: (0, 0))],
        out_specs=pl.BlockSpec((1, tq, 2 * WIDTH), lambda b, i: (b, i, 0)),
        out_shape=jax.ShapeDtypeStruct((B, T, 2 * WIDTH), BF16),
        scratch_shapes=[pltpu.VMEM((2 * N_PAIRS, tq, tq), F32),
                        pltpu.VMEM((2 * N_PAIRS, PAIR, tq), F32)],
        compiler_params=_params("arbitrary", "arbitrary"),
        name="attn_prompt",
    )(fq_t, fk, e, fv_t, dq_t, dk, dv_t, cq_t, lambda_vecs, subln_g)


def _attn_sample_kernel_keys_on_rows(fq_ref, fk_ref, fv_ref, dq_ref, dk_ref, dv_ref,
                                     pfk_ref, pfv_ref, pdk_ref, pdv_ref, e_ref, cq_ref,
                                     lv_ref, g_ref, o_ref, s_ref, snew_ref, acc_ref, *, lam_init):
    T = fq_ref.shape[1]
    P = pfk_ref.shape[1]
    tk = KV_TILE
    n_past = P // tk
    ncol = FOX_HEADS * T
    half = ncol // 2
    new_rows = LANES
    f_idx, d_idx = 0, 1

    row = lambda shape: lax.broadcasted_iota(jnp.int32, shape, 0)
    col = lambda shape: lax.broadcasted_iota(jnp.int32, shape, 1)

    spread = jnp.where(row((LANES, ncol)) == col((LANES, ncol)) % T, 1.0, 0.0).astype(BF16)
    own_head = row((WIDTH, ncol)) // HEAD_DIM == col((WIDTH, ncol)) // T
    select = jnp.where((row((LANES, ncol)) % FOX_HEADS == col((LANES, ncol)) // T)
                       & (row((LANES, ncol)) < 3 * FOX_HEADS), 1.0, 0.0).astype(BF16)

    def pad_rows(x):
        return jnp.concatenate([x, jnp.zeros((new_rows - T, x.shape[1]), x.dtype)], axis=0)

    def transposed(x):
        return pad_rows(x).astype(F32).T.astype(BF16)

    def block_diag(q_ref):
        full = jnp.dot(transposed(q_ref[0]), spread, preferred_element_type=F32)
        return jnp.where(own_head, full, 0.0).astype(BF16)

    w_f = jnp.concatenate([block_diag(fq_ref), select], axis=0)
    w_d = block_diag(dq_ref)
    cq = cq_ref[0]

    def past_scores(j, carry):
        mf, md = carry
        rows = pl.ds(pl.multiple_of(j * tk, tk), tk)
        sf = jnp.dot(jnp.concatenate([pfk_ref[0, rows, :], e_ref[0, rows, :]], axis=1), w_f,
                     preferred_element_type=F32)
        sd = jnp.dot(pdk_ref[0, rows, :], w_d, preferred_element_type=F32)
        s_ref[f_idx, j] = sf
        s_ref[d_idx, j] = sd
        return (jnp.maximum(mf, jnp.max(sf, axis=0, keepdims=True)),
                jnp.maximum(md, jnp.max(sd, axis=0, keepdims=True)))

    neg = jnp.full((1, ncol), NEG_BIG, F32)
    mf, md = lax.fori_loop(0, n_past, past_scores, (neg, neg), unroll=2)

    krow = row((new_rows, ncol))
    qpos = col((new_rows, ncol)) % T
    valid = krow < T
    causal = valid & (krow <= qpos)
    chunk_causal = valid & ((krow + P) // CHUNK <= (qpos + P) // CHUNK)
    sf = jnp.dot(jnp.concatenate([pad_rows(fk_ref[0]), e_ref[0, P:P + new_rows, :]], axis=1), w_f,
                 preferred_element_type=F32)
    sf = jnp.where(causal, sf, NEG_BIG)
    sd = jnp.where(chunk_causal, jnp.dot(pad_rows(dk_ref[0]), w_d, preferred_element_type=F32), NEG_BIG)
    snew_ref[f_idx] = sf
    snew_ref[d_idx] = sd
    mf = jnp.maximum(mf, jnp.max(sf, axis=0, keepdims=True)) + cq
    shift_f = mf - cq
    shift_d = jnp.maximum(md, jnp.max(sd, axis=0, keepdims=True))

    def apply(idx, s, shift, vt, l):
        p = jnp.exp2(s - shift)
        pb = p.astype(BF16)
        acc_ref[idx, 0] += jnp.dot(vt[:WIDTH // 2], pb[:, :half], preferred_element_type=F32)
        acc_ref[idx, 1] += jnp.dot(vt[WIDTH // 2:], pb[:, half:], preferred_element_type=F32)
        return l + jnp.sum(p, axis=0, keepdims=True)

    def past_apply(j, carry):
        lf, ld = carry
        lf = apply(f_idx, s_ref[f_idx, j], shift_f, pfv_ref[0, j], lf)
        ld = apply(d_idx, s_ref[d_idx, j], shift_d, pdv_ref[0, j], ld)
        return lf, ld

    acc_ref[...] = jnp.zeros_like(acc_ref)
    zero = jnp.zeros((1, ncol), F32)
    lf, ld = lax.fori_loop(0, n_past, past_apply, (zero, zero), unroll=2)
    lf = apply(f_idx, snew_ref[f_idx], shift_f, transposed(fv_ref[0]), lf)
    ld = apply(d_idx, snew_ref[d_idx], shift_d, transposed(dv_ref[0]), ld)

    fold = jnp.where(row((half, LANES)) % T == col((half, LANES)), 1.0, 0.0).astype(BF16)
    hrow = row((WIDTH // 2, half))
    hcol = col((WIDTH // 2, half))

    def finish(kept_halves, lane0):
        out_t = jnp.concatenate(
            [jnp.dot(k.astype(BF16), fold, preferred_element_type=F32) for k in kept_halves], axis=0)
        o_ref[0, :, lane0:lane0 + WIDTH] = out_t.T[:T].astype(o_ref.dtype)

    inv_f = 1.0 / lf
    finish([jnp.where(hrow // HEAD_DIM == hcol // T, acc_ref[f_idx, h] * inv_f[:, h * half:(h + 1) * half], 0.0)
            for h in range(2)], 0)

    lam = _lambda_value(lv_ref, lam_init)
    inv_d = 1.0 / ld
    g_col = g_ref[...]
    kept = []
    for h in range(2):
        a = acc_ref[d_idx, h] * inv_d[:, h * half:(h + 1) * half]
        d = a - lam * pltpu.roll(a, half - T, 1)
        d3 = d.reshape(WIDTH // 2 // PAIR, PAIR, half)
        ms = jnp.mean(d3 * d3, axis=1, keepdims=True)
        dn = (d3 * lax.rsqrt(ms + RMS_EPS)).reshape(WIDTH // 2, half) * g_col * (1.0 - lam_init)
        kept.append(jnp.where(hcol // T == 2 * (hrow // PAIR), dn, 0.0))
    finish(kept, WIDTH)


def _attn_sample_keys_on_rows(fq, fk, fv, dq, dk, dv, pfk, pfv_t, pdk, pdv_t, e, cq, lambda_vecs, g_col, lam_init):
    B, T, _ = fq.shape
    P = pfk.shape[1]
    tk = KV_TILE
    ncol = FOX_HEADS * T
    assert ncol == 2 * LANES and P % tk == 0 and e.shape[1] >= P + LANES
    new = pl.BlockSpec((1, T, WIDTH), lambda b: (b, 0, 0))
    keys = pl.BlockSpec((1, P, WIDTH), lambda b: (b, 0, 0))
    vals = pl.BlockSpec((1, P // tk, WIDTH, tk), lambda b: (b, 0, 0, 0))
    return pl.pallas_call(
        functools.partial(_attn_sample_kernel_keys_on_rows, lam_init=lam_init),
        grid=(B,),
        in_specs=[new] * 6 + [keys, vals, keys, vals,
                              pl.BlockSpec((1,) + e.shape[1:], lambda b: (b, 0, 0)),
                              pl.BlockSpec((1, 1, ncol), lambda b: (b, 0, 0)),
                              pl.BlockSpec(lambda_vecs.shape, lambda b: (0, 0)),
                              pl.BlockSpec(g_col.shape, lambda b: (0, 0))],
        out_specs=pl.BlockSpec((1, T, 2 * WIDTH), lambda b: (b, 0, 0)),
        out_shape=jax.ShapeDtypeStruct((B, T, 2 * WIDTH), BF16),
        scratch_shapes=[pltpu.VMEM((2, P // tk, tk, ncol), F32),
                        pltpu.VMEM((2, LANES, ncol), F32),
                        pltpu.VMEM((2, 2, WIDTH // 2, ncol // 2), F32)],
        compiler_params=_params("arbitrary"),
        name="attn_sample",
    )(fq, fk, fv, dq, dk, dv, pfk, pfv_t, pdk, pdv_t, e, cq, lambda_vecs, g_col)


def _attn_sample_kernel(fq_ref, fk_ref, fv_ref, dq_ref, dk_ref, dv_ref,
                        pfk_ref, pfv_ref, pdk_ref, pdv_ref, et_ref, cq_ref,
                        lv_ref, g_ref, o_ref, s_ref, *, lam_init):
    T = fq_ref.shape[1]
    P = pfk_ref.shape[2]
    tk = KV_TILE
    n_past = P // tk
    nrow = FOX_HEADS * T
    prow = 2 * T
    f_idx, d_idx = 0, 1
    nt_dims = (((1,), (1,)), ((), ()))

    row = lambda shape: lax.broadcasted_iota(jnp.int32, shape, 0)
    col = lambda shape: lax.broadcasted_iota(jnp.int32, shape, 1)

    own_head = row((nrow, WIDTH)) // T == col((nrow, WIDTH)) // HEAD_DIM
    select_t = jnp.where((col((nrow, LANES)) % FOX_HEADS == row((nrow, LANES)) // T)
                         & (col((nrow, LANES)) < 3 * FOX_HEADS), 1.0, 0.0).astype(BF16)

    def block_diag(q_ref):
        q = q_ref[0]
        return jnp.where(own_head, jnp.concatenate([q] * FOX_HEADS, axis=0), jnp.zeros((nrow, WIDTH), BF16))

    def pad_rows(x):
        return jnp.concatenate([x, jnp.zeros((LANES - T, x.shape[1]), x.dtype)], axis=0)

    def lane_halves(x, op):
        return op(x[:, :LANES], x[:, LANES:])

    qf = block_diag(fq_ref)
    qd = block_diag(dq_ref)
    lhs_f = jnp.concatenate([qf, select_t], axis=1)

    mf = md = jnp.full((nrow, LANES), NEG_BIG, F32)
    for j in range(n_past):
        cols = slice(j * tk, (j + 1) * tk)
        kf = jnp.concatenate([pfk_ref[0, :, cols].astype(BF16), et_ref[0, :, cols]], axis=0)
        sf = jnp.dot(lhs_f, kf, preferred_element_type=F32)
        sd = jnp.dot(qd, pdk_ref[0, :, cols].astype(BF16), preferred_element_type=F32)
        s_ref[f_idx, j] = sf
        s_ref[d_idx, j] = sd
        mf = jnp.maximum(mf, lane_halves(sf, jnp.maximum))
        md = jnp.maximum(md, lane_halves(sd, jnp.maximum))

    key = col((nrow, LANES))
    qpos = row((nrow, LANES)) % T
    valid = key < T
    causal = valid & (key <= qpos)
    chunk_causal = valid & ((key + P) // CHUNK <= (qpos + P) // CHUNK)
    sf_new = (lax.dot_general(qf, pad_rows(fk_ref[0]), nt_dims, preferred_element_type=F32)
              + jnp.dot(select_t, et_ref[0, :, P:P + LANES], preferred_element_type=F32))
    sf_new = jnp.where(causal, sf_new, NEG_BIG)
    sd_new = jnp.where(chunk_causal,
                       lax.dot_general(qd, pad_rows(dk_ref[0]), nt_dims, preferred_element_type=F32), NEG_BIG)
    cq = cq_ref[0]
    m_f = jnp.max(jnp.maximum(mf, sf_new), axis=1, keepdims=True) + cq
    shift_f = jnp.broadcast_to(m_f - cq, (nrow, LANES))
    shift_d = jnp.broadcast_to(jnp.max(jnp.maximum(md, sd_new), axis=1, keepdims=True), (nrow, LANES))

    def probs(s, shift):
        return jnp.exp2(s - jnp.concatenate([shift] * (s.shape[1] // LANES), axis=1))

    lf = ld = jnp.zeros((nrow, LANES), F32)
    acc_f = [jnp.zeros((prow, PAIR), F32) for _ in range(N_PAIRS)]
    acc_d = [jnp.zeros((prow, PAIR), F32) for _ in range(DIFF_HEADS)]
    for j in range(n_past):
        cols = slice(j * tk, (j + 1) * tk)
        pf = probs(s_ref[f_idx, j], shift_f)
        pd = probs(s_ref[d_idx, j], shift_d)
        lf = lf + lane_halves(pf, jnp.add)
        ld = ld + lane_halves(pd, jnp.add)
        pf, pd = pf.astype(BF16), pd.astype(BF16)
        for u in range(N_PAIRS):
            rows = slice(u * prow, (u + 1) * prow)
            acc_f[u] = acc_f[u] + lax.dot_general(
                pf[rows], pfv_ref[0, u * PAIR:(u + 1) * PAIR, cols].astype(BF16), nt_dims,
                preferred_element_type=F32)
            v = pdv_ref[0, pl.ds(j * tk * DIFF_HEADS + u, tk, stride=DIFF_HEADS), :].astype(BF16)
            acc_d[u] = acc_d[u] + jnp.dot(pd[rows], v, preferred_element_type=F32)
    pf = jnp.exp2(sf_new - shift_f)
    pd = jnp.exp2(sd_new - shift_d)
    lf = jnp.sum(lf + pf, axis=1, keepdims=True)
    ld = jnp.sum(ld + pd, axis=1, keepdims=True)
    pf, pd = pf.astype(BF16), pd.astype(BF16)
    fv_new, dv_new = pad_rows(fv_ref[0]), pad_rows(dv_ref[0])
    low = col((T, PAIR)) < HEAD_DIM
    lam = _lambda_value(lv_ref, lam_init)
    for u in range(N_PAIRS):
        rows = slice(u * prow, (u + 1) * prow)
        lanes = slice(u * PAIR, (u + 1) * PAIR)
        a = (acc_f[u] + jnp.dot(pf[rows], fv_new[:, lanes], preferred_element_type=F32)) / lf[rows]
        o_ref[0, :, lanes] = jnp.where(low, a[:T], a[T:]).astype(o_ref.dtype)
        a = (acc_d[u] + jnp.dot(pd[rows], dv_new[:, lanes], preferred_element_type=F32)) / ld[rows]
        d = a[:T] - lam * a[T:]
        ms = jnp.mean(d * d, axis=-1, keepdims=True)
        dn = d * lax.rsqrt(ms + RMS_EPS) * g_ref[...] * (1.0 - lam_init)
        o_ref[0, :, WIDTH + u * PAIR:WIDTH + (u + 1) * PAIR] = dn.astype(o_ref.dtype)


def _attn_sample(fq, fk, fv, dq, dk, dv, pfk_t, pfv_t, pdk_t, pdv_rows, e_t, cq, lambda_vecs, subln_g,
                 lam_init):
    B, T, _ = fq.shape
    P = pfk_t.shape[2]
    tk = KV_TILE
    nrow = FOX_HEADS * T
    assert N_PAIRS == DIFF_HEADS and T <= LANES and P % tk == 0 and e_t.shape[2] >= P + LANES
    new = pl.BlockSpec((1, T, WIDTH), lambda b: (b, 0, 0))
    cached = pl.BlockSpec((1, WIDTH, P), lambda b: (b, 0, 0))
    return pl.pallas_call(
        functools.partial(_attn_sample_kernel, lam_init=lam_init),
        grid=(B,),
        in_specs=[new] * 6 + [cached, cached, cached,
                              pl.BlockSpec((1, P * DIFF_HEADS, PAIR), lambda b: (b, 0, 0)),
                              pl.BlockSpec((1,) + e_t.shape[1:], lambda b: (b, 0, 0)),
                              pl.BlockSpec((1, nrow, 1), lambda b: (b, 0, 0)),
                              pl.BlockSpec(lambda_vecs.shape, lambda b: (0, 0)),
                              pl.BlockSpec(subln_g.shape, lambda b: (0, 0))],
        out_specs=pl.BlockSpec((1, T, 2 * WIDTH), lambda b: (b, 0, 0)),
        out_shape=jax.ShapeDtypeStruct((B, T, 2 * WIDTH), BF16),
        scratch_shapes=[pltpu.VMEM((2, P // tk, nrow, tk), F32)],
        compiler_params=_params("arbitrary"),
        name="attn_sample",
    )(fq, fk, fv, dq, dk, dv, pfk_t, pfv_t, pdk_t, pdv_rows, e_t, cq, lambda_vecs, subln_g)


def _post_norm(x, h, gate, g, b, alpha):
    y = alpha * x + gate * h
    mu = jnp.mean(y, axis=-1, keepdims=True)
    yc = y - mu
    var = jnp.mean(yc * yc, axis=-1, keepdims=True)
    return yc * lax.rsqrt(var + LN_EPS) * g + b


def _post_kernel(x_ref, o_ref, mod_ref, prev_ref, wo_ref, ln1g_ref, ln1b_ref,
                 wup_ref, cw_ref, cb_ref, wdn_ref, ln2g_ref, ln2b_ref,
                 y_ref, conv_ref, carry_ref, acc_ref, *, alpha):
    bb, tt, d = x_ref.shape
    rows = bb * tt
    keep = CONV_WIDTH - 1
    t = pl.program_id(1)

    @pl.when(t == 0)
    def _():
        carry_ref[...] = prev_ref[...]

    m = mod_ref[...]

    def rows3(v):
        return jnp.broadcast_to(v, (bb, tt, v.shape[-1])).reshape(rows, v.shape[-1])

    x = x_ref[...].reshape(rows, d)
    h = jnp.dot(o_ref[...].reshape(rows, d), wo_ref[...], preferred_element_type=F32)
    x1 = _post_norm(x, h, rows3(m[:, 2:3, :]), ln1g_ref[...], ln1b_ref[...], alpha)
    u2 = (x1 * (1.0 + rows3(m[:, 4:5, :])) + rows3(m[:, 3:4, :])).astype(BF16)

    tpos = lax.broadcasted_iota(jnp.int32, (bb, tt, 1), 1).reshape(rows, 1)
    acc_ref[...] = jnp.zeros_like(acc_ref)
    for c in range(D_FF // FF_TILE):
        cols = slice(c * FF_TILE, (c + 1) * FF_TILE)
        a = jnp.dot(u2, wup_ref[:, cols], preferred_element_type=F32)
        g = jnp.dot(u2, wup_ref[:, D_FF + c * FF_TILE:D_FF + (c + 1) * FF_TILE],
                    preferred_element_type=F32)
        prev = carry_ref[:, :, cols]
        p2 = rows3(prev[:, 0:1, :])
        p1 = rows3(prev[:, 1:2, :])
        am1 = jnp.where(tpos == 0, p1, pltpu.roll(a, 1, 0))
        am2 = jnp.where(tpos == 0, p2, jnp.where(tpos == 1, p1, pltpu.roll(a, 2, 0)))
        cw = cw_ref[:, cols]
        conv = cb_ref[:, cols] + am2 * cw[0:1] + am1 * cw[1:2] + a * cw[2:3]
        hid = conv * (1.0 / (1.0 + jnp.exp(-conv))) * g
        acc_ref[...] += jnp.dot(hid.astype(BF16), wdn_ref[cols, :], preferred_element_type=F32)
        last = a.reshape(bb, tt, FF_TILE)[:, tt - keep:, :]
        carry_ref[:, :, cols] = last
        conv_ref[:, :, cols] = last

    y = _post_norm(x1, acc_ref[...], rows3(m[:, 5:6, :]), ln2g_ref[...], ln2b_ref[...], alpha)
    y_ref[...] = y.reshape(bb, tt, d)


def _post(x, o, mod, conv_prev, wo, ln1g, ln1b, wup, cw, cb, wdn, ln2g, ln2b, bb, tt, alpha):
    B, T, d = x.shape
    keep = CONV_WIDTH - 1
    tok = pl.BlockSpec((bb, tt, d), lambda b, t: (b, t, 0))
    per_b = lambda r, w: pl.BlockSpec((bb, r, w), lambda b, t: (b, 0, 0))
    return pl.pallas_call(
        functools.partial(_post_kernel, alpha=alpha),
        grid=(B // bb, T // tt),
        in_specs=[tok, tok, per_b(6, d), per_b(keep, D_FF),
                  _resident(wo.shape), _resident(ln1g.shape), _resident(ln1b.shape),
                  _resident(wup.shape), _resident(cw.shape), _resident(cb.shape),
                  _resident(wdn.shape), _resident(ln2g.shape), _resident(ln2b.shape)],
        out_specs=[tok, per_b(keep, D_FF)],
        out_shape=[jax.ShapeDtypeStruct((B, T, d), F32),
                   jax.ShapeDtypeStruct((B, keep, D_FF), F32)],
        scratch_shapes=[pltpu.VMEM((bb, keep, D_FF), F32),
                        pltpu.VMEM((bb * tt, d), F32)],
        compiler_params=_params("arbitrary", "arbitrary"),
        name="post",
    )(x, o, mod, conv_prev, wo, ln1g, ln1b, wup, cw, cb, wdn, ln2g, ln2b)


def _rope_tables(pos0, T):
    half = HEAD_DIM // 2
    inv = ROPE_THETA ** (-jnp.arange(0, HEAD_DIM, 2, dtype=F32) / HEAD_DIM)
    ang = (pos0 + jnp.arange(T)).astype(F32)[:, None] * inv[None, :]
    cos, sin = jnp.cos(ang), jnp.sin(ang)
    reps = WIDTH // HEAD_DIM
    return (jnp.tile(jnp.concatenate([cos, cos], axis=1), (1, reps)),
            jnp.tile(jnp.concatenate([-sin, sin], axis=1), (1, reps)))


def _round_up(n, k):
    return -(-n // k) * k


def _layer(x, mod, past, w, lam_init, alpha):
    B, T, d = x.shape
    P = 0 if past is None else past[0].shape[1]
    if T % TOKEN_TILE == 0:
        bb, tt = 1, TOKEN_TILE
    else:
        tt = T
        bb = math.gcd(B, max(1, TOKEN_TILE // T))
    cos_t, sin_t = _rope_tables(P, T)
    if bb > 1:
        cos_t, sin_t = jnp.tile(cos_t, (bb, 1)), jnp.tile(sin_t, (bb, 1))
    (fk, fv, lf, dk, dv, fqb, fkb, fvb, dqb, dkb, dvb) = _inproj(
        x, mod, w["w_in"], w["b_f"], cos_t, sin_t, bb, tt, transposed=past is None)

    lf_t = jnp.swapaxes(lf, 1, 2)
    if past is None:
        cq_t, e = _cum(lf_t, "rows")
        o = _attn_prompt(fqb, fkb, e, fvb, dqb, dkb, dvb, cq_t,
                         w["lambda_vecs"], w["subln_g"], lam_init)
        conv_prev = jnp.zeros((B, CONV_WIDTH - 1, D_FF), F32)
    else:
        pfk, pfv, plf, pdk, pdv, conv_prev = past
        S = _round_up(P + LANES, MXU_DIM)
        lf_all = jnp.concatenate(
            [jnp.swapaxes(plf, 1, 2), lf_t, jnp.zeros((B, FOX_HEADS, S - P - T), F32)], axis=2)
        cq_t, e_t = _cum(lf_all, "lanes")
        cq = cq_t[:, :, P:P + T].reshape(B, FOX_HEADS * T, 1)
        keys_on_lanes = lambda c: jnp.transpose(c, (0, 2, 3, 1)).reshape(B, WIDTH, P)
        o = _attn_sample(fqb, fkb, fvb, dqb, dkb, dvb,
                         keys_on_lanes(pfk), keys_on_lanes(pfv), keys_on_lanes(pdk),
                         pdv.reshape(B, P * DIFF_HEADS, PAIR),
                         e_t, cq, w["lambda_vecs"], w["subln_g"], lam_init)

    y, conv = _post(x, o, mod, conv_prev, w["w_o"], w["ln1_g"], w["ln1_b"], w["w_up"],
                    w["conv_w"], w["conv_b"], w["w_down"], w["ln2_g"], w["ln2_b"], bb, tt, alpha)
    state = (fk.reshape(B, T, FOX_HEADS, HEAD_DIM), fv.reshape(B, T, FOX_HEADS, HEAD_DIM), lf,
             dk.reshape(B, T, 2 * DIFF_HEADS, HEAD_DIM), dv.reshape(B, T, DIFF_HEADS, 2 * HEAD_DIM), conv)
    return y, state


def kernel(x_prompt, x_sample, c_prompt, c_sample, cache_fox_k, cache_fox_v, cache_fox_logf, cache_diff_k, cache_diff_v, state_ffn_conv, w_ada, b_ada, w_in, b_f, lambda_vecs, subln_g, w_o, ln1_g, ln1_b, w_up, conv_w, conv_b, w_down, ln2_g, ln2_b):
    depth = w_ada.shape[0]
    alpha = (2 * depth) ** 0.25
    nb = c_prompt.shape[0]
    yp, ys = x_prompt, x_sample
    c_all = jnp.concatenate([c_prompt, c_sample], axis=0)
    p_states, s_states = [], []
    for l in range(depth):
        lam_init = 0.8 - 0.6 * math.exp(-0.3 * l)
        w_in_l = w_in[l]
        gate_w = jnp.pad(w_in_l[:, OFF_FF:OFF_FF + FOX_HEADS], ((0, 0), (0, GATE_COLS - FOX_HEADS)))
        w = {
            "w_in": jnp.concatenate(
                [w_in_l[:, :OFF_FF], gate_w, w_in_l[:, OFF_FF + FOX_HEADS:]], axis=1).astype(BF16),
            "b_f": jnp.pad(b_f[l], (0, GATE_COLS - FOX_HEADS)).reshape(1, GATE_COLS),
            "lambda_vecs": lambda_vecs[l],
            "subln_g": subln_g[l].reshape(1, PAIR),
            "w_o": w_o[l].astype(BF16),
            "ln1_g": ln1_g[l].reshape(1, D_MODEL), "ln1_b": ln1_b[l].reshape(1, D_MODEL),
            "w_up": w_up[l].astype(BF16),
            "conv_w": conv_w[l], "conv_b": conv_b[l].reshape(1, D_FF),
            "w_down": w_down[l].astype(BF16),
            "ln2_g": ln2_g[l].reshape(1, D_MODEL), "ln2_b": ln2_b[l].reshape(1, D_MODEL),
        }
        mod = _ada(c_all, w_ada[l], b_ada[l]).reshape(c_all.shape[0], 6, D_MODEL)
        yp, st_p = _layer(yp, mod[:nb], None, w, lam_init, alpha)
        past = (cache_fox_k[l], cache_fox_v[l], cache_fox_logf[l], cache_diff_k[l], cache_diff_v[l],
                state_ffn_conv[l])
        ys, st_s = _layer(ys, mod[nb:], past, w, lam_init, alpha)
        p_states.append(st_p)
        s_states.append(st_s)
    p_out = [jnp.stack(a, axis=0) for a in zip(*p_states)]
    s_out = [jnp.stack(a, axis=0) for a in zip(*s_states)]
    return (yp, ys, *p_out, *s_out)
```

```python
import functools
import math

import jax
import jax.numpy as jnp
from jax import lax
from jax.experimental import pallas as pl
from jax.experimental.pallas import tpu as pltpu

F32 = jnp.float32
BF16 = jnp.bfloat16

D_MODEL = 1024
CHUNK = 64
FOX_HEADS = 8
DIFF_HEADS = 4
HEAD_DIM = 64
WIDTH = 512
PAIR = 2 * HEAD_DIM
N_PAIRS = WIDTH // PAIR
D_FF = 2816
CONV_WIDTH = 3
ROPE_THETA = 10000.0
LN_EPS = 1e-5
RMS_EPS = 1e-6
NEG_BIG = -1e30
Q_SCALE = HEAD_DIM ** -0.5
LOG2E = math.log2(math.e)

LANES = 128
MXU_DIM = 256
VMEM_LIMIT_BYTES = 56 * 1024 * 1024

GATE_COLS = LANES
OFF_FQ, OFF_FK, OFF_FV = 0, WIDTH, 2 * WIDTH
OFF_FF = 3 * WIDTH
OFF_DQ = OFF_FF + GATE_COLS
OFF_DK, OFF_DV = OFF_DQ + WIDTH, OFF_DQ + 2 * WIDTH
IN_COLS_PADDED = OFF_DV + WIDTH

KV_TILE = 256
TOKEN_TILE = 512
FF_TILE = 256
SUM_ROWS = 16


def _params(*sem):
    return pltpu.CompilerParams(dimension_semantics=sem, vmem_limit_bytes=VMEM_LIMIT_BYTES)


def _resident(shape):
    nd = len(shape)
    return pl.BlockSpec(shape, lambda *_: (0,) * nd, pipeline_mode=pl.Buffered(1))


def _ada_kernel(c_ref, w_ref, b_ref, o_ref):
    c = c_ref[...]
    s = c * (1.0 / (1.0 + jnp.exp(-c)))
    o_ref[...] = jnp.dot(s.astype(BF16), w_ref[...].astype(BF16),
                         preferred_element_type=F32) + b_ref[...]


def _ada(c, w_ada, b_ada):
    n, d = c.shape
    cols = w_ada.shape[1]
    tn = cols // 4
    return pl.pallas_call(
        _ada_kernel,
        grid=(cols // tn,),
        in_specs=[pl.BlockSpec((n, d), lambda j: (0, 0)),
                  pl.BlockSpec((d, tn), lambda j: (0, j)),
                  pl.BlockSpec((1, tn), lambda j: (0, j))],
        out_specs=pl.BlockSpec((n, tn), lambda j: (0, j)),
        out_shape=jax.ShapeDtypeStruct((n, cols), F32),
        compiler_params=_params("arbitrary"),
        name="ada",
    )(c, w_ada, b_ada.reshape(1, cols))


def _inproj_kernel(x_ref, mod_ref, w_ref, bf_ref, cos_ref, sin_ref,
                   fk_o, fv_o, lf_o, dk_o, dv_o,
                   fqb_o, fkb_o, fvb_o, dqb_o, dkb_o, dvb_o, *, transposed):
    bb, tt, d = x_ref.shape
    m = mod_ref[...]
    u = x_ref[...] * (1.0 + m[:, 1:2, :]) + m[:, 0:1, :]
    u = u.reshape(bb * tt, d).astype(BF16)
    q_scale = Q_SCALE * LOG2E

    def proj(off, width):
        return jnp.dot(u, w_ref[:, off:off + width], preferred_element_type=F32)

    def put(o_ref, v):
        o_ref[...] = v.reshape(o_ref.shape).astype(o_ref.dtype)

    def put_qv(o_ref, v):
        if transposed:
            for c in range(tt // KV_TILE):
                o_ref[0, c] = v[c * KV_TILE:(c + 1) * KV_TILE, :].T.astype(o_ref.dtype)
        else:
            put(o_ref, v)

    fq = proj(OFF_FQ, WIDTH)
    put_qv(fqb_o, fq * q_scale)
    fk = proj(OFF_FK, WIDTH)
    put(fk_o, fk)
    put(fkb_o, fk)
    fv = proj(OFF_FV, WIDTH)
    put(fv_o, fv)
    put_qv(fvb_o, fv)

    zf = proj(OFF_FF, GATE_COLS) + bf_ref[...]
    lf = jnp.minimum(zf, 0.0) - jnp.log1p(jnp.exp(-jnp.abs(zf)))
    put(lf_o, lf[:, :FOX_HEADS])

    cos = cos_ref[...]
    sin = sin_ref[...]
    lane = lax.broadcasted_iota(jnp.int32, (1, WIDTH), 1)
    first_half = (lane % HEAD_DIM) < (HEAD_DIM // 2)

    def rope(v):
        partner = jnp.where(first_half, pltpu.roll(v, WIDTH - HEAD_DIM // 2, 1),
                            pltpu.roll(v, HEAD_DIM // 2, 1))
        return v * cos + partner * sin

    dq = rope(proj(OFF_DQ, WIDTH))
    put_qv(dqb_o, dq * q_scale)
    dk = rope(proj(OFF_DK, WIDTH))
    put(dk_o, dk)
    put(dkb_o, dk)
    dv = proj(OFF_DV, WIDTH)
    for hd in range(DIFF_HEADS):
        dv_o[:, :, hd, :] = dv[:, hd * PAIR:(hd + 1) * PAIR].reshape(bb, tt, PAIR)
    put_qv(dvb_o, dv)


def _inproj(x, mod, w_pad, bf_pad, cos_t, sin_t, bb, tt, transposed):
    B, T, d = x.shape
    rows = bb * tt
    grid = (B // bb, T // tt)
    assert not transposed or (bb == 1 and tt % KV_TILE == 0)
    tok = lambda w: pl.BlockSpec((bb, tt, w), lambda b, t: (b, t, 0))
    tab = pl.BlockSpec((rows, WIDTH), (lambda b, t: (t, 0)) if bb == 1 else (lambda b, t: (0, 0)))
    sds = lambda w, dt: jax.ShapeDtypeStruct((B, T, w), dt)
    if transposed:
        qv = pl.BlockSpec((1, tt // KV_TILE, WIDTH, KV_TILE), lambda b, t: (b, t, 0, 0))
        qv_sds = jax.ShapeDtypeStruct((B, T // KV_TILE, WIDTH, KV_TILE), BF16)
    else:
        qv, qv_sds = tok(WIDTH), sds(WIDTH, BF16)
    return pl.pallas_call(
        functools.partial(_inproj_kernel, transposed=transposed),
        grid=grid,
        in_specs=[tok(d),
                  pl.BlockSpec((bb, 6, d), lambda b, t: (b, 0, 0)),
                  _resident(w_pad.shape),
                  _resident(bf_pad.shape),
                  tab, tab],
        out_specs=[tok(WIDTH), tok(WIDTH), tok(FOX_HEADS), tok(WIDTH),
                   pl.BlockSpec((bb, tt, DIFF_HEADS, PAIR), lambda b, t: (b, t, 0, 0)),
                   qv, tok(WIDTH), qv, qv, tok(WIDTH), qv],
        out_shape=[sds(WIDTH, F32), sds(WIDTH, F32), sds(FOX_HEADS, F32), sds(WIDTH, F32),
                   jax.ShapeDtypeStruct((B, T, DIFF_HEADS, PAIR), F32),
                   qv_sds, sds(WIDTH, BF16), qv_sds, qv_sds, sds(WIDTH, BF16), qv_sds],
        compiler_params=_params("arbitrary", "arbitrary"),
        name="inproj",
    )(x, mod, w_pad, bf_pad, cos_t, sin_t)


def _split3(x):
    hi = x.astype(BF16).astype(F32)
    r1 = x - hi
    mid = r1.astype(BF16).astype(F32)
    lo = (r1 - mid).astype(BF16).astype(F32)
    return hi, mid, lo


def _cum_kernel(x_ref, o_ref, e_ref, *, keys_on_lanes):
    S = x_ref.shape[2]
    r = lax.broadcasted_iota(jnp.int32, (MXU_DIM, MXU_DIM), 0)
    c = lax.broadcasted_iota(jnp.int32, (MXU_DIM, MXU_DIM), 1)
    tri = jnp.where(r <= c, 1.0, 0.0).astype(BF16)
    carry = jnp.zeros((FOX_HEADS, 1), F32)
    for k in range(S // MXU_DIM):
        sl = slice(k * MXU_DIM, (k + 1) * MXU_DIM)
        hi, mid, lo = _split3(x_ref[0, :, sl])
        part = (jnp.dot(hi.astype(BF16), tri, preferred_element_type=F32)
                + jnp.dot(mid.astype(BF16), tri, preferred_element_type=F32)
                + jnp.dot(lo.astype(BF16), tri, preferred_element_type=F32))
        part = part + carry
        carry = part[:, MXU_DIM - 1:MXU_DIM]
        scaled = part * LOG2E
        o_ref[0, :, sl] = scaled
        pieces = _split3(-scaled)
        pad = jnp.zeros((LANES - 3 * FOX_HEADS, MXU_DIM), F32)
        bias = jnp.concatenate(pieces + (pad,), axis=0)
        if keys_on_lanes:
            e_ref[0, :, sl] = bias.astype(BF16)
        else:
            e_ref[0, sl, :] = bias.T.astype(BF16)


def _cum(lf_t, bias_layout):
    B, H, S = lf_t.shape
    spec = pl.BlockSpec((1, H, S), lambda b: (b, 0, 0))
    e_shape = (B, LANES, S) if bias_layout == "lanes" else (B, S, LANES)
    return pl.pallas_call(
        functools.partial(_cum_kernel, keys_on_lanes=bias_layout == "lanes"),
        grid=(B,), in_specs=[spec],
        out_specs=[spec, pl.BlockSpec((1,) + e_shape[1:], lambda b: (b, 0, 0))],
        out_shape=[jax.ShapeDtypeStruct((B, H, S), F32), jax.ShapeDtypeStruct(e_shape, BF16)],
        compiler_params=_params("arbitrary"),
        name="cum",
    )(lf_t)


def _lambda_value(lv_ref, lam_init):
    lv = lv_ref[...]
    a = jnp.sum(lv[0:1] * lv[1:2], axis=1, keepdims=True)
    b = jnp.sum(lv[2:3] * lv[3:4], axis=1, keepdims=True)
    return jnp.exp(a) - jnp.exp(b) + lam_init


def _attn_prompt_kernel(fq_ref, fk_ref, e_ref, fv_ref, dq_ref, dk_ref, dv_ref, cq_ref,
                        lv_ref, g_ref, o_ref, s_ref, acc_ref, *, lam_init):
    tq = tk = KV_TILE
    i = pl.program_id(1)
    srow = lax.broadcasted_iota(jnp.int32, (PAIR, tq), 0)
    first = srow < HEAD_DIM
    krow = lax.broadcasted_iota(jnp.int32, (tk, tq), 0)
    qcol = lax.broadcasted_iota(jnp.int32, (tk, tq), 1)
    causal = krow <= qcol
    chunk_causal = (krow // CHUNK) <= (qcol // CHUNK)

    def attend(units, mask, rows):
        chains = []
        for wa, wb, cqa, cqb, keys, vta, vtb in units:
            chains += [(wa, cqa, keys, vta), (wb, cqb, keys, vtb)]
        n = len(chains)

        def score(c, j):
            w, _, keys, _ = chains[c]
            s = jnp.dot(keys(j), w, preferred_element_type=F32)
            s_ref[c] = s
            return jnp.max(s, axis=0, keepdims=True)

        ones = jnp.ones((SUM_ROWS, tk), BF16)
        used = rows + SUM_ROWS

        def consume(c, j, mb, msk, m):
            _, cq, _, vt = chains[c]
            s = s_ref[c]
            if msk is not None:
                s = jnp.where(msk, s, NEG_BIG)
                mb = jnp.max(s, axis=0, keepdims=True)
            m_new = jnp.maximum(m, mb if cq is None else mb + cq)
            p = jnp.exp2(s - (m_new if cq is None else m_new - cq))
            alpha = jnp.exp2(m - m_new)
            acc_ref[c, :used] = alpha * acc_ref[c, :used] + jnp.dot(
                jnp.concatenate([vt(j), ones], axis=0), p.astype(BF16), preferred_element_type=F32)
            return m_new

        def body(j, carry):
            ms, mbs = list(carry[0]), list(carry[1])
            for c in range(n):
                ms[c] = consume(c, j, mbs[c], None, ms[c])
                mbs[c] = score(c, j + 1)
            return tuple(ms), tuple(mbs)

        acc_ref[...] = jnp.zeros_like(acc_ref)
        ms, _ = lax.fori_loop(
            0, i, body,
            (tuple(jnp.full((1, tq), NEG_BIG, F32) for _ in range(n)), tuple(score(c, 0) for c in range(n))))
        outs = []
        for c in range(n):
            consume(c, i, None, mask, ms[c])
            outs.append(acc_ref[c, :rows] / acc_ref[c, rows:rows + 1])
        return list(zip(outs[0::2], outs[1::2]))

    def rows_of(j):
        return pl.ds(pl.multiple_of(j * tk, tk), tk)

    def select(head):
        return jnp.where((srow % FOX_HEADS == head) & (srow < 3 * FOX_HEADS), 1.0, 0.0).astype(BF16)

    def pair_lanes(p):
        return slice(p * PAIR, (p + 1) * PAIR)

    zero = jnp.zeros((PAIR, tq), BF16)

    def fox_unit(pair):
        lanes = pair_lanes(pair)
        qt = fq_ref[0, 0, lanes, :]
        return (jnp.concatenate([jnp.where(first, qt, zero), select(2 * pair)], axis=0),
                jnp.concatenate([jnp.where(first, zero, qt), select(2 * pair + 1)], axis=0),
                cq_ref[0, 2 * pair:2 * pair + 1, :], cq_ref[0, 2 * pair + 1:2 * pair + 2, :],
                lambda j: jnp.concatenate([fk_ref[0, rows_of(j), lanes], e_ref[0, rows_of(j), :]], axis=1),
                lambda j: fv_ref[0, j, pair * PAIR:pair * PAIR + HEAD_DIM, :],
                lambda j: fv_ref[0, j, pair * PAIR + HEAD_DIM:(pair + 1) * PAIR, :])

    def diff_unit(hd):
        lanes = pair_lanes(hd)
        qt = dq_ref[0, 0, lanes, :]
        vt = lambda j: dv_ref[0, j, lanes, :]
        return (jnp.where(first, qt, zero), jnp.where(first, zero, qt), None, None,
                lambda j: dk_ref[0, rows_of(j), lanes], vt, vt)

    fox = attend([fox_unit(p) for p in range(N_PAIRS)], causal, HEAD_DIM)
    for pair, (oa, ob) in enumerate(fox):
        o_ref[0, :, pair_lanes(pair)] = jnp.concatenate([oa, ob], axis=0).T.astype(o_ref.dtype)

    lam = _lambda_value(lv_ref, lam_init)
    diff = attend([diff_unit(hd) for hd in range(DIFF_HEADS)], chunk_causal, PAIR)
    for hd, (o1, o2) in enumerate(diff):
        d = o1 - lam * o2
        ms = jnp.mean(d * d, axis=0, keepdims=True)
        dn = (d * lax.rsqrt(ms + RMS_EPS)).T * g_ref[...] * (1.0 - lam_init)
        o_ref[0, :, WIDTH + hd * PAIR:WIDTH + (hd + 1) * PAIR] = dn.astype(o_ref.dtype)


def _attn_prompt(fq_t, fk, e, fv_t, dq_t, dk, dv_t, cq_t, lambda_vecs, subln_g, lam_init):
    B, T, _ = fk.shape
    tq = KV_TILE
    nt = T // tq
    qspec = pl.BlockSpec((1, 1, WIDTH, tq), lambda b, i: (b, i, 0, 0))
    vspec = pl.BlockSpec((1, nt, WIDTH, tq), lambda b, i: (b, 0, 0, 0))
    kspec = pl.BlockSpec((1, T, WIDTH), lambda b, i: (b, 0, 0))
    return pl.pallas_call(
        functools.partial(_attn_prompt_kernel, lam_init=lam_init),
        grid=(B, nt),
        in_specs=[qspec, kspec, pl.BlockSpec((1, T, LANES), lambda b, i: (b, 0, 0)), vspec,
                  qspec, kspec, vspec,
                  pl.BlockSpec((1, FOX_HEADS, tq), lambda b, i: (b, 0, i)),
                  pl.BlockSpec(lambda_vecs.shape, lambda b, i: (0, 0)),
                  pl.BlockSpec(subln_g.shape, lambda b, i: (0, 0))],
        out_specs=pl.BlockSpec((1, tq, 2 * WIDTH), lambda b, i: (b, i, 0)),
        out_shape=jax.ShapeDtypeStruct((B, T, 2 * WIDTH), BF16),
        scratch_shapes=[pltpu.VMEM((2 * N_PAIRS, tq, tq), F32),
                        pltpu.VMEM((2 * N_PAIRS, PAIR + SUM_ROWS, tq), F32)],
        compiler_params=_params("arbitrary", "arbitrary"),
        name="attn_prompt",
    )(fq_t, fk, e, fv_t, dq_t, dk, dv_t, cq_t, lambda_vecs, subln_g)


def _attn_sample_kernel_keys_on_rows(fq_ref, fk_ref, fv_ref, dq_ref, dk_ref, dv_ref,
                                     pfk_ref, pfv_ref, pdk_ref, pdv_ref, e_ref, cq_ref,
                                     lv_ref, g_ref, o_ref, s_ref, snew_ref, acc_ref, *, lam_init):
    T = fq_ref.shape[1]
    P = pfk_ref.shape[1]
    tk = KV_TILE
    n_past = P // tk
    ncol = FOX_HEADS * T
    half = ncol // 2
    new_rows = LANES
    f_idx, d_idx = 0, 1

    row = lambda shape: lax.broadcasted_iota(jnp.int32, shape, 0)
    col = lambda shape: lax.broadcasted_iota(jnp.int32, shape, 1)

    spread = jnp.where(row((LANES, ncol)) == col((LANES, ncol)) % T, 1.0, 0.0).astype(BF16)
    own_head = row((WIDTH, ncol)) // HEAD_DIM == col((WIDTH, ncol)) // T
    select = jnp.where((row((LANES, ncol)) % FOX_HEADS == col((LANES, ncol)) // T)
                       & (row((LANES, ncol)) < 3 * FOX_HEADS), 1.0, 0.0).astype(BF16)

    def pad_rows(x):
        return jnp.concatenate([x, jnp.zeros((new_rows - T, x.shape[1]), x.dtype)], axis=0)

    def transposed(x):
        return pad_rows(x).astype(F32).T.astype(BF16)

    def block_diag(q_ref):
        full = jnp.dot(transposed(q_ref[0]), spread, preferred_element_type=F32)
        return jnp.where(own_head, full, 0.0).astype(BF16)

    w_f = jnp.concatenate([block_diag(fq_ref), select], axis=0)
    w_d = block_diag(dq_ref)
    cq = cq_ref[0]

    def past_scores(j, carry):
        mf, md = carry
        rows = pl.ds(pl.multiple_of(j * tk, tk), tk)
        sf = jnp.dot(jnp.concatenate([pfk_ref[0, rows, :], e_ref[0, rows, :]], axis=1), w_f,
                     preferred_element_type=F32)
        sd = jnp.dot(pdk_ref[0, rows, :], w_d, preferred_element_type=F32)
        s_ref[f_idx, j] = sf
        s_ref[d_idx, j] = sd
        return (jnp.maximum(mf, jnp.max(sf, axis=0, keepdims=True)),
                jnp.maximum(md, jnp.max(sd, axis=0, keepdims=True)))

    neg = jnp.full((1, ncol), NEG_BIG, F32)
    mf, md = lax.fori_loop(0, n_past, past_scores, (neg, neg), unroll=2)

    krow = row((new_rows, ncol))
    qpos = col((new_rows, ncol)) % T
    valid = krow < T
    causal = valid & (krow <= qpos)
    chunk_causal = valid & ((krow + P) // CHUNK <= (qpos + P) // CHUNK)
    sf = jnp.dot(jnp.concatenate([pad_rows(fk_ref[0]), e_ref[0, P:P + new_rows, :]], axis=1), w_f,
                 preferred_element_type=F32)
    sf = jnp.where(causal, sf, NEG_BIG)
    sd = jnp.where(chunk_causal, jnp.dot(pad_rows(dk_ref[0]), w_d, preferred_element_type=F32), NEG_BIG)
    snew_ref[f_idx] = sf
    snew_ref[d_idx] = sd
    mf = jnp.maximum(mf, jnp.max(sf, axis=0, keepdims=True)) + cq
    shift_f = mf - cq
    shift_d = jnp.maximum(md, jnp.max(sd, axis=0, keepdims=True))

    def apply(idx, s, shift, vt, l):
        p = jnp.exp2(s - shift)
        pb = p.astype(BF16)
        acc_ref[idx, 0] += jnp.dot(vt[:WIDTH // 2], pb[:, :half], preferred_element_type=F32)
        acc_ref[idx, 1] += jnp.dot(vt[WIDTH // 2:], pb[:, half:], preferred_element_type=F32)
        return l + jnp.sum(p, axis=0, keepdims=True)

    def past_apply(j, carry):
        lf, ld = carry
        lf = apply(f_idx, s_ref[f_idx, j], shift_f, pfv_ref[0, j], lf)
        ld = apply(d_idx, s_ref[d_idx, j], shift_d, pdv_ref[0, j], ld)
        return lf, ld

    acc_ref[...] = jnp.zeros_like(acc_ref)
    zero = jnp.zeros((1, ncol), F32)
    lf, ld = lax.fori_loop(0, n_past, past_apply, (zero, zero), unroll=2)
    lf = apply(f_idx, snew_ref[f_idx], shift_f, transposed(fv_ref[0]), lf)
    ld = apply(d_idx, snew_ref[d_idx], shift_d, transposed(dv_ref[0]), ld)

    fold = jnp.where(row((half, LANES)) % T == col((half, LANES)), 1.0, 0.0).astype(BF16)
    hrow = row((WIDTH // 2, half))
    hcol = col((WIDTH // 2, half))

    def finish(kept_halves, lane0):
        out_t = jnp.concatenate(
            [jnp.dot(k.astype(BF16), fold, preferred_element_type=F32) for k in kept_halves], axis=0)
        o_ref[0, :, lane0:lane0 + WIDTH] = out_t.T[:T].astype(o_ref.dtype)

    inv_f = 1.0 / lf
    finish([jnp.where(hrow // HEAD_DIM == hcol // T, acc_ref[f_idx, h] * inv_f[:, h * half:(h + 1) * half], 0.0)
            for h in range(2)], 0)

    lam = _lambda_value(lv_ref, lam_init)
    inv_d = 1.0 / ld
    g_col = g_ref[...]
    kept = []
    for h in range(2):
        a = acc_ref[d_idx, h] * inv_d[:, h * half:(h + 1) * half]
        d = a - lam * pltpu.roll(a, half - T, 1)
        d3 = d.reshape(WIDTH // 2 // PAIR, PAIR, half)
        ms = jnp.mean(d3 * d3, axis=1, keepdims=True)
        dn = (d3 * lax.rsqrt(ms + RMS_EPS)).reshape(WIDTH // 2, half) * g_col * (1.0 - lam_init)
        kept.append(jnp.where(hcol // T == 2 * (hrow // PAIR), dn, 0.0))
    finish(kept, WIDTH)


def _attn_sample_keys_on_rows(fq, fk, fv, dq, dk, dv, pfk, pfv_t, pdk, pdv_t, e, cq, lambda_vecs, g_col, lam_init):
    B, T, _ = fq.shape
    P = pfk.shape[1]
    tk = KV_TILE
    ncol = FOX_HEADS * T
    assert ncol == 2 * LANES and P % tk == 0 and e.shape[1] >= P + LANES
    new = pl.BlockSpec((1, T, WIDTH), lambda b: (b, 0, 0))
    keys = pl.BlockSpec((1, P, WIDTH), lambda b: (b, 0, 0))
    vals = pl.BlockSpec((1, P // tk, WIDTH, tk), lambda b: (b, 0, 0, 0))
    return pl.pallas_call(
        functools.partial(_attn_sample_kernel_keys_on_rows, lam_init=lam_init),
        grid=(B,),
        in_specs=[new] * 6 + [keys, vals, keys, vals,
                              pl.BlockSpec((1,) + e.shape[1:], lambda b: (b, 0, 0)),
                              pl.BlockSpec((1, 1, ncol), lambda b: (b, 0, 0)),
                              pl.BlockSpec(lambda_vecs.shape, lambda b: (0, 0)),
                              pl.BlockSpec(g_col.shape, lambda b: (0, 0))],
        out_specs=pl.BlockSpec((1, T, 2 * WIDTH), lambda b: (b, 0, 0)),
        out_shape=jax.ShapeDtypeStruct((B, T, 2 * WIDTH), BF16),
        scratch_shapes=[pltpu.VMEM((2, P // tk, tk, ncol), F32),
                        pltpu.VMEM((2, LANES, ncol), F32),
                        pltpu.VMEM((2, 2, WIDTH // 2, ncol // 2), F32)],
        compiler_params=_params("arbitrary"),
        name="attn_sample",
    )(fq, fk, fv, dq, dk, dv, pfk, pfv_t, pdk, pdv_t, e, cq, lambda_vecs, g_col)


def _attn_sample_kernel(fq_ref, fk_ref, fv_ref, dq_ref, dk_ref, dv_ref,
                        pfk_ref, pfv_ref, pdk_ref, pdv_ref, et_ref, cq_ref,
                        lv_ref, g_ref, o_ref, s_ref, *, lam_init):
    T = fq_ref.shape[1]
    P = pfk_ref.shape[2]
    tk = KV_TILE
    n_past = P // tk
    nrow = FOX_HEADS * T
    prow = 2 * T
    f_idx, d_idx = 0, 1
    nt_dims = (((1,), (1,)), ((), ()))

    row = lambda shape: lax.broadcasted_iota(jnp.int32, shape, 0)
    col = lambda shape: lax.broadcasted_iota(jnp.int32, shape, 1)

    own_head = row((nrow, WIDTH)) // T == col((nrow, WIDTH)) // HEAD_DIM
    select_t = jnp.where((col((nrow, LANES)) % FOX_HEADS == row((nrow, LANES)) // T)
                         & (col((nrow, LANES)) < 3 * FOX_HEADS), 1.0, 0.0).astype(BF16)

    def block_diag(q_ref):
        q = q_ref[0]
        return jnp.where(own_head, jnp.concatenate([q] * FOX_HEADS, axis=0), jnp.zeros((nrow, WIDTH), BF16))

    def pad_rows(x):
        return jnp.concatenate([x, jnp.zeros((LANES - T, x.shape[1]), x.dtype)], axis=0)

    def lane_halves(x, op):
        return op(x[:, :LANES], x[:, LANES:])

    qf = block_diag(fq_ref)
    qd = block_diag(dq_ref)
    lhs_f = jnp.concatenate([qf, select_t], axis=1)

    mf = md = jnp.full((nrow, LANES), NEG_BIG, F32)
    for j in range(n_past):
        cols = slice(j * tk, (j + 1) * tk)
        kf = jnp.concatenate([pfk_ref[0, :, cols].astype(BF16), et_ref[0, :, cols]], axis=0)
        sf = jnp.dot(lhs_f, kf, preferred_element_type=F32)
        sd = jnp.dot(qd, pdk_ref[0, :, cols].astype(BF16), preferred_element_type=F32)
        s_ref[f_idx, j] = sf
        s_ref[d_idx, j] = sd
        mf = jnp.maximum(mf, lane_halves(sf, jnp.maximum))
        md = jnp.maximum(md, lane_halves(sd, jnp.maximum))

    key = col((nrow, LANES))
    qpos = row((nrow, LANES)) % T
    valid = key < T
    causal = valid & (key <= qpos)
    chunk_causal = valid & ((key + P) // CHUNK <= (qpos + P) // CHUNK)
    sf_new = (lax.dot_general(qf, pad_rows(fk_ref[0]), nt_dims, preferred_element_type=F32)
              + jnp.dot(select_t, et_ref[0, :, P:P + LANES], preferred_element_type=F32))
    sf_new = jnp.where(causal, sf_new, NEG_BIG)
    sd_new = jnp.where(chunk_causal,
                       lax.dot_general(qd, pad_rows(dk_ref[0]), nt_dims, preferred_element_type=F32), NEG_BIG)
    cq = cq_ref[0]
    m_f = jnp.max(jnp.maximum(mf, sf_new), axis=1, keepdims=True) + cq
    shift_f = jnp.broadcast_to(m_f - cq, (nrow, LANES))
    shift_d = jnp.broadcast_to(jnp.max(jnp.maximum(md, sd_new), axis=1, keepdims=True), (nrow, LANES))

    def probs(s, shift):
        return jnp.exp2(s - jnp.concatenate([shift] * (s.shape[1] // LANES), axis=1))

    lf = ld = jnp.zeros((nrow, LANES), F32)
    acc_f = [jnp.zeros((prow, PAIR), F32) for _ in range(N_PAIRS)]
    acc_d = [jnp.zeros((prow, PAIR), F32) for _ in range(DIFF_HEADS)]
    for j in range(n_past):
        cols = slice(j * tk, (j + 1) * tk)
        pf = probs(s_ref[f_idx, j], shift_f)
        pd = probs(s_ref[d_idx, j], shift_d)
        lf = lf + lane_halves(pf, jnp.add)
        ld = ld + lane_halves(pd, jnp.add)
        pf, pd = pf.astype(BF16), pd.astype(BF16)
        for u in range(N_PAIRS):
            rows = slice(u * prow, (u + 1) * prow)
            acc_f[u] = acc_f[u] + lax.dot_general(
                pf[rows], pfv_ref[0, u * PAIR:(u + 1) * PAIR, cols].astype(BF16), nt_dims,
                preferred_element_type=F32)
            v = pdv_ref[0, pl.ds(j * tk * DIFF_HEADS + u, tk, stride=DIFF_HEADS), :].astype(BF16)
            acc_d[u] = acc_d[u] + jnp.dot(pd[rows], v, preferred_element_type=F32)
    pf = jnp.exp2(sf_new - shift_f)
    pd = jnp.exp2(sd_new - shift_d)
    lf = jnp.sum(lf + pf, axis=1, keepdims=True)
    ld = jnp.sum(ld + pd, axis=1, keepdims=True)
    pf, pd = pf.astype(BF16), pd.astype(BF16)
    fv_new, dv_new = pad_rows(fv_ref[0]), pad_rows(dv_ref[0])
    low = col((T, PAIR)) < HEAD_DIM
    lam = _lambda_value(lv_ref, lam_init)
    for u in range(N_PAIRS):
        rows = slice(u * prow, (u + 1) * prow)
        lanes = slice(u * PAIR, (u + 1) * PAIR)
        a = (acc_f[u] + jnp.dot(pf[rows], fv_new[:, lanes], preferred_element_type=F32)) / lf[rows]
        o_ref[0, :, lanes] = jnp.where(low, a[:T], a[T:]).astype(o_ref.dtype)
        a = (acc_d[u] + jnp.dot(pd[rows], dv_new[:, lanes], preferred_element_type=F32)) / ld[rows]
        d = a[:T] - lam * a[T:]
        ms = jnp.mean(d * d, axis=-1, keepdims=True)
        dn = d * lax.rsqrt(ms + RMS_EPS) * g_ref[...] * (1.0 - lam_init)
        o_ref[0, :, WIDTH + u * PAIR:WIDTH + (u + 1) * PAIR] = dn.astype(o_ref.dtype)


def _attn_sample(fq, fk, fv, dq, dk, dv, pfk_t, pfv_t, pdk_t, pdv_rows, e_t, cq, lambda_vecs, subln_g,
                 lam_init):
    B, T, _ = fq.shape
    P = pfk_t.shape[2]
    tk = KV_TILE
    nrow = FOX_HEADS * T
    assert N_PAIRS == DIFF_HEADS and T <= LANES and P % tk == 0 and e_t.shape[2] >= P + LANES
    new = pl.BlockSpec((1, T, WIDTH), lambda b: (b, 0, 0))
    cached = pl.BlockSpec((1, WIDTH, P), lambda b: (b, 0, 0))
    return pl.pallas_call(
        functools.partial(_attn_sample_kernel, lam_init=lam_init),
        grid=(B,),
        in_specs=[new] * 6 + [cached, cached, cached,
                              pl.BlockSpec((1, P * DIFF_HEADS, PAIR), lambda b: (b, 0, 0)),
                              pl.BlockSpec((1,) + e_t.shape[1:], lambda b: (b, 0, 0)),
                              pl.BlockSpec((1, nrow, 1), lambda b: (b, 0, 0)),
                              pl.BlockSpec(lambda_vecs.shape, lambda b: (0, 0)),
                              pl.BlockSpec(subln_g.shape, lambda b: (0, 0))],
        out_specs=pl.BlockSpec((1, T, 2 * WIDTH), lambda b: (b, 0, 0)),
        out_shape=jax.ShapeDtypeStruct((B, T, 2 * WIDTH), BF16),
        scratch_shapes=[pltpu.VMEM((2, P // tk, nrow, tk), F32)],
        compiler_params=_params("arbitrary"),
        name="attn_sample",
    )(fq, fk, fv, dq, dk, dv, pfk_t, pfv_t, pdk_t, pdv_rows, e_t, cq, lambda_vecs, subln_g)


def _post_norm(x, h, gate, g, b, alpha):
    y = alpha * x + gate * h
    mu = jnp.mean(y, axis=-1, keepdims=True)
    yc = y - mu
    var = jnp.mean(yc * yc, axis=-1, keepdims=True)
    return yc * lax.rsqrt(var + LN_EPS) * g + b


def _post_kernel(x_ref, o_ref, mod_ref, prev_ref, wo_ref, ln1g_ref, ln1b_ref,
                 wup_ref, cw_ref, cb_ref, wdn_ref, ln2g_ref, ln2b_ref,
                 y_ref, conv_ref, carry_ref, hid_ref, *, alpha):
    bb, tt, d = x_ref.shape
    rows = bb * tt
    keep = CONV_WIDTH - 1
    t = pl.program_id(1)

    @pl.when(t == 0)
    def _():
        carry_ref[...] = prev_ref[...]

    m = mod_ref[...]

    def rows3(v):
        return jnp.broadcast_to(v, (bb, tt, v.shape[-1])).reshape(rows, v.shape[-1])

    x = x_ref[...].reshape(rows, d)
    h = jnp.dot(o_ref[...].reshape(rows, d), wo_ref[...], preferred_element_type=F32)
    x1 = _post_norm(x, h, rows3(m[:, 2:3, :]), ln1g_ref[...], ln1b_ref[...], alpha)
    u2 = (x1 * (1.0 + rows3(m[:, 4:5, :])) + rows3(m[:, 3:4, :])).astype(BF16)

    tpos = lax.broadcasted_iota(jnp.int32, (bb, tt, 1), 1).reshape(rows, 1)
    for c in range(D_FF // FF_TILE):
        cols = slice(c * FF_TILE, (c + 1) * FF_TILE)
        a = jnp.dot(u2, wup_ref[:, cols], preferred_element_type=F32)
        g = jnp.dot(u2, wup_ref[:, D_FF + c * FF_TILE:D_FF + (c + 1) * FF_TILE],
                    preferred_element_type=F32)
        prev = carry_ref[:, :, cols]
        p2 = rows3(prev[:, 0:1, :])
        p1 = rows3(prev[:, 1:2, :])
        am1 = jnp.where(tpos == 0, p1, pltpu.roll(a, 1, 0))
        am2 = jnp.where(tpos == 0, p2, jnp.where(tpos == 1, p1, pltpu.roll(a, 2, 0)))
        cw = cw_ref[:, cols]
        conv = cb_ref[:, cols] + am2 * cw[0:1] + am1 * cw[1:2] + a * cw[2:3]
        hid = conv * (1.0 / (1.0 + jnp.exp(-conv))) * g
        hid_ref[:, cols] = hid.astype(BF16)
        last = a.reshape(bb, tt, FF_TILE)[:, tt - keep:, :]
        carry_ref[:, :, cols] = last
        conv_ref[:, :, cols] = last

    f_out = jnp.dot(hid_ref[...], wdn_ref[...], preferred_element_type=F32)
    y = _post_norm(x1, f_out, rows3(m[:, 5:6, :]), ln2g_ref[...], ln2b_ref[...], alpha)
    y_ref[...] = y.reshape(bb, tt, d)


def _post(x, o, mod, conv_prev, wo, ln1g, ln1b, wup, cw, cb, wdn, ln2g, ln2b, bb, tt, alpha):
    B, T, d = x.shape
    keep = CONV_WIDTH - 1
    tok = pl.BlockSpec((bb, tt, d), lambda b, t: (b, t, 0))
    per_b = lambda r, w: pl.BlockSpec((bb, r, w), lambda b, t: (b, 0, 0))
    return pl.pallas_call(
        functools.partial(_post_kernel, alpha=alpha),
        grid=(B // bb, T // tt),
        in_specs=[tok, tok, per_b(6, d), per_b(keep, D_FF),
                  _resident(wo.shape), _resident(ln1g.shape), _resident(ln1b.shape),
                  _resident(wup.shape), _resident(cw.shape), _resident(cb.shape),
                  _resident(wdn.shape), _resident(ln2g.shape), _resident(ln2b.shape)],
        out_specs=[tok, per_b(keep, D_FF)],
        out_shape=[jax.ShapeDtypeStruct((B, T, d), F32),
                   jax.ShapeDtypeStruct((B, keep, D_FF), F32)],
        scratch_shapes=[pltpu.VMEM((bb, keep, D_FF), F32),
                        pltpu.VMEM((bb * tt, D_FF), BF16)],
        compiler_params=_params("arbitrary", "arbitrary"),
        name="post",
    )(x, o, mod, conv_prev, wo, ln1g, ln1b, wup, cw, cb, wdn, ln2g, ln2b)


def _rope_tables(pos0, T):
    half = HEAD_DIM // 2
    inv = ROPE_THETA ** (-jnp.arange(0, HEAD_DIM, 2, dtype=F32) / HEAD_DIM)
    ang = (pos0 + jnp.arange(T)).astype(F32)[:, None] * inv[None, :]
    cos, sin = jnp.cos(ang), jnp.sin(ang)
    reps = WIDTH // HEAD_DIM
    return (jnp.tile(jnp.concatenate([cos, cos], axis=1), (1, reps)),
            jnp.tile(jnp.concatenate([-sin, sin], axis=1), (1, reps)))


def _round_up(n, k):
    return -(-n // k) * k


def _layer(x, mod, past, w, lam_init, alpha):
    B, T, d = x.shape
    P = 0 if past is None else past[0].shape[1]
    if T % TOKEN_TILE == 0:
        bb, tt = 1, TOKEN_TILE
    else:
        tt = T
        bb = math.gcd(B, max(1, TOKEN_TILE // T))
    cos_t, sin_t = _rope_tables(P, T)
    if bb > 1:
        cos_t, sin_t = jnp.tile(cos_t, (bb, 1)), jnp.tile(sin_t, (bb, 1))
    (fk, fv, lf, dk, dv, fqb, fkb, fvb, dqb, dkb, dvb) = _inproj(
        x, mod, w["w_in"], w["b_f"], cos_t, sin_t, bb, tt, transposed=past is None)

    lf_t = jnp.swapaxes(lf, 1, 2)
    if past is None:
        cq_t, e = _cum(lf_t, "rows")
        o = _attn_prompt(fqb, fkb, e, fvb, dqb, dkb, dvb, cq_t,
                         w["lambda_vecs"], w["subln_g"], lam_init)
        conv_prev = jnp.zeros((B, CONV_WIDTH - 1, D_FF), F32)
    else:
        pfk, pfv, plf, pdk, pdv, conv_prev = past
        S = _round_up(P + LANES, MXU_DIM)
        lf_all = jnp.concatenate(
            [jnp.swapaxes(plf, 1, 2), lf_t, jnp.zeros((B, FOX_HEADS, S - P - T), F32)], axis=2)
        cq_t, e_t = _cum(lf_all, "lanes")
        cq = cq_t[:, :, P:P + T].reshape(B, FOX_HEADS * T, 1)
        keys_on_lanes = lambda c: jnp.transpose(c, (0, 2, 3, 1)).reshape(B, WIDTH, P)
        o = _attn_sample(fqb, fkb, fvb, dqb, dkb, dvb,
                         keys_on_lanes(pfk), keys_on_lanes(pfv), keys_on_lanes(pdk),
                         pdv.reshape(B, P * DIFF_HEADS, PAIR),
                         e_t, cq, w["lambda_vecs"], w["subln_g"], lam_init)

    y, conv = _post(x, o, mod, conv_prev, w["w_o"], w["ln1_g"], w["ln1_b"], w["w_up"],
                    w["conv_w"], w["conv_b"], w["w_down"], w["ln2_g"], w["ln2_b"], bb, tt, alpha)
    state = (fk.reshape(B, T, FOX_HEADS, HEAD_DIM), fv.reshape(B, T, FOX_HEADS, HEAD_DIM), lf,
             dk.reshape(B, T, 2 * DIFF_HEADS, HEAD_DIM), dv, conv)
    return y, state


def kernel(x_prompt, x_sample, c_prompt, c_sample, cache_fox_k, cache_fox_v, cache_fox_logf, cache_diff_k, cache_diff_v, state_ffn_conv, w_ada, b_ada, w_in, b_f, lambda_vecs, subln_g, w_o, ln1_g, ln1_b, w_up, conv_w, conv_b, w_down, ln2_g, ln2_b):
    depth = w_ada.shape[0]
    alpha = (2 * depth) ** 0.25
    nb = c_prompt.shape[0]
    yp, ys = x_prompt, x_sample
    c_all = jnp.concatenate([c_prompt, c_sample], axis=0)
    p_states, s_states = [], []
    for l in range(depth):
        lam_init = 0.8 - 0.6 * math.exp(-0.3 * l)
        w_in_l = w_in[l]
        gate_w = jnp.pad(w_in_l[:, OFF_FF:OFF_FF + FOX_HEADS], ((0, 0), (0, GATE_COLS - FOX_HEADS)))
        w = {
            "w_in": jnp.concatenate(
                [w_in_l[:, :OFF_FF], gate_w, w_in_l[:, OFF_FF + FOX_HEADS:]], axis=1).astype(BF16),
            "b_f": jnp.pad(b_f[l], (0, GATE_COLS - FOX_HEADS)).reshape(1, GATE_COLS),
            "lambda_vecs": lambda_vecs[l],
            "subln_g": subln_g[l].reshape(1, PAIR),
            "w_o": w_o[l].astype(BF16),
            "ln1_g": ln1_g[l].reshape(1, D_MODEL), "ln1_b": ln1_b[l].reshape(1, D_MODEL),
            "w_up": w_up[l].astype(BF16),
            "conv_w": conv_w[l], "conv_b": conv_b[l].reshape(1, D_FF),
            "w_down": w_down[l].astype(BF16),
            "ln2_g": ln2_g[l].reshape(1, D_MODEL), "ln2_b": ln2_b[l].reshape(1, D_MODEL),
        }
        mod = _ada(c_all, w_ada[l], b_ada[l]).reshape(c_all.shape[0], 6, D_MODEL)
        yp, st_p = _layer(yp, mod[:nb], None, w, lam_init, alpha)
        past = (cache_fox_k[l], cache_fox_v[l], cache_fox_logf[l], cache_diff_k[l], cache_diff_v[l],
                state_ffn_conv[l])
        ys, st_s = _layer(ys, mod[nb:], past, w, lam_init, alpha)
        p_states.append(st_p)
        s_states.append(st_s)
    p_out = [jnp.stack(a, axis=0) for a in zip(*p_states)]
    s_out = [jnp.stack(a, axis=0) for a in zip(*s_states)]
    return (yp, ys, *p_out, *s_out)
```

```python
import functools
import math

import jax
import jax.numpy as jnp
from jax import lax
from jax.experimental import pallas as pl
from jax.experimental.pallas import tpu as pltpu

F32 = jnp.float32
BF16 = jnp.bfloat16

D_MODEL = 1024
CHUNK = 64
FOX_HEADS = 8
DIFF_HEADS = 4
HEAD_DIM = 64
WIDTH = 512
PAIR = 2 * HEAD_DIM
N_PAIRS = WIDTH // PAIR
D_FF = 2816
CONV_WIDTH = 3
ROPE_THETA = 10000.0
LN_EPS = 1e-5
RMS_EPS = 1e-6
NEG_BIG = -1e30
Q_SCALE = HEAD_DIM ** -0.5
LOG2E = math.log2(math.e)

LANES = 128
MXU_DIM = 256
VMEM_LIMIT_BYTES = 56 * 1024 * 1024

GATE_COLS = LANES
OFF_FQ, OFF_FK, OFF_FV = 0, WIDTH, 2 * WIDTH
OFF_FF = 3 * WIDTH
OFF_DQ = OFF_FF + GATE_COLS
OFF_DK, OFF_DV = OFF_DQ + WIDTH, OFF_DQ + 2 * WIDTH
IN_COLS_PADDED = OFF_DV + WIDTH

KV_TILE = 256
TOKEN_TILE = 512
FF_TILE = 256
SUM_ROWS = 16
N_CHAINS = FOX_HEADS + 2 * DIFF_HEADS


def _params(*sem):
    return pltpu.CompilerParams(dimension_semantics=sem, vmem_limit_bytes=VMEM_LIMIT_BYTES)


def _resident(shape):
    nd = len(shape)
    return pl.BlockSpec(shape, lambda *_: (0,) * nd, pipeline_mode=pl.Buffered(1))


def _ada_kernel(c_ref, w_ref, b_ref, o_ref):
    c = c_ref[...]
    s = c * (1.0 / (1.0 + jnp.exp(-c)))
    o_ref[...] = jnp.dot(s.astype(BF16), w_ref[...].astype(BF16),
                         preferred_element_type=F32) + b_ref[...]


def _ada(c, w_ada, b_ada):
    n, d = c.shape
    cols = w_ada.shape[1]
    tn = cols // 4
    return pl.pallas_call(
        _ada_kernel,
        grid=(cols // tn,),
        in_specs=[pl.BlockSpec((n, d), lambda j: (0, 0)),
                  pl.BlockSpec((d, tn), lambda j: (0, j)),
                  pl.BlockSpec((1, tn), lambda j: (0, j))],
        out_specs=pl.BlockSpec((n, tn), lambda j: (0, j)),
        out_shape=jax.ShapeDtypeStruct((n, cols), F32),
        compiler_params=_params("arbitrary"),
        name="ada",
    )(c, w_ada, b_ada.reshape(1, cols))


def _inproj_kernel(x_ref, mod_ref, w_ref, bf_ref, cos_ref, sin_ref,
                   fk_o, fv_o, lf_o, dk_o, dv_o,
                   fqb_o, fkb_o, fvb_o, dqb_o, dkb_o, dvb_o, *, transposed):
    bb, tt, d = x_ref.shape
    m = mod_ref[...]
    u = x_ref[...] * (1.0 + m[:, 1:2, :]) + m[:, 0:1, :]
    u = u.reshape(bb * tt, d).astype(BF16)
    q_scale = Q_SCALE * LOG2E

    def proj(off, width):
        return jnp.dot(u, w_ref[:, off:off + width], preferred_element_type=F32)

    def put(o_ref, v):
        o_ref[...] = v.reshape(o_ref.shape).astype(o_ref.dtype)

    def put_qv(o_ref, v):
        if transposed:
            for c in range(tt // KV_TILE):
                o_ref[0, c] = v[c * KV_TILE:(c + 1) * KV_TILE, :].T.astype(o_ref.dtype)
        else:
            put(o_ref, v)

    fq = proj(OFF_FQ, WIDTH)
    put_qv(fqb_o, fq * q_scale)
    fk = proj(OFF_FK, WIDTH)
    put(fk_o, fk)
    put(fkb_o, fk)
    fv = proj(OFF_FV, WIDTH)
    put(fv_o, fv)
    put_qv(fvb_o, fv)

    zf = proj(OFF_FF, GATE_COLS) + bf_ref[...]
    lf = jnp.minimum(zf, 0.0) - jnp.log1p(jnp.exp(-jnp.abs(zf)))
    put(lf_o, lf[:, :FOX_HEADS])

    cos = cos_ref[...]
    sin = sin_ref[...]
    lane = lax.broadcasted_iota(jnp.int32, (1, WIDTH), 1)
    first_half = (lane % HEAD_DIM) < (HEAD_DIM // 2)

    def rope(v):
        partner = jnp.where(first_half, pltpu.roll(v, WIDTH - HEAD_DIM // 2, 1),
                            pltpu.roll(v, HEAD_DIM // 2, 1))
        return v * cos + partner * sin

    dq = rope(proj(OFF_DQ, WIDTH))
    put_qv(dqb_o, dq * q_scale)
    dk = rope(proj(OFF_DK, WIDTH))
    put(dk_o, dk)
    put(dkb_o, dk)
    dv = proj(OFF_DV, WIDTH)
    for hd in range(DIFF_HEADS):
        dv_o[:, :, hd, :] = dv[:, hd * PAIR:(hd + 1) * PAIR].reshape(bb, tt, PAIR)
    put_qv(dvb_o, dv)


def _inproj(x, mod, w_pad, bf_pad, cos_t, sin_t, bb, tt, transposed):
    B, T, d = x.shape
    rows = bb * tt
    grid = (B // bb, T // tt)
    assert not transposed or (bb == 1 and tt % KV_TILE == 0)
    tok = lambda w: pl.BlockSpec((bb, tt, w), lambda b, t: (b, t, 0))
    tab = pl.BlockSpec((rows, WIDTH), (lambda b, t: (t, 0)) if bb == 1 else (lambda b, t: (0, 0)))
    sds = lambda w, dt: jax.ShapeDtypeStruct((B, T, w), dt)
    if transposed:
        qv = pl.BlockSpec((1, tt // KV_TILE, WIDTH, KV_TILE), lambda b, t: (b, t, 0, 0))
        qv_sds = jax.ShapeDtypeStruct((B, T // KV_TILE, WIDTH, KV_TILE), BF16)
    else:
        qv, qv_sds = tok(WIDTH), sds(WIDTH, BF16)
    return pl.pallas_call(
        functools.partial(_inproj_kernel, transposed=transposed),
        grid=grid,
        in_specs=[tok(d),
                  pl.BlockSpec((bb, 6, d), lambda b, t: (b, 0, 0)),
                  _resident(w_pad.shape),
                  _resident(bf_pad.shape),
                  tab, tab],
        out_specs=[tok(WIDTH), tok(WIDTH), tok(FOX_HEADS), tok(WIDTH),
                   pl.BlockSpec((bb, tt, DIFF_HEADS, PAIR), lambda b, t: (b, t, 0, 0)),
                   qv, tok(WIDTH), qv, qv, tok(WIDTH), qv],
        out_shape=[sds(WIDTH, F32), sds(WIDTH, F32), sds(FOX_HEADS, F32), sds(WIDTH, F32),
                   jax.ShapeDtypeStruct((B, T, DIFF_HEADS, PAIR), F32),
                   qv_sds, sds(WIDTH, BF16), qv_sds, qv_sds, sds(WIDTH, BF16), qv_sds],
        compiler_params=_params("arbitrary", "arbitrary"),
        name="inproj",
    )(x, mod, w_pad, bf_pad, cos_t, sin_t)


def _split3(x):
    hi = x.astype(BF16).astype(F32)
    r1 = x - hi
    mid = r1.astype(BF16).astype(F32)
    lo = (r1 - mid).astype(BF16).astype(F32)
    return hi, mid, lo


def _cum_kernel(x_ref, o_ref, e_ref, *, keys_on_lanes):
    S = x_ref.shape[2]
    nblk = S // MXU_DIM
    r = lax.broadcasted_iota(jnp.int32, (MXU_DIM, MXU_DIM), 0)
    c = lax.broadcasted_iota(jnp.int32, (MXU_DIM, MXU_DIM), 1)
    tri = jnp.where(r <= c, 1.0, 0.0).astype(BF16)
    pieces = []
    for k in range(nblk):
        pieces += list(_split3(x_ref[0, :, k * MXU_DIM:(k + 1) * MXU_DIM]))
    local = jnp.dot(jnp.concatenate(pieces, axis=0).astype(BF16), tri, preferred_element_type=F32)
    carry = jnp.zeros((FOX_HEADS, 1), F32)
    for k in range(nblk):
        sl = slice(k * MXU_DIM, (k + 1) * MXU_DIM)
        r0 = 3 * FOX_HEADS * k
        part = (local[r0:r0 + FOX_HEADS] + local[r0 + FOX_HEADS:r0 + 2 * FOX_HEADS]
                + local[r0 + 2 * FOX_HEADS:r0 + 3 * FOX_HEADS]) + carry
        carry = part[:, MXU_DIM - 1:MXU_DIM]
        scaled = part * LOG2E
        o_ref[0, :, sl] = scaled
        pad = jnp.zeros((LANES - 3 * FOX_HEADS, MXU_DIM), F32)
        bias = jnp.concatenate(_split3(-scaled) + (pad,), axis=0)
        if keys_on_lanes:
            e_ref[0, :, sl] = bias.astype(BF16)
        else:
            e_ref[0, sl, :] = bias.T.astype(BF16)


def _cum(lf_t, bias_layout):
    B, H, S = lf_t.shape
    spec = pl.BlockSpec((1, H, S), lambda b: (b, 0, 0))
    e_shape = (B, LANES, S) if bias_layout == "lanes" else (B, S, LANES)
    return pl.pallas_call(
        functools.partial(_cum_kernel, keys_on_lanes=bias_layout == "lanes"),
        grid=(B,), in_specs=[spec],
        out_specs=[spec, pl.BlockSpec((1,) + e_shape[1:], lambda b: (b, 0, 0))],
        out_shape=[jax.ShapeDtypeStruct((B, H, S), F32), jax.ShapeDtypeStruct(e_shape, BF16)],
        compiler_params=_params("arbitrary"),
        name="cum",
    )(lf_t)


def _lambda_value(lv_ref, lam_init):
    lv = lv_ref[...]
    a = jnp.sum(lv[0:1] * lv[1:2], axis=1, keepdims=True)
    b = jnp.sum(lv[2:3] * lv[3:4], axis=1, keepdims=True)
    return jnp.exp(a) - jnp.exp(b) + lam_init


def _attn_prompt_kernel(fq_ref, fqn_ref, fk_ref, e_ref, fv_ref, dq_ref, dqn_ref, dk_ref, dv_ref, cq_ref,
                        sel_ref, lv_ref, g_ref, o_ref, s_ref, mb_ref, w_ref, acc_ref, *, lam_init):
    tq = tk = KV_TILE
    i = pl.program_id(1)
    srow = lax.broadcasted_iota(jnp.int32, (PAIR, tq), 0)
    first = srow < HEAD_DIM
    krow = lax.broadcasted_iota(jnp.int32, (tk, tq), 0)
    qcol = lax.broadcasted_iota(jnp.int32, (tk, tq), 1)
    causal = krow <= qcol
    chunk_causal = (krow // CHUNK) <= (qcol // CHUNK)
    zero = jnp.zeros((PAIR, tq), BF16)
    ones = jnp.ones((SUM_ROWS, tk), BF16)

    def rows_of(j):
        return pl.ds(pl.multiple_of(j * tk, tk), tk)

    def pair_lanes(p):
        return slice(p * PAIR, (p + 1) * PAIR)

    def weights(fq_src, dq_src):
        out = []
        for pair in range(N_PAIRS):
            qt = fq_src[0, 0, pair_lanes(pair), :]
            out += [jnp.concatenate([jnp.where(first, qt, zero), sel_ref[2 * pair]], axis=0),
                    jnp.concatenate([jnp.where(first, zero, qt), sel_ref[2 * pair + 1]], axis=0)]
        for hd in range(DIFF_HEADS):
            qt = dq_src[0, 0, pair_lanes(hd), :]
            out += [jnp.where(first, qt, zero), jnp.where(first, zero, qt)]
        return out

    chains = []
    for pair in range(N_PAIRS):
        lanes = pair_lanes(pair)
        keys = lambda j, lanes=lanes: jnp.concatenate([fk_ref[0, rows_of(j), lanes], e_ref[0, rows_of(j), :]], axis=1)
        for sub in range(2):
            head = 2 * pair + sub
            vt = lambda j, r0=head * HEAD_DIM: fv_ref[0, j, r0:r0 + HEAD_DIM, :]
            chains.append((2 * PAIR, cq_ref[0, head:head + 1, :], keys, vt, causal, HEAD_DIM))
    for hd in range(DIFF_HEADS):
        lanes = pair_lanes(hd)
        keys = lambda j, lanes=lanes: dk_ref[0, rows_of(j), lanes]
        vt = lambda j, lanes=lanes: dv_ref[0, j, lanes, :]
        chains += [(PAIR, None, keys, vt, chunk_causal, PAIR)] * 2
    n = len(chains)

    def score(c, j, w):
        return jnp.dot(chains[c][2](j), w, preferred_element_type=F32)

    def keep(c, s):
        s_ref[c] = s
        return jnp.max(s, axis=0, keepdims=True)

    def consume(c, j, mb, masked, m):
        _, cq, _, vt, mask, rows = chains[c]
        used = rows + SUM_ROWS
        s = s_ref[c]
        if masked:
            s = jnp.where(mask, s, NEG_BIG)
            mb = jnp.max(s, axis=0, keepdims=True)
        m_new = jnp.maximum(m, mb if cq is None else mb + cq)
        p = jnp.exp2(s - (m_new if cq is None else m_new - cq))
        alpha = jnp.exp2(m - m_new)
        acc_ref[c, :used] = alpha * acc_ref[c, :used] + jnp.dot(
            jnp.concatenate([vt(j), ones], axis=0), p.astype(BF16), preferred_element_type=F32)
        return m_new

    @pl.when(i == 0)
    def _():
        for c, w in enumerate(weights(fq_ref, dq_ref)):
            w_ref[c, :chains[c][0]] = w
            mb_ref[c] = keep(c, score(c, 0, w))

    def body(j, carry):
        ms, mbs = list(carry[0]), list(carry[1])
        for c in range(n):
            s_next = score(c, j + 1, w_ref[c, :chains[c][0]])
            ms[c] = consume(c, j, mbs[c], False, ms[c])
            mbs[c] = keep(c, s_next)
        return tuple(ms), tuple(mbs)

    acc_ref[...] = jnp.zeros_like(acc_ref)
    ms, _ = lax.fori_loop(
        0, i, body,
        (tuple(jnp.full((1, tq), NEG_BIG, F32) for _ in range(n)), tuple(mb_ref[c] for c in range(n))))
    outs = []
    for c, w_next in enumerate(weights(fqn_ref, dqn_ref)):
        rows = chains[c][5]
        s_next = score(c, 0, w_next)
        consume(c, i, None, True, ms[c])
        w_ref[c, :chains[c][0]] = w_next
        mb_ref[c] = keep(c, s_next)
        outs.append(acc_ref[c, :rows] / acc_ref[c, rows:rows + 1])

    for pair in range(N_PAIRS):
        oa, ob = outs[2 * pair], outs[2 * pair + 1]
        o_ref[0, :, pair_lanes(pair)] = jnp.concatenate([oa, ob], axis=0).T.astype(o_ref.dtype)

    lam = _lambda_value(lv_ref, lam_init)
    for hd in range(DIFF_HEADS):
        o1, o2 = outs[2 * N_PAIRS + 2 * hd], outs[2 * N_PAIRS + 2 * hd + 1]
        d = o1 - lam * o2
        ms = jnp.mean(d * d, axis=0, keepdims=True)
        dn = (d * lax.rsqrt(ms + RMS_EPS)).T * g_ref[...] * (1.0 - lam_init)
        o_ref[0, :, WIDTH + hd * PAIR:WIDTH + (hd + 1) * PAIR] = dn.astype(o_ref.dtype)


def _attn_prompt(fq_t, fk, e, fv_t, dq_t, dk, dv_t, cq_t, lambda_vecs, subln_g, lam_init):
    B, T, _ = fk.shape
    tq = KV_TILE
    nt = T // tq
    qspec = pl.BlockSpec((1, 1, WIDTH, tq), lambda b, i: (b, i, 0, 0))
    qnext = pl.BlockSpec((1, 1, WIDTH, tq), lambda b, i: (b, jnp.minimum(i + 1, nt - 1), 0, 0))
    vspec = pl.BlockSpec((1, nt, WIDTH, tq), lambda b, i: (b, 0, 0, 0))
    kspec = pl.BlockSpec((1, T, WIDTH), lambda b, i: (b, 0, 0))
    slot = jnp.arange(PAIR)[None, :, None]
    sel = (slot % FOX_HEADS == jnp.arange(FOX_HEADS)[:, None, None]) & (slot < 3 * FOX_HEADS)
    sel = jnp.broadcast_to(sel, (FOX_HEADS, PAIR, tq)).astype(BF16)
    return pl.pallas_call(
        functools.partial(_attn_prompt_kernel, lam_init=lam_init),
        grid=(B, nt),
        in_specs=[qspec, qnext, kspec, pl.BlockSpec((1, T, LANES), lambda b, i: (b, 0, 0)), vspec,
                  qspec, qnext, kspec, vspec,
                  pl.BlockSpec((1, FOX_HEADS, tq), lambda b, i: (b, 0, i)),
                  _resident(sel.shape),
                  pl.BlockSpec(lambda_vecs.shape, lambda b, i: (0, 0)),
                  pl.BlockSpec(subln_g.shape, lambda b, i: (0, 0))],
        out_specs=pl.BlockSpec((1, tq, 2 * WIDTH), lambda b, i: (b, i, 0)),
        out_shape=jax.ShapeDtypeStruct((B, T, 2 * WIDTH), BF16),
        scratch_shapes=[pltpu.VMEM((N_CHAINS, tq, tq), F32),
                        pltpu.VMEM((N_CHAINS, 1, tq), F32),
                        pltpu.VMEM((N_CHAINS, 2 * PAIR, tq), BF16),
                        pltpu.VMEM((N_CHAINS, PAIR + SUM_ROWS, tq), F32)],
        compiler_params=_params("arbitrary", "arbitrary"),
        name="attn_prompt",
    )(fq_t, fq_t, fk, e, fv_t, dq_t, dq_t, dk, dv_t, cq_t, sel, lambda_vecs, subln_g)


def _attn_sample_kernel(fq_ref, fk_ref, fv_ref, dq_ref, dk_ref, dv_ref,
                        pfk_ref, pfv_ref, pdk_ref, pdv_ref, et_ref, cq_ref,
                        lv_ref, g_ref, o_ref, s_ref, *, lam_init):
    T = fq_ref.shape[1]
    P = pfk_ref.shape[2]
    tk = KV_TILE
    n_past = P // tk
    nrow = FOX_HEADS * T
    prow = 2 * T
    f_idx, d_idx = 0, 1
    nt_dims = (((1,), (1,)), ((), ()))

    row = lambda shape: lax.broadcasted_iota(jnp.int32, shape, 0)
    col = lambda shape: lax.broadcasted_iota(jnp.int32, shape, 1)

    own_head = row((nrow, WIDTH)) // T == col((nrow, WIDTH)) // HEAD_DIM
    select_t = jnp.where((col((nrow, LANES)) % FOX_HEADS == row((nrow, LANES)) // T)
                         & (col((nrow, LANES)) < 3 * FOX_HEADS), 1.0, 0.0).astype(BF16)

    def block_diag(q_ref):
        q = q_ref[0]
        return jnp.where(own_head, jnp.concatenate([q] * FOX_HEADS, axis=0), jnp.zeros((nrow, WIDTH), BF16))

    def pad_rows(x):
        return jnp.concatenate([x, jnp.zeros((LANES - T, x.shape[1]), x.dtype)], axis=0)

    def lane_halves(x, op):
        return op(x[:, :LANES], x[:, LANES:])

    qf = block_diag(fq_ref)
    qd = block_diag(dq_ref)
    lhs_f = jnp.concatenate([qf, select_t], axis=1)

    mf = md = jnp.full((nrow, LANES), NEG_BIG, F32)
    for j in range(n_past):
        cols = slice(j * tk, (j + 1) * tk)
        kf = jnp.concatenate([pfk_ref[0, :, cols].astype(BF16), et_ref[0, :, cols]], axis=0)
        sf = jnp.dot(lhs_f, kf, preferred_element_type=F32)
        sd = jnp.dot(qd, pdk_ref[0, :, cols].astype(BF16), preferred_element_type=F32)
        s_ref[f_idx, j] = sf
        s_ref[d_idx, j] = sd
        mf = jnp.maximum(mf, lane_halves(sf, jnp.maximum))
        md = jnp.maximum(md, lane_halves(sd, jnp.maximum))

    key = col((nrow, LANES))
    qpos = row((nrow, LANES)) % T
    valid = key < T
    causal = valid & (key <= qpos)
    chunk_causal = valid & ((key + P) // CHUNK <= (qpos + P) // CHUNK)
    sf_new = (lax.dot_general(qf, pad_rows(fk_ref[0]), nt_dims, preferred_element_type=F32)
              + jnp.dot(select_t, et_ref[0, :, P:P + LANES], preferred_element_type=F32))
    sf_new = jnp.where(causal, sf_new, NEG_BIG)
    sd_new = jnp.where(chunk_causal,
                       lax.dot_general(qd, pad_rows(dk_ref[0]), nt_dims, preferred_element_type=F32), NEG_BIG)
    cq = cq_ref[0]
    m_f = jnp.max(jnp.maximum(mf, sf_new), axis=1, keepdims=True) + cq
    shift_f = jnp.broadcast_to(m_f - cq, (nrow, LANES))
    shift_d = jnp.broadcast_to(jnp.max(jnp.maximum(md, sd_new), axis=1, keepdims=True), (nrow, LANES))

    def probs(s, shift):
        return jnp.exp2(s - jnp.concatenate([shift] * (s.shape[1] // LANES), axis=1))

    lf = ld = jnp.zeros((nrow, LANES), F32)
    acc_f = [jnp.zeros((prow, PAIR), F32) for _ in range(N_PAIRS)]
    acc_d = [jnp.zeros((prow, PAIR), F32) for _ in range(DIFF_HEADS)]
    for j in range(n_past):
        cols = slice(j * tk, (j + 1) * tk)
        pf = probs(s_ref[f_idx, j], shift_f)
        pd = probs(s_ref[d_idx, j], shift_d)
        lf = lf + lane_halves(pf, jnp.add)
        ld = ld + lane_halves(pd, jnp.add)
        pf, pd = pf.astype(BF16), pd.astype(BF16)
        for u in range(N_PAIRS):
            rows = slice(u * prow, (u + 1) * prow)
            acc_f[u] = acc_f[u] + lax.dot_general(
                pf[rows], pfv_ref[0, u * PAIR:(u + 1) * PAIR, cols].astype(BF16), nt_dims,
                preferred_element_type=F32)
            v = pdv_ref[0, pl.ds(j * tk * DIFF_HEADS + u, tk, stride=DIFF_HEADS), :].astype(BF16)
            acc_d[u] = acc_d[u] + jnp.dot(pd[rows], v, preferred_element_type=F32)
    pf = jnp.exp2(sf_new - shift_f)
    pd = jnp.exp2(sd_new - shift_d)
    lf = jnp.sum(lf + pf, axis=1, keepdims=True)
    ld = jnp.sum(ld + pd, axis=1, keepdims=True)
    pf, pd = pf.astype(BF16), pd.astype(BF16)
    fv_new, dv_new = pad_rows(fv_ref[0]), pad_rows(dv_ref[0])
    low = col((T, PAIR)) < HEAD_DIM
    lam = _lambda_value(lv_ref, lam_init)
    for u in range(N_PAIRS):
        rows = slice(u * prow, (u + 1) * prow)
        lanes = slice(u * PAIR, (u + 1) * PAIR)
        a = (acc_f[u] + jnp.dot(pf[rows], fv_new[:, lanes], preferred_element_type=F32)) / lf[rows]
        o_ref[0, :, lanes] = jnp.where(low, a[:T], a[T:]).astype(o_ref.dtype)
        a = (acc_d[u] + jnp.dot(pd[rows], dv_new[:, lanes], preferred_element_type=F32)) / ld[rows]
        d = a[:T] - lam * a[T:]
        ms = jnp.mean(d * d, axis=-1, keepdims=True)
        dn = d * lax.rsqrt(ms + RMS_EPS) * g_ref[...] * (1.0 - lam_init)
        o_ref[0, :, WIDTH + u * PAIR:WIDTH + (u + 1) * PAIR] = dn.astype(o_ref.dtype)


def _attn_sample(fq, fk, fv, dq, dk, dv, pfk_t, pfv_t, pdk_t, pdv_rows, e_t, cq, lambda_vecs, subln_g,
                 lam_init):
    B, T, _ = fq.shape
    P = pfk_t.shape[2]
    tk = KV_TILE
    nrow = FOX_HEADS * T
    assert N_PAIRS == DIFF_HEADS and T <= LANES and P % tk == 0 and e_t.shape[2] >= P + LANES
    new = pl.BlockSpec((1, T, WIDTH), lambda b: (b, 0, 0))
    cached = pl.BlockSpec((1, WIDTH, P), lambda b: (b, 0, 0))
    return pl.pallas_call(
        functools.partial(_attn_sample_kernel, lam_init=lam_init),
        grid=(B,),
        in_specs=[new] * 6 + [cached, cached, cached,
                              pl.BlockSpec((1, P * DIFF_HEADS, PAIR), lambda b: (b, 0, 0)),
                              pl.BlockSpec((1,) + e_t.shape[1:], lambda b: (b, 0, 0)),
                              pl.BlockSpec((1, nrow, 1), lambda b: (b, 0, 0)),
                              pl.BlockSpec(lambda_vecs.shape, lambda b: (0, 0)),
                              pl.BlockSpec(subln_g.shape, lambda b: (0, 0))],
        out_specs=pl.BlockSpec((1, T, 2 * WIDTH), lambda b: (b, 0, 0)),
        out_shape=jax.ShapeDtypeStruct((B, T, 2 * WIDTH), BF16),
        scratch_shapes=[pltpu.VMEM((2, P // tk, nrow, tk), F32)],
        compiler_params=_params("arbitrary"),
        name="attn_sample",
    )(fq, fk, fv, dq, dk, dv, pfk_t, pfv_t, pdk_t, pdv_rows, e_t, cq, lambda_vecs, subln_g)


def _post_norm(x, h, gate, g, b, alpha):
    y = alpha * x + gate * h
    mu = jnp.mean(y, axis=-1, keepdims=True)
    yc = y - mu
    var = jnp.mean(yc * yc, axis=-1, keepdims=True)
    return yc * lax.rsqrt(var + LN_EPS) * g + b


def _post_kernel(x_ref, o_ref, mod_ref, prev_ref, wo_ref, ln1g_ref, ln1b_ref,
                 wup_ref, cw_ref, cb_ref, wdn_ref, ln2g_ref, ln2b_ref,
                 y_ref, conv_ref, carry_ref, hid_ref, *, alpha):
    bb, tt, d = x_ref.shape
    rows = bb * tt
    keep = CONV_WIDTH - 1
    t = pl.program_id(1)

    @pl.when(t == 0)
    def _():
        carry_ref[...] = prev_ref[...]

    m = mod_ref[...]

    def rows3(v):
        return jnp.broadcast_to(v, (bb, tt, v.shape[-1])).reshape(rows, v.shape[-1])

    x = x_ref[...].reshape(rows, d)
    h = jnp.dot(o_ref[...].reshape(rows, d), wo_ref[...], preferred_element_type=F32)
    x1 = _post_norm(x, h, rows3(m[:, 2:3, :]), ln1g_ref[...], ln1b_ref[...], alpha)
    u2 = (x1 * (1.0 + rows3(m[:, 4:5, :])) + rows3(m[:, 3:4, :])).astype(BF16)

    tpos = lax.broadcasted_iota(jnp.int32, (bb, tt, 1), 1).reshape(rows, 1)
    for c in range(D_FF // FF_TILE):
        cols = slice(c * FF_TILE, (c + 1) * FF_TILE)
        a = jnp.dot(u2, wup_ref[:, cols], preferred_element_type=F32)
        g = jnp.dot(u2, wup_ref[:, D_FF + c * FF_TILE:D_FF + (c + 1) * FF_TILE],
                    preferred_element_type=F32)
        prev = carry_ref[:, :, cols]
        p2 = rows3(prev[:, 0:1, :])
        p1 = rows3(prev[:, 1:2, :])
        am1 = jnp.where(tpos == 0, p1, pltpu.roll(a, 1, 0))
        am2 = jnp.where(tpos == 0, p2, jnp.where(tpos == 1, p1, pltpu.roll(a, 2, 0)))
        cw = cw_ref[:, cols]
        conv = cb_ref[:, cols] + am2 * cw[0:1] + am1 * cw[1:2] + a * cw[2:3]
        hid = conv * (1.0 / (1.0 + jnp.exp(-conv))) * g
        hid_ref[:, cols] = hid.astype(BF16)
        last = a.reshape(bb, tt, FF_TILE)[:, tt - keep:, :]
        carry_ref[:, :, cols] = last
        conv_ref[:, :, cols] = last

    f_out = jnp.dot(hid_ref[...], wdn_ref[...], preferred_element_type=F32)
    y = _post_norm(x1, f_out, rows3(m[:, 5:6, :]), ln2g_ref[...], ln2b_ref[...], alpha)
    y_ref[...] = y.reshape(bb, tt, d)


def _post(x, o, mod, conv_prev, wo, ln1g, ln1b, wup, cw, cb, wdn, ln2g, ln2b, bb, tt, alpha):
    B, T, d = x.shape
    keep = CONV_WIDTH - 1
    tok = pl.BlockSpec((bb, tt, d), lambda b, t: (b, t, 0))
    per_b = lambda r, w: pl.BlockSpec((bb, r, w), lambda b, t: (b, 0, 0))
    return pl.pallas_call(
        functools.partial(_post_kernel, alpha=alpha),
        grid=(B // bb, T // tt),
        in_specs=[tok, tok, per_b(6, d), per_b(keep, D_FF),
                  _resident(wo.shape), _resident(ln1g.shape), _resident(ln1b.shape),
                  _resident(wup.shape), _resident(cw.shape), _resident(cb.shape),
                  _resident(wdn.shape), _resident(ln2g.shape), _resident(ln2b.shape)],
        out_specs=[tok, per_b(keep, D_FF)],
        out_shape=[jax.ShapeDtypeStruct((B, T, d), F32),
                   jax.ShapeDtypeStruct((B, keep, D_FF), F32)],
        scratch_shapes=[pltpu.VMEM((bb, keep, D_FF), F32),
                        pltpu.VMEM((bb * tt, D_FF), BF16)],
        compiler_params=_params("arbitrary", "arbitrary"),
        name="post",
    )(x, o, mod, conv_prev, wo, ln1g, ln1b, wup, cw, cb, wdn, ln2g, ln2b)


def _rope_tables(pos0, T):
    inv = ROPE_THETA ** (-jnp.arange(0, HEAD_DIM, 2, dtype=F32) / HEAD_DIM)
    ang = (pos0 + jnp.arange(T)).astype(F32)[:, None] * inv[None, :]
    cos, sin = jnp.cos(ang), jnp.sin(ang)
    reps = WIDTH // HEAD_DIM
    return (jnp.tile(jnp.concatenate([cos, cos], axis=1), (1, reps)),
            jnp.tile(jnp.concatenate([-sin, sin], axis=1), (1, reps)))


def _round_up(n, k):
    return -(-n // k) * k


def _layer(x, mod, past, w, lam_init, alpha):
    B, T, d = x.shape
    P = 0 if past is None else past[0].shape[1]
    if T % TOKEN_TILE == 0:
        bb, tt = 1, TOKEN_TILE
    else:
        tt = T
        bb = math.gcd(B, max(1, TOKEN_TILE // T))
    cos_t, sin_t = _rope_tables(P, T)
    if bb > 1:
        cos_t, sin_t = jnp.tile(cos_t, (bb, 1)), jnp.tile(sin_t, (bb, 1))
    (fk, fv, lf, dk, dv, fqb, fkb, fvb, dqb, dkb, dvb) = _inproj(
        x, mod, w["w_in"], w["b_f"], cos_t, sin_t, bb, tt, transposed=past is None)

    lf_t = jnp.swapaxes(lf, 1, 2)
    if past is None:
        cq_t, e = _cum(lf_t, "rows")
        o = _attn_prompt(fqb, fkb, e, fvb, dqb, dkb, dvb, cq_t,
                         w["lambda_vecs"], w["subln_g"], lam_init)
        conv_prev = jnp.zeros((B, CONV_WIDTH - 1, D_FF), F32)
    else:
        pfk, pfv, plf, pdk, pdv, conv_prev = past
        S = _round_up(P + LANES, MXU_DIM)
        lf_all = jnp.concatenate(
            [jnp.swapaxes(plf, 1, 2), lf_t, jnp.zeros((B, FOX_HEADS, S - P - T), F32)], axis=2)
        cq_t, e_t = _cum(lf_all, "lanes")
        cq = cq_t[:, :, P:P + T].reshape(B, FOX_HEADS * T, 1)
        keys_on_lanes = lambda c: jnp.transpose(c, (0, 2, 3, 1)).reshape(B, WIDTH, P)
        o = _attn_sample(fqb, fkb, fvb, dqb, dkb, dvb,
                         keys_on_lanes(pfk), keys_on_lanes(pfv), keys_on_lanes(pdk),
                         pdv.reshape(B, P * DIFF_HEADS, PAIR),
                         e_t, cq, w["lambda_vecs"], w["subln_g"], lam_init)

    y, conv = _post(x, o, mod, conv_prev, w["w_o"], w["ln1_g"], w["ln1_b"], w["w_up"],
                    w["conv_w"], w["conv_b"], w["w_down"], w["ln2_g"], w["ln2_b"], bb, tt, alpha)
    state = (fk.reshape(B, T, FOX_HEADS, HEAD_DIM), fv.reshape(B, T, FOX_HEADS, HEAD_DIM), lf,
             dk.reshape(B, T, 2 * DIFF_HEADS, HEAD_DIM), dv, conv)
    return y, state


def kernel(x_prompt, x_sample, c_prompt, c_sample, cache_fox_k, cache_fox_v, cache_fox_logf, cache_diff_k, cache_diff_v, state_ffn_conv, w_ada, b_ada, w_in, b_f, lambda_vecs, subln_g, w_o, ln1_g, ln1_b, w_up, conv_w, conv_b, w_down, ln2_g, ln2_b):
    depth = w_ada.shape[0]
    alpha = (2 * depth) ** 0.25
    nb = c_prompt.shape[0]
    yp, ys = x_prompt, x_sample
    c_all = jnp.concatenate([c_prompt, c_sample], axis=0)
    p_states, s_states = [], []
    for l in range(depth):
        lam_init = 0.8 - 0.6 * math.exp(-0.3 * l)
        w_in_l = w_in[l]
        gate_w = jnp.pad(w_in_l[:, OFF_FF:OFF_FF + FOX_HEADS], ((0, 0), (0, GATE_COLS - FOX_HEADS)))
        w = {
            "w_in": jnp.concatenate(
                [w_in_l[:, :OFF_FF], gate_w, w_in_l[:, OFF_FF + FOX_HEADS:]], axis=1).astype(BF16),
            "b_f": jnp.pad(b_f[l], (0, GATE_COLS - FOX_HEADS)).reshape(1, GATE_COLS),
            "lambda_vecs": lambda_vecs[l],
            "subln_g": subln_g[l].reshape(1, PAIR),
            "w_o": w_o[l].astype(BF16),
            "ln1_g": ln1_g[l].reshape(1, D_MODEL), "ln1_b": ln1_b[l].reshape(1, D_MODEL),
            "w_up": w_up[l].astype(BF16),
            "conv_w": conv_w[l], "conv_b": conv_b[l].reshape(1, D_FF),
            "w_down": w_down[l].astype(BF16),
            "ln2_g": ln2_g[l].reshape(1, D_MODEL), "ln2_b": ln2_b[l].reshape(1, D_MODEL),
        }
        mod = _ada(c_all, w_ada[l], b_ada[l]).reshape(c_all.shape[0], 6, D_MODEL)
        yp, st_p = _layer(yp, mod[:nb], None, w, lam_init, alpha)
        past = (cache_fox_k[l], cache_fox_v[l], cache_fox_logf[l], cache_diff_k[l], cache_diff_v[l],
                state_ffn_conv[l])
        ys, st_s = _layer(ys, mod[nb:], past, w, lam_init, alpha)
        p_states.append(st_p)
        s_states.append(st_s)
    p_out = [jnp.stack(a, axis=0) for a in zip(*p_states)]
    s_out = [jnp.stack(a, axis=0) for a in zip(*s_states)]
    return (yp, ys, *p_out, *s_out)
```

```python
import functools
import math

import jax
import jax.numpy as jnp
from jax import lax
from jax.experimental import pallas as pl
from jax.experimental.pallas import tpu as pltpu

F32 = jnp.float32
BF16 = jnp.bfloat16

D_MODEL = 1024
CHUNK = 64
FOX_HEADS = 8
DIFF_HEADS = 4
HEAD_DIM = 64
WIDTH = 512
PAIR = 2 * HEAD_DIM
N_PAIRS = WIDTH // PAIR
D_FF = 2816
CONV_WIDTH = 3
ROPE_THETA = 10000.0
LN_EPS = 1e-5
RMS_EPS = 1e-6
NEG_BIG = -1e30
Q_SCALE = HEAD_DIM ** -0.5
LOG2E = math.log2(math.e)

LANES = 128
MXU_DIM = 256
VMEM_LIMIT_BYTES = 56 * 1024 * 1024

GATE_COLS = LANES
OFF_FQ, OFF_FK, OFF_FV = 0, WIDTH, 2 * WIDTH
OFF_FF = 3 * WIDTH
OFF_DQ = OFF_FF + GATE_COLS
OFF_DK, OFF_DV = OFF_DQ + WIDTH, OFF_DQ + 2 * WIDTH
IN_COLS_PADDED = OFF_DV + WIDTH

KV_TILE = 256
TOKEN_TILE = 512
FF_TILE = 256
SUM_ROWS = 16
N_CHAINS = FOX_HEADS + 2 * DIFF_HEADS
CUM_SEQS = 4


def _params(*sem):
    return pltpu.CompilerParams(dimension_semantics=sem, vmem_limit_bytes=VMEM_LIMIT_BYTES)


def _resident(shape):
    nd = len(shape)
    return pl.BlockSpec(shape, lambda *_: (0,) * nd, pipeline_mode=pl.Buffered(1))


def _ada_kernel(c_ref, w_ref, b_ref, o_ref):
    c = c_ref[...]
    s = c * (1.0 / (1.0 + jnp.exp(-c)))
    o_ref[...] = jnp.dot(s.astype(BF16), w_ref[...].astype(BF16),
                         preferred_element_type=F32) + b_ref[...]


def _ada(c, w_ada, b_ada):
    n, d = c.shape
    cols = w_ada.shape[1]
    tn = cols // 4
    return pl.pallas_call(
        _ada_kernel,
        grid=(cols // tn,),
        in_specs=[pl.BlockSpec((n, d), lambda j: (0, 0)),
                  pl.BlockSpec((d, tn), lambda j: (0, j)),
                  pl.BlockSpec((1, tn), lambda j: (0, j))],
        out_specs=pl.BlockSpec((n, tn), lambda j: (0, j)),
        out_shape=jax.ShapeDtypeStruct((n, cols), F32),
        compiler_params=_params("arbitrary"),
        name="ada",
    )(c, w_ada, b_ada.reshape(1, cols))


def _inproj_kernel(x_ref, mod_ref, w_ref, bf_ref, cos_ref, sin_ref,
                   fk_o, fv_o, lf_o, dk_o, dv_o,
                   fqb_o, fkb_o, fvb_o, dqb_o, dkb_o, dvb_o, *, transposed):
    bb, tt, d = x_ref.shape
    m = mod_ref[...]
    u = x_ref[...] * (1.0 + m[:, 1:2, :]) + m[:, 0:1, :]
    u = u.reshape(bb * tt, d).astype(BF16)
    q_scale = Q_SCALE * LOG2E

    def proj(off, width):
        return jnp.dot(u, w_ref[:, off:off + width], preferred_element_type=F32)

    def put(o_ref, v):
        o_ref[...] = v.reshape(o_ref.shape).astype(o_ref.dtype)

    def put_qv(o_ref, v):
        if transposed:
            for c in range(tt // KV_TILE):
                o_ref[0, c] = v[c * KV_TILE:(c + 1) * KV_TILE, :].T.astype(o_ref.dtype)
        else:
            put(o_ref, v)

    fq = proj(OFF_FQ, WIDTH)
    put_qv(fqb_o, fq * q_scale)
    fk = proj(OFF_FK, WIDTH)
    put(fk_o, fk)
    put(fkb_o, fk)
    fv = proj(OFF_FV, WIDTH)
    put(fv_o, fv)
    put_qv(fvb_o, fv)

    zf = proj(OFF_FF, GATE_COLS) + bf_ref[...]
    lf = jnp.minimum(zf, 0.0) - jnp.log1p(jnp.exp(-jnp.abs(zf)))
    lf_o[...] = lf.T[:FOX_HEADS, :]

    cos = cos_ref[...]
    sin = sin_ref[...]
    lane = lax.broadcasted_iota(jnp.int32, (1, WIDTH), 1)
    first_half = (lane % HEAD_DIM) < (HEAD_DIM // 2)

    def rope(v):
        partner = jnp.where(first_half, pltpu.roll(v, WIDTH - HEAD_DIM // 2, 1),
                            pltpu.roll(v, HEAD_DIM // 2, 1))
        return v * cos + partner * sin

    dq = rope(proj(OFF_DQ, WIDTH))
    put_qv(dqb_o, dq * q_scale)
    dk = rope(proj(OFF_DK, WIDTH))
    put(dk_o, dk)
    put(dkb_o, dk)
    dv = proj(OFF_DV, WIDTH)
    for hd in range(DIFF_HEADS):
        dv_o[:, :, hd, :] = dv[:, hd * PAIR:(hd + 1) * PAIR].reshape(bb, tt, PAIR)
    put_qv(dvb_o, dv)


def _inproj(x, mod, w_pad, bf_pad, cos_t, sin_t, bb, tt, transposed):
    B, T, d = x.shape
    rows = bb * tt
    grid = (B // bb, T // tt)
    assert not transposed or (bb == 1 and tt % KV_TILE == 0)
    tok = lambda w: pl.BlockSpec((bb, tt, w), lambda b, t: (b, t, 0))
    tab = pl.BlockSpec((rows, WIDTH), (lambda b, t: (t, 0)) if bb == 1 else (lambda b, t: (0, 0)))
    sds = lambda w, dt: jax.ShapeDtypeStruct((B, T, w), dt)
    if transposed:
        qv = pl.BlockSpec((1, tt // KV_TILE, WIDTH, KV_TILE), lambda b, t: (b, t, 0, 0))
        qv_sds = jax.ShapeDtypeStruct((B, T // KV_TILE, WIDTH, KV_TILE), BF16)
    else:
        qv, qv_sds = tok(WIDTH), sds(WIDTH, BF16)
    return pl.pallas_call(
        functools.partial(_inproj_kernel, transposed=transposed),
        grid=grid,
        in_specs=[tok(d),
                  pl.BlockSpec((bb, 6, d), lambda b, t: (b, 0, 0)),
                  _resident(w_pad.shape),
                  _resident(bf_pad.shape),
                  tab, tab],
        out_specs=[tok(WIDTH), tok(WIDTH),
                   pl.BlockSpec((FOX_HEADS, rows), lambda b, t: (0, b * (T // tt) + t)),
                   tok(WIDTH),
                   pl.BlockSpec((bb, tt, DIFF_HEADS, PAIR), lambda b, t: (b, t, 0, 0)),
                   qv, tok(WIDTH), qv, qv, tok(WIDTH), qv],
        out_shape=[sds(WIDTH, F32), sds(WIDTH, F32), jax.ShapeDtypeStruct((FOX_HEADS, B * T), F32),
                   sds(WIDTH, F32),
                   jax.ShapeDtypeStruct((B, T, DIFF_HEADS, PAIR), F32),
                   qv_sds, sds(WIDTH, BF16), qv_sds, qv_sds, sds(WIDTH, BF16), qv_sds],
        compiler_params=_params("arbitrary", "arbitrary"),
        name="inproj",
    )(x, mod, w_pad, bf_pad, cos_t, sin_t)


def _split3(x):
    hi = x.astype(BF16).astype(F32)
    r1 = x - hi
    mid = r1.astype(BF16).astype(F32)
    lo = (r1 - mid).astype(BF16).astype(F32)
    return hi, mid, lo


def _cum_kernel(x_ref, o_ref, e_ref, *, keys_on_lanes):
    bb, _, S = x_ref.shape
    nblk = S // MXU_DIM
    r = lax.broadcasted_iota(jnp.int32, (MXU_DIM, MXU_DIM), 0)
    c = lax.broadcasted_iota(jnp.int32, (MXU_DIM, MXU_DIM), 1)
    tri = jnp.where(r <= c, 1.0, 0.0).astype(BF16)
    pieces = []
    for b in range(bb):
        for k in range(nblk):
            pieces += list(_split3(x_ref[b, :, k * MXU_DIM:(k + 1) * MXU_DIM]))
    local = jnp.dot(jnp.concatenate(pieces, axis=0).astype(BF16), tri, preferred_element_type=F32)
    pad = jnp.zeros((LANES - 3 * FOX_HEADS, MXU_DIM), F32)
    for b in range(bb):
        carry = jnp.zeros((FOX_HEADS, 1), F32)
        for k in range(nblk):
            sl = slice(k * MXU_DIM, (k + 1) * MXU_DIM)
            r0 = 3 * FOX_HEADS * (b * nblk + k)
            block = (local[r0:r0 + FOX_HEADS] + local[r0 + FOX_HEADS:r0 + 2 * FOX_HEADS]
                     + local[r0 + 2 * FOX_HEADS:r0 + 3 * FOX_HEADS])
            part = block + carry
            carry = carry + block[:, MXU_DIM - 1:MXU_DIM]
            scaled = part * LOG2E
            o_ref[b, :, sl] = scaled
            bias = jnp.concatenate(_split3(-scaled) + (pad,), axis=0)
            if keys_on_lanes:
                e_ref[b, :, sl] = bias.astype(BF16)
            else:
                e_ref[b, sl, :] = bias.T.astype(BF16)


def _cum(lf_t, bias_layout):
    B, H, S = lf_t.shape
    bb = math.gcd(B, CUM_SEQS)
    spec = pl.BlockSpec((bb, H, S), lambda b: (b, 0, 0))
    e_shape = (B, LANES, S) if bias_layout == "lanes" else (B, S, LANES)
    return pl.pallas_call(
        functools.partial(_cum_kernel, keys_on_lanes=bias_layout == "lanes"),
        grid=(B // bb,), in_specs=[spec],
        out_specs=[spec, pl.BlockSpec((bb,) + e_shape[1:], lambda b: (b, 0, 0))],
        out_shape=[jax.ShapeDtypeStruct((B, H, S), F32), jax.ShapeDtypeStruct(e_shape, BF16)],
        compiler_params=_params("arbitrary"),
        name="cum",
    )(lf_t)


def _lambda_value(lv_ref, lam_init):
    lv = lv_ref[...]
    a = jnp.sum(lv[0:1] * lv[1:2], axis=1, keepdims=True)
    b = jnp.sum(lv[2:3] * lv[3:4], axis=1, keepdims=True)
    return jnp.exp(a) - jnp.exp(b) + lam_init


def _attn_prompt_kernel(fq_ref, fqn_ref, fk_ref, e_ref, fv_ref, dq_ref, dqn_ref, dk_ref, dv_ref, cq_ref,
                        sel_ref, lv_ref, g_ref, o_ref, s_ref, mb_ref, w_ref, acc_ref, *, lam_init):
    tq = tk = KV_TILE
    i = pl.program_id(1)
    srow = lax.broadcasted_iota(jnp.int32, (PAIR, tq), 0)
    first = srow < HEAD_DIM
    krow = lax.broadcasted_iota(jnp.int32, (tk, tq), 0)
    qcol = lax.broadcasted_iota(jnp.int32, (tk, tq), 1)
    causal = krow <= qcol
    chunk_causal = (krow // CHUNK) <= (qcol // CHUNK)
    zero = jnp.zeros((PAIR, tq), BF16)
    ones = jnp.ones((SUM_ROWS, tk), BF16)

    def rows_of(j):
        return pl.ds(pl.multiple_of(j * tk, tk), tk)

    def pair_lanes(p):
        return slice(p * PAIR, (p + 1) * PAIR)

    def weights(fq_src, dq_src):
        out = []
        for pair in range(N_PAIRS):
            qt = fq_src[0, 0, pair_lanes(pair), :]
            out += [jnp.concatenate([jnp.where(first, qt, zero), sel_ref[2 * pair]], axis=0),
                    jnp.concatenate([jnp.where(first, zero, qt), sel_ref[2 * pair + 1]], axis=0)]
        for hd in range(DIFF_HEADS):
            qt = dq_src[0, 0, pair_lanes(hd), :]
            out += [jnp.where(first, qt, zero), jnp.where(first, zero, qt)]
        return out

    chains = []
    for pair in range(N_PAIRS):
        lanes = pair_lanes(pair)
        keys = lambda j, lanes=lanes: jnp.concatenate([fk_ref[0, rows_of(j), lanes], e_ref[0, rows_of(j), :]], axis=1)
        for sub in range(2):
            head = 2 * pair + sub
            vt = lambda j, r0=head * HEAD_DIM: fv_ref[0, j, r0:r0 + HEAD_DIM, :]
            chains.append((2 * PAIR, cq_ref[0, head:head + 1, :], keys, vt, causal, HEAD_DIM))
    for hd in range(DIFF_HEADS):
        lanes = pair_lanes(hd)
        keys = lambda j, lanes=lanes: dk_ref[0, rows_of(j), lanes]
        vt = lambda j, lanes=lanes: dv_ref[0, j, lanes, :]
        chains += [(PAIR, None, keys, vt, chunk_causal, PAIR)] * 2
    n = len(chains)

    def score(c, j, w):
        return jnp.dot(chains[c][2](j), w, preferred_element_type=F32)

    def keep(c, s):
        s_ref[c] = s
        return jnp.max(s, axis=0, keepdims=True)

    def consume(c, j, mb, masked, m):
        _, cq, _, vt, mask, rows = chains[c]
        used = rows + SUM_ROWS
        s = s_ref[c]
        if masked:
            s = jnp.where(mask, s, NEG_BIG)
            mb = jnp.max(s, axis=0, keepdims=True)
        m_new = jnp.maximum(m, mb if cq is None else mb + cq)
        p = jnp.exp2(s - (m_new if cq is None else m_new - cq))
        alpha = jnp.exp2(m - m_new)
        acc_ref[c, :used] = alpha * acc_ref[c, :used] + jnp.dot(
            jnp.concatenate([vt(j), ones], axis=0), p.astype(BF16), preferred_element_type=F32)
        return m_new

    @pl.when(i == 0)
    def _():
        for c, w in enumerate(weights(fq_ref, dq_ref)):
            w_ref[c, :chains[c][0]] = w
            mb_ref[c] = keep(c, score(c, 0, w))

    def body(j, carry):
        ms, mbs = list(carry[0]), list(carry[1])
        for c in range(n):
            s_next = score(c, j + 1, w_ref[c, :chains[c][0]])
            ms[c] = consume(c, j, mbs[c], False, ms[c])
            mbs[c] = keep(c, s_next)
        return tuple(ms), tuple(mbs)

    acc_ref[...] = jnp.zeros_like(acc_ref)
    ms, _ = lax.fori_loop(
        0, i, body,
        (tuple(jnp.full((1, tq), NEG_BIG, F32) for _ in range(n)), tuple(mb_ref[c] for c in range(n))))
    outs = []
    for c, w_next in enumerate(weights(fqn_ref, dqn_ref)):
        rows = chains[c][5]
        s_next = score(c, 0, w_next)
        consume(c, i, None, True, ms[c])
        w_ref[c, :chains[c][0]] = w_next
        mb_ref[c] = keep(c, s_next)
        outs.append(acc_ref[c, :rows] / acc_ref[c, rows:rows + 1])

    for pair in range(N_PAIRS):
        oa, ob = outs[2 * pair], outs[2 * pair + 1]
        o_ref[0, :, pair_lanes(pair)] = jnp.concatenate([oa, ob], axis=0).T.astype(o_ref.dtype)

    lam = _lambda_value(lv_ref, lam_init)
    for hd in range(DIFF_HEADS):
        o1, o2 = outs[2 * N_PAIRS + 2 * hd], outs[2 * N_PAIRS + 2 * hd + 1]
        d = o1 - lam * o2
        ms = jnp.mean(d * d, axis=0, keepdims=True)
        dn = (d * lax.rsqrt(ms + RMS_EPS)).T * g_ref[...] * (1.0 - lam_init)
        o_ref[0, :, WIDTH + hd * PAIR:WIDTH + (hd + 1) * PAIR] = dn.astype(o_ref.dtype)


def _attn_prompt(fq_t, fk, e, fv_t, dq_t, dk, dv_t, cq_t, lambda_vecs, subln_g, lam_init):
    B, T, _ = fk.shape
    tq = KV_TILE
    nt = T // tq
    qspec = pl.BlockSpec((1, 1, WIDTH, tq), lambda b, i: (b, i, 0, 0))
    qnext = pl.BlockSpec((1, 1, WIDTH, tq), lambda b, i: (b, jnp.minimum(i + 1, nt - 1), 0, 0))
    vspec = pl.BlockSpec((1, nt, WIDTH, tq), lambda b, i: (b, 0, 0, 0))
    kspec = pl.BlockSpec((1, T, WIDTH), lambda b, i: (b, 0, 0))
    slot = jnp.arange(PAIR)[None, :, None]
    sel = (slot % FOX_HEADS == jnp.arange(FOX_HEADS)[:, None, None]) & (slot < 3 * FOX_HEADS)
    sel = jnp.broadcast_to(sel, (FOX_HEADS, PAIR, tq)).astype(BF16)
    return pl.pallas_call(
        functools.partial(_attn_prompt_kernel, lam_init=lam_init),
        grid=(B, nt),
        in_specs=[qspec, qnext, kspec, pl.BlockSpec((1, T, LANES), lambda b, i: (b, 0, 0)), vspec,
                  qspec, qnext, kspec, vspec,
                  pl.BlockSpec((1, FOX_HEADS, tq), lambda b, i: (b, 0, i)),
                  _resident(sel.shape),
                  pl.BlockSpec(lambda_vecs.shape, lambda b, i: (0, 0)),
                  pl.BlockSpec(subln_g.shape, lambda b, i: (0, 0))],
        out_specs=pl.BlockSpec((1, tq, 2 * WIDTH), lambda b, i: (b, i, 0)),
        out_shape=jax.ShapeDtypeStruct((B, T, 2 * WIDTH), BF16),
        scratch_shapes=[pltpu.VMEM((N_CHAINS, tq, tq), F32),
                        pltpu.VMEM((N_CHAINS, 1, tq), F32),
                        pltpu.VMEM((N_CHAINS, 2 * PAIR, tq), BF16),
                        pltpu.VMEM((N_CHAINS, PAIR + SUM_ROWS, tq), F32)],
        compiler_params=_params("arbitrary", "arbitrary"),
        name="attn_prompt",
    )(fq_t, fq_t, fk, e, fv_t, dq_t, dq_t, dk, dv_t, cq_t, sel, lambda_vecs, subln_g)


def _attn_sample_kernel(fq_ref, fk_ref, fv_ref, dq_ref, dk_ref, dv_ref,
                        pfk_ref, pfv_ref, pdk_ref, pdv_ref, et_ref, cq_ref,
                        lv_ref, g_ref, o_ref, s_ref, *, lam_init):
    T = fq_ref.shape[1]
    P = pfk_ref.shape[2]
    tk = KV_TILE
    n_past = P // tk
    nrow = FOX_HEADS * T
    prow = 2 * T
    f_idx, d_idx = 0, 1
    nt_dims = (((1,), (1,)), ((), ()))

    row = lambda shape: lax.broadcasted_iota(jnp.int32, shape, 0)
    col = lambda shape: lax.broadcasted_iota(jnp.int32, shape, 1)

    own_head = row((nrow, WIDTH)) // T == col((nrow, WIDTH)) // HEAD_DIM
    select_t = jnp.where((col((nrow, LANES)) % FOX_HEADS == row((nrow, LANES)) // T)
                         & (col((nrow, LANES)) < 3 * FOX_HEADS), 1.0, 0.0).astype(BF16)

    def block_diag(q_ref):
        q = q_ref[0]
        return jnp.where(own_head, jnp.concatenate([q] * FOX_HEADS, axis=0), jnp.zeros((nrow, WIDTH), BF16))

    def pad_rows(x):
        return jnp.concatenate([x, jnp.zeros((LANES - T, x.shape[1]), x.dtype)], axis=0)

    def lane_halves(x, op):
        return op(x[:, :LANES], x[:, LANES:])

    qf = block_diag(fq_ref)
    qd = block_diag(dq_ref)
    lhs_f = jnp.concatenate([qf, select_t], axis=1)

    mf = md = jnp.full((nrow, LANES), NEG_BIG, F32)
    for j in range(n_past):
        cols = slice(j * tk, (j + 1) * tk)
        kf = jnp.concatenate([pfk_ref[0, :, cols].astype(BF16), et_ref[0, :, cols]], axis=0)
        sf = jnp.dot(lhs_f, kf, preferred_element_type=F32)
        sd = jnp.dot(qd, pdk_ref[0, :, cols].astype(BF16), preferred_element_type=F32)
        s_ref[f_idx, j] = sf
        s_ref[d_idx, j] = sd
        mf = jnp.maximum(mf, lane_halves(sf, jnp.maximum))
        md = jnp.maximum(md, lane_halves(sd, jnp.maximum))

    key = col((nrow, LANES))
    qpos = row((nrow, LANES)) % T
    valid = key < T
    causal = valid & (key <= qpos)
    chunk_causal = valid & ((key + P) // CHUNK <= (qpos + P) // CHUNK)
    sf_new = (lax.dot_general(qf, pad_rows(fk_ref[0]), nt_dims, preferred_element_type=F32)
              + jnp.dot(select_t, et_ref[0, :, P:P + LANES], preferred_element_type=F32))
    sf_new = jnp.where(causal, sf_new, NEG_BIG)
    sd_new = jnp.where(chunk_causal,
                       lax.dot_general(qd, pad_rows(dk_ref[0]), nt_dims, preferred_element_type=F32), NEG_BIG)
    cq = cq_ref[0]
    m_f = jnp.max(jnp.maximum(mf, sf_new), axis=1, keepdims=True) + cq
    shift_f = jnp.broadcast_to(m_f - cq, (nrow, LANES))
    shift_d = jnp.broadcast_to(jnp.max(jnp.maximum(md, sd_new), axis=1, keepdims=True), (nrow, LANES))

    def probs(s, shift):
        return jnp.exp2(s - jnp.concatenate([shift] * (s.shape[1] // LANES), axis=1))

    lf = ld = jnp.zeros((nrow, LANES), F32)
    acc_f = [jnp.zeros((prow, PAIR), F32) for _ in range(N_PAIRS)]
    acc_d = [jnp.zeros((prow, PAIR), F32) for _ in range(DIFF_HEADS)]
    for j in range(n_past):
        cols = slice(j * tk, (j + 1) * tk)
        pf = probs(s_ref[f_idx, j], shift_f)
        pd = probs(s_ref[d_idx, j], shift_d)
        lf = lf + lane_halves(pf, jnp.add)
        ld = ld + lane_halves(pd, jnp.add)
        pf, pd = pf.astype(BF16), pd.astype(BF16)
        for u in range(N_PAIRS):
            rows = slice(u * prow, (u + 1) * prow)
            acc_f[u] = acc_f[u] + lax.dot_general(
                pf[rows], pfv_ref[0, u * PAIR:(u + 1) * PAIR, cols].astype(BF16), nt_dims,
                preferred_element_type=F32)
            v = pdv_ref[0, pl.ds(j * tk * DIFF_HEADS + u, tk, stride=DIFF_HEADS), :].astype(BF16)
            acc_d[u] = acc_d[u] + jnp.dot(pd[rows], v, preferred_element_type=F32)
    pf = jnp.exp2(sf_new - shift_f)
    pd = jnp.exp2(sd_new - shift_d)
    lf = jnp.sum(lf + pf, axis=1, keepdims=True)
    ld = jnp.sum(ld + pd, axis=1, keepdims=True)
    pf, pd = pf.astype(BF16), pd.astype(BF16)
    fv_new, dv_new = pad_rows(fv_ref[0]), pad_rows(dv_ref[0])
    low = col((T, PAIR)) < HEAD_DIM
    lam = _lambda_value(lv_ref, lam_init)
    for u in range(N_PAIRS):
        rows = slice(u * prow, (u + 1) * prow)
        lanes = slice(u * PAIR, (u + 1) * PAIR)
        a = (acc_f[u] + jnp.dot(pf[rows], fv_new[:, lanes], preferred_element_type=F32)) / lf[rows]
        o_ref[0, :, lanes] = jnp.where(low, a[:T], a[T:]).astype(o_ref.dtype)
        a = (acc_d[u] + jnp.dot(pd[rows], dv_new[:, lanes], preferred_element_type=F32)) / ld[rows]
        d = a[:T] - lam * a[T:]
        ms = jnp.mean(d * d, axis=-1, keepdims=True)
        dn = d * lax.rsqrt(ms + RMS_EPS) * g_ref[...] * (1.0 - lam_init)
        o_ref[0, :, WIDTH + u * PAIR:WIDTH + (u + 1) * PAIR] = dn.astype(o_ref.dtype)


def _attn_sample(fq, fk, fv, dq, dk, dv, pfk_t, pfv_t, pdk_t, pdv_rows, e_t, cq, lambda_vecs, subln_g,
                 lam_init):
    B, T, _ = fq.shape
    P = pfk_t.shape[2]
    tk = KV_TILE
    nrow = FOX_HEADS * T
    assert N_PAIRS == DIFF_HEADS and T <= LANES and P % tk == 0 and e_t.shape[2] >= P + LANES
    new = pl.BlockSpec((1, T, WIDTH), lambda b: (b, 0, 0))
    cached = pl.BlockSpec((1, WIDTH, P), lambda b: (b, 0, 0))
    return pl.pallas_call(
        functools.partial(_attn_sample_kernel, lam_init=lam_init),
        grid=(B,),
        in_specs=[new] * 6 + [cached, cached, cached,
                              pl.BlockSpec((1, P * DIFF_HEADS, PAIR), lambda b: (b, 0, 0)),
                              pl.BlockSpec((1,) + e_t.shape[1:], lambda b: (b, 0, 0)),
                              pl.BlockSpec((1, nrow, 1), lambda b: (b, 0, 0)),
                              pl.BlockSpec(lambda_vecs.shape, lambda b: (0, 0)),
                              pl.BlockSpec(subln_g.shape, lambda b: (0, 0))],
        out_specs=pl.BlockSpec((1, T, 2 * WIDTH), lambda b: (b, 0, 0)),
        out_shape=jax.ShapeDtypeStruct((B, T, 2 * WIDTH), BF16),
        scratch_shapes=[pltpu.VMEM((2, P // tk, nrow, tk), F32)],
        compiler_params=_params("arbitrary"),
        name="attn_sample",
    )(fq, fk, fv, dq, dk, dv, pfk_t, pfv_t, pdk_t, pdv_rows, e_t, cq, lambda_vecs, subln_g)


def _post_norm(x, h, gate, g, b, alpha):
    y = alpha * x + gate * h
    mu = jnp.mean(y, axis=-1, keepdims=True)
    yc = y - mu
    var = jnp.mean(yc * yc, axis=-1, keepdims=True)
    return yc * lax.rsqrt(var + LN_EPS) * g + b


def _post_kernel(x_ref, o_ref, mod_ref, prev_ref, wo_ref, ln1g_ref, ln1b_ref,
                 wup_ref, cw_ref, cb_ref, wdn_ref, ln2g_ref, ln2b_ref,
                 y_ref, conv_ref, carry_ref, hid_ref, *, alpha):
    bb, tt, d = x_ref.shape
    rows = bb * tt
    keep = CONV_WIDTH - 1
    t = pl.program_id(1)

    @pl.when(t == 0)
    def _():
        carry_ref[...] = prev_ref[...]

    m = mod_ref[...]

    def rows3(v):
        return jnp.broadcast_to(v, (bb, tt, v.shape[-1])).reshape(rows, v.shape[-1])

    x = x_ref[...].reshape(rows, d)
    h = jnp.dot(o_ref[...].reshape(rows, d), wo_ref[...], preferred_element_type=F32)
    x1 = _post_norm(x, h, rows3(m[:, 2:3, :]), ln1g_ref[...], ln1b_ref[...], alpha)
    u2 = (x1 * (1.0 + rows3(m[:, 4:5, :])) + rows3(m[:, 3:4, :])).astype(BF16)

    tpos = lax.broadcasted_iota(jnp.int32, (bb, tt, 1), 1).reshape(rows, 1)
    for c in range(D_FF // FF_TILE):
        cols = slice(c * FF_TILE, (c + 1) * FF_TILE)
        a = jnp.dot(u2, wup_ref[:, cols], preferred_element_type=F32)
        g = jnp.dot(u2, wup_ref[:, D_FF + c * FF_TILE:D_FF + (c + 1) * FF_TILE],
                    preferred_element_type=F32)
        prev = carry_ref[:, :, cols]
        p2 = rows3(prev[:, 0:1, :])
        p1 = rows3(prev[:, 1:2, :])
        am1 = jnp.where(tpos == 0, p1, pltpu.roll(a, 1, 0))
        am2 = jnp.where(tpos == 0, p2, jnp.where(tpos == 1, p1, pltpu.roll(a, 2, 0)))
        cw = cw_ref[:, cols]
        conv = cb_ref[:, cols] + am2 * cw[0:1] + am1 * cw[1:2] + a * cw[2:3]
        hid = conv * (1.0 / (1.0 + jnp.exp(-conv))) * g
        hid_ref[:, cols] = hid.astype(BF16)
        last = a.reshape(bb, tt, FF_TILE)[:, tt - keep:, :]
        carry_ref[:, :, cols] = last
        conv_ref[:, :, cols] = last

    f_out = jnp.dot(hid_ref[...], wdn_ref[...], preferred_element_type=F32)
    y = _post_norm(x1, f_out, rows3(m[:, 5:6, :]), ln2g_ref[...], ln2b_ref[...], alpha)
    y_ref[...] = y.reshape(bb, tt, d)


def _post(x, o, mod, conv_prev, wo, ln1g, ln1b, wup, cw, cb, wdn, ln2g, ln2b, bb, tt, alpha):
    B, T, d = x.shape
    keep = CONV_WIDTH - 1
    tok = pl.BlockSpec((bb, tt, d), lambda b, t: (b, t, 0))
    per_b = lambda r, w: pl.BlockSpec((bb, r, w), lambda b, t: (b, 0, 0))
    return pl.pallas_call(
        functools.partial(_post_kernel, alpha=alpha),
        grid=(B // bb, T // tt),
        in_specs=[tok, tok, per_b(6, d), per_b(keep, D_FF),
                  _resident(wo.shape), _resident(ln1g.shape), _resident(ln1b.shape),
                  _resident(wup.shape), _resident(cw.shape), _resident(cb.shape),
                  _resident(wdn.shape), _resident(ln2g.shape), _resident(ln2b.shape)],
        out_specs=[tok, per_b(keep, D_FF)],
        out_shape=[jax.ShapeDtypeStruct((B, T, d), F32),
                   jax.ShapeDtypeStruct((B, keep, D_FF), F32)],
        scratch_shapes=[pltpu.VMEM((bb, keep, D_FF), F32),
                        pltpu.VMEM((bb * tt, D_FF), BF16)],
        compiler_params=_params("arbitrary", "arbitrary"),
        name="post",
    )(x, o, mod, conv_prev, wo, ln1g, ln1b, wup, cw, cb, wdn, ln2g, ln2b)


def _rope_tables(pos0, T):
    inv = ROPE_THETA ** (-jnp.arange(0, HEAD_DIM, 2, dtype=F32) / HEAD_DIM)
    ang = (pos0 + jnp.arange(T)).astype(F32)[:, None] * inv[None, :]
    cos, sin = jnp.cos(ang), jnp.sin(ang)
    reps = WIDTH // HEAD_DIM
    return (jnp.tile(jnp.concatenate([cos, cos], axis=1), (1, reps)),
            jnp.tile(jnp.concatenate([-sin, sin], axis=1), (1, reps)))


def _round_up(n, k):
    return -(-n // k) * k


def _layer(x, mod, past, w, lam_init, alpha):
    B, T, d = x.shape
    P = 0 if past is None else past[0].shape[1]
    if T % TOKEN_TILE == 0:
        bb, tt = 1, TOKEN_TILE
    else:
        tt = T
        bb = math.gcd(B, max(1, TOKEN_TILE // T))
    cos_t, sin_t = _rope_tables(P, T)
    if bb > 1:
        cos_t, sin_t = jnp.tile(cos_t, (bb, 1)), jnp.tile(sin_t, (bb, 1))
    (fk, fv, lf_2d, dk, dv, fqb, fkb, fvb, dqb, dkb, dvb) = _inproj(
        x, mod, w["w_in"], w["b_f"], cos_t, sin_t, bb, tt, transposed=past is None)

    lf_t = jnp.swapaxes(lf_2d.reshape(FOX_HEADS, B, T), 0, 1)
    lf = jnp.swapaxes(lf_t, 1, 2)
    if past is None:
        cq_t, e = _cum(lf_t, "rows")
        o = _attn_prompt(fqb, fkb, e, fvb, dqb, dkb, dvb, cq_t,
                         w["lambda_vecs"], w["subln_g"], lam_init)
        conv_prev = jnp.zeros((B, CONV_WIDTH - 1, D_FF), F32)
    else:
        pfk, pfv, plf, pdk, pdv, conv_prev = past
        S = _round_up(P + LANES, MXU_DIM)
        lf_all = jnp.concatenate(
            [jnp.swapaxes(plf, 1, 2), lf_t, jnp.zeros((B, FOX_HEADS, S - P - T), F32)], axis=2)
        cq_t, e_t = _cum(lf_all, "lanes")
        cq = cq_t[:, :, P:P + T].reshape(B, FOX_HEADS * T, 1)
        keys_on_lanes = lambda c: jnp.transpose(c, (0, 2, 3, 1)).reshape(B, WIDTH, P)
        o = _attn_sample(fqb, fkb, fvb, dqb, dkb, dvb,
                         keys_on_lanes(pfk), keys_on_lanes(pfv), keys_on_lanes(pdk),
                         pdv.reshape(B, P * DIFF_HEADS, PAIR),
                         e_t, cq, w["lambda_vecs"], w["subln_g"], lam_init)

    y, conv = _post(x, o, mod, conv_prev, w["w_o"], w["ln1_g"], w["ln1_b"], w["w_up"],
                    w["conv_w"], w["conv_b"], w["w_down"], w["ln2_g"], w["ln2_b"], bb, tt, alpha)
    state = (fk.reshape(B, T, FOX_HEADS, HEAD_DIM), fv.reshape(B, T, FOX_HEADS, HEAD_DIM), lf,
             dk.reshape(B, T, 2 * DIFF_HEADS, HEAD_DIM), dv, conv)
    return y, state


def kernel(x_prompt, x_sample, c_prompt, c_sample, cache_fox_k, cache_fox_v, cache_fox_logf, cache_diff_k, cache_diff_v, state_ffn_conv, w_ada, b_ada, w_in, b_f, lambda_vecs, subln_g, w_o, ln1_g, ln1_b, w_up, conv_w, conv_b, w_down, ln2_g, ln2_b):
    depth = w_ada.shape[0]
    alpha = (2 * depth) ** 0.25
    nb = c_prompt.shape[0]
    yp, ys = x_prompt, x_sample
    c_all = jnp.concatenate([c_prompt, c_sample], axis=0)
    p_states, s_states = [], []
    for l in range(depth):
        lam_init = 0.8 - 0.6 * math.exp(-0.3 * l)
        w_in_l = w_in[l]
        gate_w = jnp.pad(w_in_l[:, OFF_FF:OFF_FF + FOX_HEADS], ((0, 0), (0, GATE_COLS - FOX_HEADS)))
        w = {
            "w_in": jnp.concatenate(
                [w_in_l[:, :OFF_FF], gate_w, w_in_l[:, OFF_FF + FOX_HEADS:]], axis=1).astype(BF16),
            "b_f": jnp.pad(b_f[l], (0, GATE_COLS - FOX_HEADS)).reshape(1, GATE_COLS),
            "lambda_vecs": lambda_vecs[l],
            "subln_g": subln_g[l].reshape(1, PAIR),
            "w_o": w_o[l].astype(BF16),
            "ln1_g": ln1_g[l].reshape(1, D_MODEL), "ln1_b": ln1_b[l].reshape(1, D_MODEL),
            "w_up": w_up[l].astype(BF16),
            "conv_w": conv_w[l], "conv_b": conv_b[l].reshape(1, D_FF),
            "w_down": w_down[l].astype(BF16),
            "ln2_g": ln2_g[l].reshape(1, D_MODEL), "ln2_b": ln2_b[l].reshape(1, D_MODEL),
        }
        mod = _ada(c_all, w_ada[l], b_ada[l]).reshape(c_all.shape[0], 6, D_MODEL)
        yp, st_p = _layer(yp, mod[:nb], None, w, lam_init, alpha)
        past = (cache_fox_k[l], cache_fox_v[l], cache_fox_logf[l], cache_diff_k[l], cache_diff_v[l],
                state_ffn_conv[l])
        ys, st_s = _layer(ys, mod[nb:], past, w, lam_init, alpha)
        p_states.append(st_p)
        s_states.append(st_s)
    p_out = [jnp.stack(a, axis=0) for a in zip(*p_states)]
    s_out = [jnp.stack(a, axis=0) for a in zip(*s_states)]
    return (yp, ys, *p_out, *s_out)
```

```python
import functools
import math

import jax
import jax.numpy as jnp
from jax import lax
from jax.experimental import pallas as pl
from jax.experimental.pallas import tpu as pltpu

F32 = jnp.float32
BF16 = jnp.bfloat16

D_MODEL = 1024
CHUNK = 64
FOX_HEADS = 8
DIFF_HEADS = 4
HEAD_DIM = 64
WIDTH = 512
PAIR = 2 * HEAD_DIM
N_PAIRS = WIDTH // PAIR
D_FF = 2816
CONV_WIDTH = 3
ROPE_THETA = 10000.0
LN_EPS = 1e-5
RMS_EPS = 1e-6
NEG_BIG = -1e30
Q_SCALE = HEAD_DIM ** -0.5
LOG2E = math.log2(math.e)

LANES = 128
MXU_DIM = 256
VMEM_LIMIT_BYTES = 56 * 1024 * 1024

GATE_COLS = LANES
OFF_FQ, OFF_FK, OFF_FV = 0, WIDTH, 2 * WIDTH
OFF_FF = 3 * WIDTH
OFF_DQ = OFF_FF + GATE_COLS
OFF_DK, OFF_DV = OFF_DQ + WIDTH, OFF_DQ + 2 * WIDTH
IN_COLS_PADDED = OFF_DV + WIDTH

KV_TILE = 256
TOKEN_TILE = 512
FF_TILE = 256
SUM_ROWS = 16
N_CHAINS = FOX_HEADS + 2 * DIFF_HEADS
CUM_SEQS = 4


def _params(*sem):
    return pltpu.CompilerParams(dimension_semantics=sem, vmem_limit_bytes=VMEM_LIMIT_BYTES)


def _resident(shape):
    nd = len(shape)
    return pl.BlockSpec(shape, lambda *_: (0,) * nd, pipeline_mode=pl.Buffered(1))


def _ada_kernel(c_ref, w_ref, b_ref, o_ref):
    c = c_ref[...]
    s = c * (1.0 / (1.0 + jnp.exp(-c)))
    o_ref[...] = jnp.dot(s.astype(BF16), w_ref[...].astype(BF16),
                         preferred_element_type=F32) + b_ref[...]


def _ada(c, w_ada, b_ada):
    n, d = c.shape
    cols = w_ada.shape[1]
    tn = cols // 4
    return pl.pallas_call(
        _ada_kernel,
        grid=(cols // tn,),
        in_specs=[pl.BlockSpec((n, d), lambda j: (0, 0)),
                  pl.BlockSpec((d, tn), lambda j: (0, j)),
                  pl.BlockSpec((1, tn), lambda j: (0, j))],
        out_specs=pl.BlockSpec((n, tn), lambda j: (0, j)),
        out_shape=jax.ShapeDtypeStruct((n, cols), F32),
        compiler_params=_params("arbitrary"),
        name="ada",
    )(c, w_ada, b_ada.reshape(1, cols))


def _inproj_kernel(x_ref, mod_ref, w_ref, bf_ref, cos_ref, sin_ref,
                   fk_o, fv_o, lf_o, dk_o, dv_o,
                   fqb_o, fkb_o, fvb_o, dqb_o, dkb_o, dvb_o, *, transposed):
    bb, tt, d = x_ref.shape
    m = mod_ref[...]
    u = x_ref[...] * (1.0 + m[:, 1:2, :]) + m[:, 0:1, :]
    u = u.reshape(bb * tt, d).astype(BF16)
    q_scale = Q_SCALE * LOG2E

    def proj(off, width):
        return jnp.dot(u, w_ref[:, off:off + width], preferred_element_type=F32)

    def put(o_ref, v):
        o_ref[...] = v.reshape(o_ref.shape).astype(o_ref.dtype)

    def put_qv(o_ref, v):
        if transposed:
            for c in range(tt // KV_TILE):
                o_ref[0, c] = v[c * KV_TILE:(c + 1) * KV_TILE, :].T.astype(o_ref.dtype)
        else:
            put(o_ref, v)

    fq = proj(OFF_FQ, WIDTH)
    put_qv(fqb_o, fq * q_scale)
    fk = proj(OFF_FK, WIDTH)
    put(fk_o, fk)
    put(fkb_o, fk)
    fv = proj(OFF_FV, WIDTH)
    put(fv_o, fv)
    put_qv(fvb_o, fv)

    zf = proj(OFF_FF, GATE_COLS) + bf_ref[...]
    lf = jnp.minimum(zf, 0.0) - jnp.log1p(jnp.exp(-jnp.abs(zf)))
    lf_o[...] = lf.T[:FOX_HEADS, :]

    cos = cos_ref[...]
    sin = sin_ref[...]
    lane = lax.broadcasted_iota(jnp.int32, (1, WIDTH), 1)
    first_half = (lane % HEAD_DIM) < (HEAD_DIM // 2)

    def rope(v):
        partner = jnp.where(first_half, pltpu.roll(v, WIDTH - HEAD_DIM // 2, 1),
                            pltpu.roll(v, HEAD_DIM // 2, 1))
        return v * cos + partner * sin

    dq = rope(proj(OFF_DQ, WIDTH))
    put_qv(dqb_o, dq * q_scale)
    dk = rope(proj(OFF_DK, WIDTH))
    put(dk_o, dk)
    put(dkb_o, dk)
    dv = proj(OFF_DV, WIDTH)
    for hd in range(DIFF_HEADS):
        dv_o[:, hd] = dv[:, hd * PAIR:(hd + 1) * PAIR].reshape(bb, tt, PAIR)
    put_qv(dvb_o, dv)


def _inproj(x, mod, w_pad, bf_pad, cos_t, sin_t, bb, tt, transposed):
    B, T, d = x.shape
    rows = bb * tt
    grid = (B // bb, T // tt)
    assert not transposed or (bb == 1 and tt % KV_TILE == 0)
    tok = lambda w: pl.BlockSpec((bb, tt, w), lambda b, t: (b, t, 0))
    tab = pl.BlockSpec((rows, WIDTH), (lambda b, t: (t, 0)) if bb == 1 else (lambda b, t: (0, 0)))
    sds = lambda w, dt: jax.ShapeDtypeStruct((B, T, w), dt)
    if transposed:
        qv = pl.BlockSpec((1, tt // KV_TILE, WIDTH, KV_TILE), lambda b, t: (b, t, 0, 0))
        qv_sds = jax.ShapeDtypeStruct((B, T // KV_TILE, WIDTH, KV_TILE), BF16)
    else:
        qv, qv_sds = tok(WIDTH), sds(WIDTH, BF16)
    return pl.pallas_call(
        functools.partial(_inproj_kernel, transposed=transposed),
        grid=grid,
        in_specs=[tok(d),
                  pl.BlockSpec((bb, 6, d), lambda b, t: (b, 0, 0)),
                  _resident(w_pad.shape),
                  _resident(bf_pad.shape),
                  tab, tab],
        out_specs=[tok(WIDTH), tok(WIDTH),
                   pl.BlockSpec((FOX_HEADS, rows), lambda b, t: (0, b * (T // tt) + t)),
                   tok(WIDTH),
                   pl.BlockSpec((bb, DIFF_HEADS, tt, PAIR), lambda b, t: (b, 0, t, 0)),
                   qv, tok(WIDTH), qv, qv, tok(WIDTH), qv],
        out_shape=[sds(WIDTH, F32), sds(WIDTH, F32), jax.ShapeDtypeStruct((FOX_HEADS, B * T), F32),
                   sds(WIDTH, F32),
                   jax.ShapeDtypeStruct((B, DIFF_HEADS, T, PAIR), F32),
                   qv_sds, sds(WIDTH, BF16), qv_sds, qv_sds, sds(WIDTH, BF16), qv_sds],
        compiler_params=_params("arbitrary", "arbitrary"),
        name="inproj",
    )(x, mod, w_pad, bf_pad, cos_t, sin_t)


def _split3(x):
    hi = x.astype(BF16).astype(F32)
    r1 = x - hi
    mid = r1.astype(BF16).astype(F32)
    lo = (r1 - mid).astype(BF16).astype(F32)
    return hi, mid, lo


def _cum_kernel(x_ref, o_ref, e_ref, *, keys_on_lanes):
    bb, _, S = x_ref.shape
    nblk = S // MXU_DIM
    r = lax.broadcasted_iota(jnp.int32, (MXU_DIM, MXU_DIM), 0)
    c = lax.broadcasted_iota(jnp.int32, (MXU_DIM, MXU_DIM), 1)
    tri = jnp.where(r <= c, 1.0, 0.0).astype(BF16)
    pieces = []
    for b in range(bb):
        for k in range(nblk):
            pieces += list(_split3(x_ref[b, :, k * MXU_DIM:(k + 1) * MXU_DIM]))
    local = jnp.dot(jnp.concatenate(pieces, axis=0).astype(BF16), tri, preferred_element_type=F32)
    pad = jnp.zeros((LANES - 3 * FOX_HEADS, MXU_DIM), F32)
    for b in range(bb):
        carry = jnp.zeros((FOX_HEADS, 1), F32)
        for k in range(nblk):
            sl = slice(k * MXU_DIM, (k + 1) * MXU_DIM)
            r0 = 3 * FOX_HEADS * (b * nblk + k)
            block = (local[r0:r0 + FOX_HEADS] + local[r0 + FOX_HEADS:r0 + 2 * FOX_HEADS]
                     + local[r0 + 2 * FOX_HEADS:r0 + 3 * FOX_HEADS])
            part = block + carry
            carry = carry + block[:, MXU_DIM - 1:MXU_DIM]
            scaled = part * LOG2E
            o_ref[b, :, sl] = scaled
            bias = jnp.concatenate(_split3(-scaled) + (pad,), axis=0)
            if keys_on_lanes:
                e_ref[b, :, sl] = bias.astype(BF16)
            else:
                e_ref[b, sl, :] = bias.T.astype(BF16)


def _cum(lf_t, bias_layout):
    B, H, S = lf_t.shape
    bb = math.gcd(B, CUM_SEQS)
    spec = pl.BlockSpec((bb, H, S), lambda b: (b, 0, 0))
    e_shape = (B, LANES, S) if bias_layout == "lanes" else (B, S, LANES)
    return pl.pallas_call(
        functools.partial(_cum_kernel, keys_on_lanes=bias_layout == "lanes"),
        grid=(B // bb,), in_specs=[spec],
        out_specs=[spec, pl.BlockSpec((bb,) + e_shape[1:], lambda b: (b, 0, 0))],
        out_shape=[jax.ShapeDtypeStruct((B, H, S), F32), jax.ShapeDtypeStruct(e_shape, BF16)],
        compiler_params=_params("arbitrary"),
        name="cum",
    )(lf_t)


def _lambda_value(lv_ref, lam_init):
    lv = lv_ref[...]
    a = jnp.sum(lv[0:1] * lv[1:2], axis=1, keepdims=True)
    b = jnp.sum(lv[2:3] * lv[3:4], axis=1, keepdims=True)
    return jnp.exp(a) - jnp.exp(b) + lam_init


def _attn_prompt_kernel(fq_ref, fqn_ref, fk_ref, e_ref, fv_ref, dq_ref, dqn_ref, dk_ref, dv_ref, cq_ref,
                        sel_ref, lv_ref, g_ref, o_ref, s_ref, mb_ref, w_ref, acc_ref, *, lam_init):
    tq = tk = KV_TILE
    i = pl.program_id(1)
    srow = lax.broadcasted_iota(jnp.int32, (PAIR, tq), 0)
    first = srow < HEAD_DIM
    half = tq // 2
    krow = lax.broadcasted_iota(jnp.int32, (half, half), 0)
    qcol = lax.broadcasted_iota(jnp.int32, (half, half), 1)
    causal = krow <= qcol
    chunk_causal = (krow // CHUNK) <= (qcol // CHUNK)
    zero = jnp.zeros((PAIR, tq), BF16)
    ones = jnp.ones((SUM_ROWS, tk), BF16)

    def rows_of(j):
        return pl.ds(pl.multiple_of(j * tk, tk), tk)

    def pair_lanes(p):
        return slice(p * PAIR, (p + 1) * PAIR)

    def weights(fq_src, dq_src):
        out = []
        for pair in range(N_PAIRS):
            qt = fq_src[0, 0, pair_lanes(pair), :]
            out += [jnp.concatenate([jnp.where(first, qt, zero), sel_ref[2 * pair]], axis=0),
                    jnp.concatenate([jnp.where(first, zero, qt), sel_ref[2 * pair + 1]], axis=0)]
        for hd in range(DIFF_HEADS):
            qt = dq_src[0, 0, pair_lanes(hd), :]
            out += [jnp.where(first, qt, zero), jnp.where(first, zero, qt)]
        return out

    chains = []
    for pair in range(N_PAIRS):
        lanes = pair_lanes(pair)
        keys = lambda j, lanes=lanes: jnp.concatenate([fk_ref[0, rows_of(j), lanes], e_ref[0, rows_of(j), :]], axis=1)
        for sub in range(2):
            head = 2 * pair + sub
            vt = lambda j, r0=head * HEAD_DIM: fv_ref[0, j, r0:r0 + HEAD_DIM, :]
            chains.append((2 * PAIR, cq_ref[0, head:head + 1, :], keys, vt, causal, HEAD_DIM))
    for hd in range(DIFF_HEADS):
        lanes = pair_lanes(hd)
        keys = lambda j, lanes=lanes: dk_ref[0, rows_of(j), lanes]
        vt = lambda j, lanes=lanes: dv_ref[0, j, lanes, :]
        chains += [(PAIR, None, keys, vt, chunk_causal, PAIR)] * 2
    n = len(chains)

    def score(c, j, w):
        return jnp.dot(chains[c][2](j), w, preferred_element_type=F32)

    def keep(c, s):
        s_ref[c] = s
        return jnp.max(s, axis=0, keepdims=True)

    def consume(c, j, mb, m):
        _, cq, _, vt, _, rows = chains[c]
        used = rows + SUM_ROWS
        s = s_ref[c]
        m_new = jnp.maximum(m, mb if cq is None else mb + cq)
        p = jnp.exp2(s - (m_new if cq is None else m_new - cq))
        alpha = jnp.exp2(m - m_new)
        acc_ref[c, :used] = alpha * acc_ref[c, :used] + jnp.dot(
            jnp.concatenate([vt(j), ones], axis=0), p.astype(BF16), preferred_element_type=F32)
        return m_new

    def consume_diagonal(c, m):
        _, cq, _, vt, mask, rows = chains[c]
        used = rows + SUM_ROWS
        lhs = jnp.concatenate([vt(i), ones], axis=0)
        for h in range(2):
            cols = slice(h * half, (h + 1) * half)
            s = jnp.where(mask, s_ref[c, h * half:(h + 1) * half, cols], NEG_BIG)
            if h:
                s = jnp.concatenate([s_ref[c, :half, cols], s], axis=0)
            mb = jnp.max(s, axis=0, keepdims=True)
            m_new = jnp.maximum(m[:, cols], mb if cq is None else mb + cq[:, cols])
            p = jnp.exp2(s - (m_new if cq is None else m_new - cq[:, cols]))
            alpha = jnp.exp2(m[:, cols] - m_new)
            acc_ref[c, :used, cols] = alpha * acc_ref[c, :used, cols] + jnp.dot(
                lhs[:, :s.shape[0]], p.astype(BF16), preferred_element_type=F32)

    @pl.when(i == 0)
    def _():
        for c, w in enumerate(weights(fq_ref, dq_ref)):
            w_ref[c, :chains[c][0]] = w
            mb_ref[c] = keep(c, score(c, 0, w))

    def body(j, carry):
        ms, mbs = list(carry[0]), list(carry[1])
        for c in range(n):
            s_next = score(c, j + 1, w_ref[c, :chains[c][0]])
            ms[c] = consume(c, j, mbs[c], ms[c])
            mbs[c] = keep(c, s_next)
        return tuple(ms), tuple(mbs)

    acc_ref[...] = jnp.zeros_like(acc_ref)
    carry = lax.fori_loop(
        0, i // 2, lambda jj, carry: body(2 * jj + 1, body(2 * jj, carry)),
        (tuple(jnp.full((1, tq), NEG_BIG, F32) for _ in range(n)), tuple(mb_ref[c] for c in range(n))))
    ms, _ = lax.cond(i % 2 == 1, lambda carry: body(i - 1, carry), lambda carry: carry, carry)
    outs = []
    for c, w_next in enumerate(weights(fqn_ref, dqn_ref)):
        rows = chains[c][5]
        s_next = score(c, 0, w_next)
        consume_diagonal(c, ms[c])
        w_ref[c, :chains[c][0]] = w_next
        mb_ref[c] = keep(c, s_next)
        outs.append(acc_ref[c, :rows] / acc_ref[c, rows:rows + 1])

    for pair in range(N_PAIRS):
        oa, ob = outs[2 * pair], outs[2 * pair + 1]
        o_ref[0, :, pair_lanes(pair)] = jnp.concatenate([oa, ob], axis=0).T.astype(o_ref.dtype)

    lam = _lambda_value(lv_ref, lam_init)
    for hd in range(DIFF_HEADS):
        o1, o2 = outs[2 * N_PAIRS + 2 * hd], outs[2 * N_PAIRS + 2 * hd + 1]
        d = o1 - lam * o2
        ms = jnp.mean(d * d, axis=0, keepdims=True)
        dn = (d * lax.rsqrt(ms + RMS_EPS)).T * g_ref[...] * (1.0 - lam_init)
        o_ref[0, :, WIDTH + hd * PAIR:WIDTH + (hd + 1) * PAIR] = dn.astype(o_ref.dtype)


def _attn_prompt(fq_t, fk, e, fv_t, dq_t, dk, dv_t, cq_t, lambda_vecs, subln_g, lam_init):
    B, T, _ = fk.shape
    tq = KV_TILE
    nt = T // tq
    qspec = pl.BlockSpec((1, 1, WIDTH, tq), lambda b, i: (b, i, 0, 0))
    qnext = pl.BlockSpec((1, 1, WIDTH, tq), lambda b, i: (b, jnp.minimum(i + 1, nt - 1), 0, 0))
    vspec = pl.BlockSpec((1, nt, WIDTH, tq), lambda b, i: (b, 0, 0, 0))
    kspec = pl.BlockSpec((1, T, WIDTH), lambda b, i: (b, 0, 0))
    slot = jnp.arange(PAIR)[None, :, None]
    sel = (slot % FOX_HEADS == jnp.arange(FOX_HEADS)[:, None, None]) & (slot < 3 * FOX_HEADS)
    sel = jnp.broadcast_to(sel, (FOX_HEADS, PAIR, tq)).astype(BF16)
    return pl.pallas_call(
        functools.partial(_attn_prompt_kernel, lam_init=lam_init),
        grid=(B, nt),
        in_specs=[qspec, qnext, kspec, pl.BlockSpec((1, T, LANES), lambda b, i: (b, 0, 0)), vspec,
                  qspec, qnext, kspec, vspec,
                  pl.BlockSpec((1, FOX_HEADS, tq), lambda b, i: (b, 0, i)),
                  _resident(sel.shape),
                  pl.BlockSpec(lambda_vecs.shape, lambda b, i: (0, 0)),
                  pl.BlockSpec(subln_g.shape, lambda b, i: (0, 0))],
        out_specs=pl.BlockSpec((1, tq, 2 * WIDTH), lambda b, i: (b, i, 0)),
        out_shape=jax.ShapeDtypeStruct((B, T, 2 * WIDTH), BF16),
        scratch_shapes=[pltpu.VMEM((N_CHAINS, tq, tq), F32),
                        pltpu.VMEM((N_CHAINS, 1, tq), F32),
                        pltpu.VMEM((N_CHAINS, 2 * PAIR, tq), BF16),
                        pltpu.VMEM((N_CHAINS, PAIR + SUM_ROWS, tq), F32)],
        compiler_params=_params("arbitrary", "arbitrary"),
        name="attn_prompt",
    )(fq_t, fq_t, fk, e, fv_t, dq_t, dq_t, dk, dv_t, cq_t, sel, lambda_vecs, subln_g)


def _attn_sample_kernel(fq_ref, fk_ref, fv_ref, dq_ref, dk_ref, dv_ref,
                        pfk_ref, pfv_ref, pdk_ref, pdv_ref, et_ref, cq_ref,
                        lv_ref, g_ref, o_ref, s_ref, *, lam_init):
    T = fq_ref.shape[1]
    P = pfk_ref.shape[2]
    tk = KV_TILE
    n_past = P // tk
    nrow = FOX_HEADS * T
    prow = 2 * T
    f_idx, d_idx = 0, 1
    nt_dims = (((1,), (1,)), ((), ()))

    row = lambda shape: lax.broadcasted_iota(jnp.int32, shape, 0)
    col = lambda shape: lax.broadcasted_iota(jnp.int32, shape, 1)

    own_head = row((nrow, WIDTH)) // T == col((nrow, WIDTH)) // HEAD_DIM
    select_t = jnp.where((col((nrow, LANES)) % FOX_HEADS == row((nrow, LANES)) // T)
                         & (col((nrow, LANES)) < 3 * FOX_HEADS), 1.0, 0.0).astype(BF16)

    def block_diag(q_ref):
        q = q_ref[0]
        return jnp.where(own_head, jnp.concatenate([q] * FOX_HEADS, axis=0), jnp.zeros((nrow, WIDTH), BF16))

    def pad_rows(x):
        return jnp.concatenate([x, jnp.zeros((LANES - T, x.shape[1]), x.dtype)], axis=0)

    def lane_halves(x, op):
        return op(x[:, :LANES], x[:, LANES:])

    qf = block_diag(fq_ref)
    qd = block_diag(dq_ref)
    lhs_f = jnp.concatenate([qf, select_t], axis=1)

    mf = md = jnp.full((nrow, LANES), NEG_BIG, F32)
    for j in range(n_past):
        cols = slice(j * tk, (j + 1) * tk)
        kf = jnp.concatenate([pfk_ref[0, :, cols].astype(BF16), et_ref[0, :, cols]], axis=0)
        sf = jnp.dot(lhs_f, kf, preferred_element_type=F32)
        sd = jnp.dot(qd, pdk_ref[0, :, cols].astype(BF16), preferred_element_type=F32)
        s_ref[f_idx, j] = sf
        s_ref[d_idx, j] = sd
        mf = jnp.maximum(mf, lane_halves(sf, jnp.maximum))
        md = jnp.maximum(md, lane_halves(sd, jnp.maximum))

    key = col((nrow, LANES))
    qpos = row((nrow, LANES)) % T
    valid = key < T
    causal = valid & (key <= qpos)
    chunk_causal = valid & ((key + P) // CHUNK <= (qpos + P) // CHUNK)
    sf_new = (lax.dot_general(qf, pad_rows(fk_ref[0]), nt_dims, preferred_element_type=F32)
              + jnp.dot(select_t, et_ref[0, :, P:P + LANES], preferred_element_type=F32))
    sf_new = jnp.where(causal, sf_new, NEG_BIG)
    sd_new = jnp.where(chunk_causal,
                       lax.dot_general(qd, pad_rows(dk_ref[0]), nt_dims, preferred_element_type=F32), NEG_BIG)
    cq = cq_ref[0]
    m_f = jnp.max(jnp.maximum(mf, sf_new), axis=1, keepdims=True) + cq
    shift_f = jnp.broadcast_to(m_f - cq, (nrow, LANES))
    shift_d = jnp.broadcast_to(jnp.max(jnp.maximum(md, sd_new), axis=1, keepdims=True), (nrow, LANES))

    def probs(s, shift):
        return jnp.exp2(s - jnp.concatenate([shift] * (s.shape[1] // LANES), axis=1))

    lf = ld = jnp.zeros((nrow, LANES), F32)
    acc_f = [jnp.zeros((prow, PAIR), F32) for _ in range(N_PAIRS)]
    acc_d = [jnp.zeros((prow, PAIR), F32) for _ in range(DIFF_HEADS)]
    for j in range(n_past):
        cols = slice(j * tk, (j + 1) * tk)
        pf = probs(s_ref[f_idx, j], shift_f)
        pd = probs(s_ref[d_idx, j], shift_d)
        lf = lf + lane_halves(pf, jnp.add)
        ld = ld + lane_halves(pd, jnp.add)
        pf, pd = pf.astype(BF16), pd.astype(BF16)
        for u in range(N_PAIRS):
            rows = slice(u * prow, (u + 1) * prow)
            acc_f[u] = acc_f[u] + lax.dot_general(
                pf[rows], pfv_ref[0, u * PAIR:(u + 1) * PAIR, cols].astype(BF16), nt_dims,
                preferred_element_type=F32)
            v = pdv_ref[0, pl.ds(j * tk * DIFF_HEADS + u, tk, stride=DIFF_HEADS), :].astype(BF16)
            acc_d[u] = acc_d[u] + jnp.dot(pd[rows], v, preferred_element_type=F32)
    pf = jnp.exp2(sf_new - shift_f)
    pd = jnp.exp2(sd_new - shift_d)
    lf = jnp.sum(lf + pf, axis=1, keepdims=True)
    ld = jnp.sum(ld + pd, axis=1, keepdims=True)
    pf, pd = pf.astype(BF16), pd.astype(BF16)
    fv_new, dv_new = pad_rows(fv_ref[0]), pad_rows(dv_ref[0])
    low = col((T, PAIR)) < HEAD_DIM
    lam = _lambda_value(lv_ref, lam_init)
    for u in range(N_PAIRS):
        rows = slice(u * prow, (u + 1) * prow)
        lanes = slice(u * PAIR, (u + 1) * PAIR)
        a = (acc_f[u] + jnp.dot(pf[rows], fv_new[:, lanes], preferred_element_type=F32)) / lf[rows]
        o_ref[0, :, lanes] = jnp.where(low, a[:T], a[T:]).astype(o_ref.dtype)
        a = (acc_d[u] + jnp.dot(pd[rows], dv_new[:, lanes], preferred_element_type=F32)) / ld[rows]
        d = a[:T] - lam * a[T:]
        ms = jnp.mean(d * d, axis=-1, keepdims=True)
        dn = d * lax.rsqrt(ms + RMS_EPS) * g_ref[...] * (1.0 - lam_init)
        o_ref[0, :, WIDTH + u * PAIR:WIDTH + (u + 1) * PAIR] = dn.astype(o_ref.dtype)


def _attn_sample(fq, fk, fv, dq, dk, dv, pfk_t, pfv_t, pdk_t, pdv_rows, e_t, cq, lambda_vecs, subln_g,
                 lam_init):
    B, T, _ = fq.shape
    P = pfk_t.shape[2]
    tk = KV_TILE
    nrow = FOX_HEADS * T
    assert N_PAIRS == DIFF_HEADS and T <= LANES and P % tk == 0 and e_t.shape[2] >= P + LANES
    new = pl.BlockSpec((1, T, WIDTH), lambda b: (b, 0, 0))
    cached = pl.BlockSpec((1, WIDTH, P), lambda b: (b, 0, 0))
    return pl.pallas_call(
        functools.partial(_attn_sample_kernel, lam_init=lam_init),
        grid=(B,),
        in_specs=[new] * 6 + [cached, cached, cached,
                              pl.BlockSpec((1, P * DIFF_HEADS, PAIR), lambda b: (b, 0, 0)),
                              pl.BlockSpec((1,) + e_t.shape[1:], lambda b: (b, 0, 0)),
                              pl.BlockSpec((1, nrow, 1), lambda b: (b, 0, 0)),
                              pl.BlockSpec(lambda_vecs.shape, lambda b: (0, 0)),
                              pl.BlockSpec(subln_g.shape, lambda b: (0, 0))],
        out_specs=pl.BlockSpec((1, T, 2 * WIDTH), lambda b: (b, 0, 0)),
        out_shape=jax.ShapeDtypeStruct((B, T, 2 * WIDTH), BF16),
        scratch_shapes=[pltpu.VMEM((2, P // tk, nrow, tk), F32)],
        compiler_params=_params("arbitrary"),
        name="attn_sample",
    )(fq, fk, fv, dq, dk, dv, pfk_t, pfv_t, pdk_t, pdv_rows, e_t, cq, lambda_vecs, subln_g)


def _post_norm(x, h, gate, g, b, alpha):
    y = alpha * x + gate * h
    mu = jnp.mean(y, axis=-1, keepdims=True)
    yc = y - mu
    var = jnp.mean(yc * yc, axis=-1, keepdims=True)
    return yc * lax.rsqrt(var + LN_EPS) * g + b


def _post_kernel(x_ref, o_ref, mod_ref, prev_ref, wo_ref, ln1g_ref, ln1b_ref,
                 wup_ref, cw_ref, cb_ref, wdn_ref, ln2g_ref, ln2b_ref,
                 y_ref, conv_ref, carry_ref, hid_ref, *, alpha):
    bb, tt, d = x_ref.shape
    rows = bb * tt
    keep = CONV_WIDTH - 1
    t = pl.program_id(1)

    @pl.when(t == 0)
    def _():
        carry_ref[...] = prev_ref[...]

    m = mod_ref[...]

    def rows3(v):
        return jnp.broadcast_to(v, (bb, tt, v.shape[-1])).reshape(rows, v.shape[-1])

    x = x_ref[...].reshape(rows, d)
    h = jnp.dot(o_ref[...].reshape(rows, d), wo_ref[...], preferred_element_type=F32)
    x1 = _post_norm(x, h, rows3(m[:, 2:3, :]), ln1g_ref[...], ln1b_ref[...], alpha)
    u2 = (x1 * (1.0 + rows3(m[:, 4:5, :])) + rows3(m[:, 3:4, :])).astype(BF16)

    tpos = lax.broadcasted_iota(jnp.int32, (bb, tt, 1), 1).reshape(rows, 1)
    for c in range(D_FF // FF_TILE):
        cols = slice(c * FF_TILE, (c + 1) * FF_TILE)
        a = jnp.dot(u2, wup_ref[:, cols], preferred_element_type=F32)
        g = jnp.dot(u2, wup_ref[:, D_FF + c * FF_TILE:D_FF + (c + 1) * FF_TILE],
                    preferred_element_type=F32)
        prev = carry_ref[:, :, cols]
        p2 = rows3(prev[:, 0:1, :])
        p1 = rows3(prev[:, 1:2, :])
        am1 = jnp.where(tpos == 0, p1, pltpu.roll(a, 1, 0))
        am2 = jnp.where(tpos == 0, p2, jnp.where(tpos == 1, p1, pltpu.roll(a, 2, 0)))
        cw = cw_ref[:, cols]
        conv = cb_ref[:, cols] + am2 * cw[0:1] + am1 * cw[1:2] + a * cw[2:3]
        hid = conv * (1.0 / (1.0 + jnp.exp(-conv))) * g
        hid_ref[:, cols] = hid.astype(BF16)
        last = a.reshape(bb, tt, FF_TILE)[:, tt - keep:, :]
        carry_ref[:, :, cols] = last
        conv_ref[:, :, cols] = last

    f_out = jnp.dot(hid_ref[...], wdn_ref[...], preferred_element_type=F32)
    y = _post_norm(x1, f_out, rows3(m[:, 5:6, :]), ln2g_ref[...], ln2b_ref[...], alpha)
    y_ref[...] = y.reshape(bb, tt, d)


def _post(x, o, mod, conv_prev, wo, ln1g, ln1b, wup, cw, cb, wdn, ln2g, ln2b, bb, tt, alpha):
    B, T, d = x.shape
    keep = CONV_WIDTH - 1
    tok = pl.BlockSpec((bb, tt, d), lambda b, t: (b, t, 0))
    per_b = lambda r, w: pl.BlockSpec((bb, r, w), lambda b, t: (b, 0, 0))
    return pl.pallas_call(
        functools.partial(_post_kernel, alpha=alpha),
        grid=(B // bb, T // tt),
        in_specs=[tok, tok, per_b(6, d), per_b(keep, D_FF),
                  _resident(wo.shape), _resident(ln1g.shape), _resident(ln1b.shape),
                  _resident(wup.shape), _resident(cw.shape), _resident(cb.shape),
                  _resident(wdn.shape), _resident(ln2g.shape), _resident(ln2b.shape)],
        out_specs=[tok, per_b(keep, D_FF)],
        out_shape=[jax.ShapeDtypeStruct((B, T, d), F32),
                   jax.ShapeDtypeStruct((B, keep, D_FF), F32)],
        scratch_shapes=[pltpu.VMEM((bb, keep, D_FF), F32),
                        pltpu.VMEM((bb * tt, D_FF), BF16)],
        compiler_params=_params("arbitrary", "arbitrary"),
        name="post",
    )(x, o, mod, conv_prev, wo, ln1g, ln1b, wup, cw, cb, wdn, ln2g, ln2b)


def _rope_tables(pos0, T):
    inv = ROPE_THETA ** (-jnp.arange(0, HEAD_DIM, 2, dtype=F32) / HEAD_DIM)
    ang = (pos0 + jnp.arange(T)).astype(F32)[:, None] * inv[None, :]
    cos, sin = jnp.cos(ang), jnp.sin(ang)
    reps = WIDTH // HEAD_DIM
    return (jnp.tile(jnp.concatenate([cos, cos], axis=1), (1, reps)),
            jnp.tile(jnp.concatenate([-sin, sin], axis=1), (1, reps)))


def _round_up(n, k):
    return -(-n // k) * k


def _layer(x, mod, past, w, lam_init, alpha):
    B, T, d = x.shape
    P = 0 if past is None else past[0].shape[1]
    if T % TOKEN_TILE == 0:
        bb, tt = 1, TOKEN_TILE
    else:
        tt = T
        bb = math.gcd(B, max(1, TOKEN_TILE // T))
    cos_t, sin_t = _rope_tables(P, T)
    if bb > 1:
        cos_t, sin_t = jnp.tile(cos_t, (bb, 1)), jnp.tile(sin_t, (bb, 1))
    (fk, fv, lf_2d, dk, dv, fqb, fkb, fvb, dqb, dkb, dvb) = _inproj(
        x, mod, w["w_in"], w["b_f"], cos_t, sin_t, bb, tt, transposed=past is None)

    lf_t = jnp.swapaxes(lf_2d.reshape(FOX_HEADS, B, T), 0, 1)
    lf = jnp.swapaxes(lf_t, 1, 2)
    if past is None:
        cq_t, e = _cum(lf_t, "rows")
        o = _attn_prompt(fqb, fkb, e, fvb, dqb, dkb, dvb, cq_t,
                         w["lambda_vecs"], w["subln_g"], lam_init)
        conv_prev = jnp.zeros((B, CONV_WIDTH - 1, D_FF), F32)
    else:
        pfk, pfv, plf, pdk, pdv, conv_prev = past
        S = _round_up(P + LANES, MXU_DIM)
        lf_all = jnp.concatenate(
            [jnp.swapaxes(plf, 1, 2), lf_t, jnp.zeros((B, FOX_HEADS, S - P - T), F32)], axis=2)
        cq_t, e_t = _cum(lf_all, "lanes")
        cq = cq_t[:, :, P:P + T].reshape(B, FOX_HEADS * T, 1)
        keys_on_lanes = lambda c: jnp.transpose(c, (0, 2, 3, 1)).reshape(B, WIDTH, P)
        o = _attn_sample(fqb, fkb, fvb, dqb, dkb, dvb,
                         keys_on_lanes(pfk), keys_on_lanes(pfv), keys_on_lanes(pdk),
                         pdv.reshape(B, P * DIFF_HEADS, PAIR),
                         e_t, cq, w["lambda_vecs"], w["subln_g"], lam_init)

    y, conv = _post(x, o, mod, conv_prev, w["w_o"], w["ln1_g"], w["ln1_b"], w["w_up"],
                    w["conv_w"], w["conv_b"], w["w_down"], w["ln2_g"], w["ln2_b"], bb, tt, alpha)
    state = (fk.reshape(B, T, FOX_HEADS, HEAD_DIM), fv.reshape(B, T, FOX_HEADS, HEAD_DIM), lf,
             dk.reshape(B, T, 2 * DIFF_HEADS, HEAD_DIM), jnp.swapaxes(dv, 1, 2), conv)
    return y, state


def kernel(x_prompt, x_sample, c_prompt, c_sample, cache_fox_k, cache_fox_v, cache_fox_logf, cache_diff_k, cache_diff_v, state_ffn_conv, w_ada, b_ada, w_in, b_f, lambda_vecs, subln_g, w_o, ln1_g, ln1_b, w_up, conv_w, conv_b, w_down, ln2_g, ln2_b):
    depth = w_ada.shape[0]
    alpha = (2 * depth) ** 0.25
    nb = c_prompt.shape[0]
    yp, ys = x_prompt, x_sample
    c_all = jnp.concatenate([c_prompt, c_sample], axis=0)
    p_states, s_states = [], []
    for l in range(depth):
        lam_init = 0.8 - 0.6 * math.exp(-0.3 * l)
        w_in_l = w_in[l]
        gate_w = jnp.pad(w_in_l[:, OFF_FF:OFF_FF + FOX_HEADS], ((0, 0), (0, GATE_COLS - FOX_HEADS)))
        w = {
            "w_in": jnp.concatenate(
                [w_in_l[:, :OFF_FF], gate_w, w_in_l[:, OFF_FF + FOX_HEADS:]], axis=1).astype(BF16),
            "b_f": jnp.pad(b_f[l], (0, GATE_COLS - FOX_HEADS)).reshape(1, GATE_COLS),
            "lambda_vecs": lambda_vecs[l],
            "subln_g": subln_g[l].reshape(1, PAIR),
            "w_o": w_o[l].astype(BF16),
            "ln1_g": ln1_g[l].reshape(1, D_MODEL), "ln1_b": ln1_b[l].reshape(1, D_MODEL),
            "w_up": w_up[l].astype(BF16),
            "conv_w": conv_w[l], "conv_b": conv_b[l].reshape(1, D_FF),
            "w_down": w_down[l].astype(BF16),
            "ln2_g": ln2_g[l].reshape(1, D_MODEL), "ln2_b": ln2_b[l].reshape(1, D_MODEL),
        }
        mod = _ada(c_all, w_ada[l], b_ada[l]).reshape(c_all.shape[0], 6, D_MODEL)
        yp, st_p = _layer(yp, mod[:nb], None, w, lam_init, alpha)
        past = (cache_fox_k[l], cache_fox_v[l], cache_fox_logf[l], cache_diff_k[l], cache_diff_v[l],
                state_ffn_conv[l])
        ys, st_s = _layer(ys, mod[nb:], past, w, lam_init, alpha)
        p_states.append(st_p)
        s_states.append(st_s)
    p_out = [jnp.stack(a, axis=0) for a in zip(*p_states)]
    s_out = [jnp.stack(a, axis=0) for a in zip(*s_states)]
    return (yp, ys, *p_out, *s_out)
```

```python
import functools
import math

import jax
import jax.numpy as jnp
from jax import lax
from jax.experimental import pallas as pl
from jax.experimental.pallas import tpu as pltpu

F32 = jnp.float32
BF16 = jnp.bfloat16

D_MODEL = 1024
CHUNK = 64
FOX_HEADS = 8
DIFF_HEADS = 4
HEAD_DIM = 64
WIDTH = 512
PAIR = 2 * HEAD_DIM
N_PAIRS = WIDTH // PAIR
D_FF = 2816
CONV_WIDTH = 3
ROPE_THETA = 10000.0
LN_EPS = 1e-5
RMS_EPS = 1e-6
NEG_BIG = -1e30
Q_SCALE = HEAD_DIM ** -0.5
LOG2E = math.log2(math.e)

LANES = 128
MXU_DIM = 256
VMEM_LIMIT_BYTES = 56 * 1024 * 1024

GATE_COLS = LANES
OFF_FQ, OFF_FK, OFF_FV = 0, WIDTH, 2 * WIDTH
OFF_FF = 3 * WIDTH
OFF_DQ = OFF_FF + GATE_COLS
OFF_DK, OFF_DV = OFF_DQ + WIDTH, OFF_DQ + 2 * WIDTH

KV_TILE = 256
TOKEN_TILE = 512
FF_TILE = 256
SUM_ROWS = 16
N_CHAINS = FOX_HEADS + 2 * DIFF_HEADS
CUM_SEQS = 4


def _params(*sem):
    return pltpu.CompilerParams(dimension_semantics=sem, vmem_limit_bytes=VMEM_LIMIT_BYTES)


def _resident(shape):
    nd = len(shape)
    return pl.BlockSpec(shape, lambda *_: (0,) * nd, pipeline_mode=pl.Buffered(1))


def _ada_kernel(c_ref, w_ref, b_ref, o_ref):
    c = c_ref[...]
    s = c * (1.0 / (1.0 + jnp.exp(-c)))
    o_ref[...] = jnp.dot(s.astype(BF16), w_ref[...].astype(BF16),
                         preferred_element_type=F32) + b_ref[...]


def _ada(c, w_ada, b_ada):
    n, d = c.shape
    cols = w_ada.shape[1]
    tn = cols // 4
    return pl.pallas_call(
        _ada_kernel,
        grid=(cols // tn,),
        in_specs=[pl.BlockSpec((n, d), lambda j: (0, 0)),
                  pl.BlockSpec((d, tn), lambda j: (0, j)),
                  pl.BlockSpec((1, tn), lambda j: (0, j))],
        out_specs=pl.BlockSpec((n, tn), lambda j: (0, j)),
        out_shape=jax.ShapeDtypeStruct((n, cols), F32),
        compiler_params=_params("arbitrary"),
        name="ada",
    )(c, w_ada, b_ada.reshape(1, cols))


def _inproj_kernel(x_ref, mod_ref, w_ref, bf_ref, cos_ref, sin_ref,
                   fk_o, fv_o, lf_o, dk_o, dv_o,
                   fqb_o, fkb_o, fvb_o, dqb_o, dkb_o, dvb_o, *, transposed):
    bb, tt, d = x_ref.shape
    m = mod_ref[...]
    u = x_ref[...] * (1.0 + m[:, 1:2, :]) + m[:, 0:1, :]
    u = u.reshape(bb * tt, d).astype(BF16)
    q_scale = Q_SCALE * LOG2E

    def proj(off, width):
        return jnp.dot(u, w_ref[:, off:off + width], preferred_element_type=F32)

    def put(o_ref, v):
        o_ref[...] = v.reshape(o_ref.shape).astype(o_ref.dtype)

    def put_qv(o_ref, v):
        if transposed:
            for c in range(tt // KV_TILE):
                o_ref[0, c] = v[c * KV_TILE:(c + 1) * KV_TILE, :].T.astype(o_ref.dtype)
        else:
            put(o_ref, v)

    fq = proj(OFF_FQ, WIDTH)
    put_qv(fqb_o, fq * q_scale)
    fk = proj(OFF_FK, WIDTH)
    put(fk_o, fk)
    put(fkb_o, fk)
    fv = proj(OFF_FV, WIDTH)
    put(fv_o, fv)
    put_qv(fvb_o, fv)

    zf = proj(OFF_FF, GATE_COLS) + bf_ref[...]
    lf = jnp.minimum(zf, 0.0) - jnp.log1p(jnp.exp(-jnp.abs(zf)))
    lf_o[...] = lf.T[:FOX_HEADS, :]

    cos = cos_ref[...]
    sin = sin_ref[...]
    lane = lax.broadcasted_iota(jnp.int32, (1, WIDTH), 1)
    first_half = (lane % HEAD_DIM) < (HEAD_DIM // 2)

    def rope(v):
        partner = jnp.where(first_half, pltpu.roll(v, WIDTH - HEAD_DIM // 2, 1),
                            pltpu.roll(v, HEAD_DIM // 2, 1))
        return v * cos + partner * sin

    dq = rope(proj(OFF_DQ, WIDTH))
    put_qv(dqb_o, dq * q_scale)
    dk = rope(proj(OFF_DK, WIDTH))
    put(dk_o, dk)
    put(dkb_o, dk)
    dv = proj(OFF_DV, WIDTH)
    for hd in range(DIFF_HEADS):
        dv_o[:, hd] = dv[:, hd * PAIR:(hd + 1) * PAIR].reshape(bb, tt, PAIR)
    put_qv(dvb_o, dv)


def _inproj(x, mod, w_pad, bf_pad, cos_t, sin_t, bb, tt, transposed):
    B, T, d = x.shape
    rows = bb * tt
    grid = (B // bb, T // tt)
    assert not transposed or (bb == 1 and tt % KV_TILE == 0)
    tok = lambda w: pl.BlockSpec((bb, tt, w), lambda b, t: (b, t, 0))
    tab = pl.BlockSpec((rows, WIDTH), (lambda b, t: (t, 0)) if bb == 1 else (lambda b, t: (0, 0)))
    sds = lambda w, dt: jax.ShapeDtypeStruct((B, T, w), dt)
    if transposed:
        qv = pl.BlockSpec((1, tt // KV_TILE, WIDTH, KV_TILE), lambda b, t: (b, t, 0, 0))
        qv_sds = jax.ShapeDtypeStruct((B, T // KV_TILE, WIDTH, KV_TILE), BF16)
    else:
        qv, qv_sds = tok(WIDTH), sds(WIDTH, BF16)
    return pl.pallas_call(
        functools.partial(_inproj_kernel, transposed=transposed),
        grid=grid,
        in_specs=[tok(d),
                  pl.BlockSpec((bb, 6, d), lambda b, t: (b, 0, 0)),
                  _resident(w_pad.shape),
                  _resident(bf_pad.shape),
                  tab, tab],
        out_specs=[tok(WIDTH), tok(WIDTH),
                   pl.BlockSpec((FOX_HEADS, rows), lambda b, t: (0, b * (T // tt) + t)),
                   tok(WIDTH),
                   pl.BlockSpec((bb, DIFF_HEADS, tt, PAIR), lambda b, t: (b, 0, t, 0)),
                   qv, tok(WIDTH), qv, qv, tok(WIDTH), qv],
        out_shape=[sds(WIDTH, F32), sds(WIDTH, F32), jax.ShapeDtypeStruct((FOX_HEADS, B * T), F32),
                   sds(WIDTH, F32),
                   jax.ShapeDtypeStruct((B, DIFF_HEADS, T, PAIR), F32),
                   qv_sds, sds(WIDTH, BF16), qv_sds, qv_sds, sds(WIDTH, BF16), qv_sds],
        compiler_params=_params("arbitrary", "arbitrary"),
        name="inproj",
    )(x, mod, w_pad, bf_pad, cos_t, sin_t)


def _split3(x):
    hi = x.astype(BF16).astype(F32)
    r1 = x - hi
    mid = r1.astype(BF16).astype(F32)
    lo = (r1 - mid).astype(BF16).astype(F32)
    return hi, mid, lo


def _cum_kernel(x_ref, o_ref, e_ref, *, keys_on_lanes):
    bb, _, S = x_ref.shape
    nblk = S // MXU_DIM
    r = lax.broadcasted_iota(jnp.int32, (MXU_DIM, MXU_DIM), 0)
    c = lax.broadcasted_iota(jnp.int32, (MXU_DIM, MXU_DIM), 1)
    tri = jnp.where(r <= c, 1.0, 0.0).astype(BF16)
    pieces = []
    for b in range(bb):
        for k in range(nblk):
            pieces += list(_split3(x_ref[b, :, k * MXU_DIM:(k + 1) * MXU_DIM]))
    local = jnp.dot(jnp.concatenate(pieces, axis=0).astype(BF16), tri, preferred_element_type=F32)
    pad = jnp.zeros((LANES - 3 * FOX_HEADS, MXU_DIM), F32)
    for b in range(bb):
        carry = jnp.zeros((FOX_HEADS, 1), F32)
        for k in range(nblk):
            sl = slice(k * MXU_DIM, (k + 1) * MXU_DIM)
            r0 = 3 * FOX_HEADS * (b * nblk + k)
            block = (local[r0:r0 + FOX_HEADS] + local[r0 + FOX_HEADS:r0 + 2 * FOX_HEADS]
                     + local[r0 + 2 * FOX_HEADS:r0 + 3 * FOX_HEADS])
            part = block + carry
            carry = carry + block[:, MXU_DIM - 1:MXU_DIM]
            scaled = part * LOG2E
            o_ref[b, :, sl] = scaled
            bias = jnp.concatenate(_split3(-scaled) + (pad,), axis=0)
            if keys_on_lanes:
                e_ref[b, :, sl] = bias.astype(BF16)
            else:
                e_ref[b, sl, :] = bias.T.astype(BF16)


def _cum(lf_t, bias_layout):
    B, H, S = lf_t.shape
    bb = math.gcd(B, CUM_SEQS)
    spec = pl.BlockSpec((bb, H, S), lambda b: (b, 0, 0))
    e_shape = (B, LANES, S) if bias_layout == "lanes" else (B, S, LANES)
    return pl.pallas_call(
        functools.partial(_cum_kernel, keys_on_lanes=bias_layout == "lanes"),
        grid=(B // bb,), in_specs=[spec],
        out_specs=[spec, pl.BlockSpec((bb,) + e_shape[1:], lambda b: (b, 0, 0))],
        out_shape=[jax.ShapeDtypeStruct((B, H, S), F32), jax.ShapeDtypeStruct(e_shape, BF16)],
        compiler_params=_params("arbitrary"),
        name="cum",
    )(lf_t)


def _lambda_value(lv_ref, lam_init):
    lv = lv_ref[...]
    a = jnp.sum(lv[0:1] * lv[1:2], axis=1, keepdims=True)
    b = jnp.sum(lv[2:3] * lv[3:4], axis=1, keepdims=True)
    return jnp.exp(a) - jnp.exp(b) + lam_init


def _attn_prompt_kernel(fq_ref, fqn_ref, fk_ref, e_ref, fv_ref, dq_ref, dqn_ref, dk_ref, dv_ref, cq_ref,
                        sel_ref, lv_ref, g_ref, o_ref, s_ref, mb_ref, w_ref, acc_ref, *, lam_init):
    tq = tk = KV_TILE
    i = pl.program_id(1)
    srow = lax.broadcasted_iota(jnp.int32, (PAIR, tq), 0)
    first = srow < HEAD_DIM
    half = tq // 2
    krow = lax.broadcasted_iota(jnp.int32, (half, half), 0)
    qcol = lax.broadcasted_iota(jnp.int32, (half, half), 1)
    causal = krow <= qcol
    chunk_causal = (krow // CHUNK) <= (qcol // CHUNK)
    zero = jnp.zeros((PAIR, tq), BF16)
    ones = jnp.ones((SUM_ROWS, tk), BF16)

    def rows_of(j):
        return pl.ds(pl.multiple_of(j * tk, tk), tk)

    def pair_lanes(p):
        return slice(p * PAIR, (p + 1) * PAIR)

    def weights(fq_src, dq_src):
        out = []
        for pair in range(N_PAIRS):
            qt = fq_src[0, 0, pair_lanes(pair), :]
            out += [jnp.concatenate([jnp.where(first, qt, zero), sel_ref[2 * pair]], axis=0),
                    jnp.concatenate([jnp.where(first, zero, qt), sel_ref[2 * pair + 1]], axis=0)]
        for hd in range(DIFF_HEADS):
            qt = dq_src[0, 0, pair_lanes(hd), :]
            out += [jnp.where(first, qt, zero), jnp.where(first, zero, qt)]
        return out

    chains = []
    for pair in range(N_PAIRS):
        lanes = pair_lanes(pair)
        keys = lambda j, lanes=lanes: jnp.concatenate([fk_ref[0, rows_of(j), lanes], e_ref[0, rows_of(j), :]], axis=1)
        for sub in range(2):
            head = 2 * pair + sub
            vt = lambda j, r0=head * HEAD_DIM: fv_ref[0, j, r0:r0 + HEAD_DIM, :]
            chains.append((2 * PAIR, cq_ref[0, head:head + 1, :], keys, vt, causal, HEAD_DIM))
    for hd in range(DIFF_HEADS):
        lanes = pair_lanes(hd)
        keys = lambda j, lanes=lanes: dk_ref[0, rows_of(j), lanes]
        vt = lambda j, lanes=lanes: dv_ref[0, j, lanes, :]
        chains += [(PAIR, None, keys, vt, chunk_causal, PAIR)] * 2
    n = len(chains)

    def score(c, j, w):
        return jnp.dot(chains[c][2](j), w, preferred_element_type=F32)

    def keep(c, s):
        s_ref[c] = s
        return jnp.max(s, axis=0, keepdims=True)

    def consume(c, j, mb, m):
        _, cq, _, vt, _, rows = chains[c]
        used = rows + SUM_ROWS
        s = s_ref[c]
        m_new = jnp.maximum(m, mb if cq is None else mb + cq)
        p = jnp.exp2(s - (m_new if cq is None else m_new - cq))
        alpha = jnp.exp2(m - m_new)
        acc_ref[c, :used] = alpha * acc_ref[c, :used] + jnp.dot(
            jnp.concatenate([vt(j), ones], axis=0), p.astype(BF16), preferred_element_type=F32)
        return m_new

    def consume_diagonal(c, m):
        _, cq, _, vt, mask, rows = chains[c]
        used = rows + SUM_ROWS
        ps, alphas = [], []
        for h in range(2):
            cols = slice(h * half, (h + 1) * half)
            s = jnp.where(mask, s_ref[c, h * half:(h + 1) * half, cols], NEG_BIG)
            if h:
                s = jnp.concatenate([s_ref[c, :half, cols], s], axis=0)
            mb = jnp.max(s, axis=0, keepdims=True)
            m_new = jnp.maximum(m[:, cols], mb if cq is None else mb + cq[:, cols])
            p = jnp.exp2(s - (m_new if cq is None else m_new - cq[:, cols])).astype(BF16)
            ps.append(p if h else jnp.concatenate([p, jnp.zeros((half, half), BF16)], axis=0))
            alphas.append(jnp.exp2(m[:, cols] - m_new))
        acc_ref[c, :used] = jnp.concatenate(alphas, axis=1) * acc_ref[c, :used] + jnp.dot(
            jnp.concatenate([vt(i), ones], axis=0), jnp.concatenate(ps, axis=1), preferred_element_type=F32)

    @pl.when(i == 0)
    def _():
        for c, w in enumerate(weights(fq_ref, dq_ref)):
            w_ref[c, :chains[c][0]] = w
            mb_ref[c] = keep(c, score(c, 0, w))

    def body(j, carry):
        ms, mbs = list(carry[0]), list(carry[1])
        for c in range(n):
            s_next = score(c, j + 1, w_ref[c, :chains[c][0]])
            ms[c] = consume(c, j, mbs[c], ms[c])
            mbs[c] = keep(c, s_next)
        return tuple(ms), tuple(mbs)

    acc_ref[...] = jnp.zeros_like(acc_ref)
    carry = lax.fori_loop(
        0, i // 2, lambda jj, carry: body(2 * jj + 1, body(2 * jj, carry)),
        (tuple(jnp.full((1, tq), NEG_BIG, F32) for _ in range(n)), tuple(mb_ref[c] for c in range(n))))
    ms, _ = lax.cond(i % 2 == 1, lambda carry: body(i - 1, carry), lambda carry: carry, carry)
    outs = []
    for c, w_next in enumerate(weights(fqn_ref, dqn_ref)):
        rows = chains[c][5]
        s_next = score(c, 0, w_next)
        consume_diagonal(c, ms[c])
        w_ref[c, :chains[c][0]] = w_next
        mb_ref[c] = keep(c, s_next)
        outs.append(acc_ref[c, :rows] / acc_ref[c, rows:rows + 1])

    for pair in range(N_PAIRS):
        oa, ob = outs[2 * pair], outs[2 * pair + 1]
        o_ref[0, :, pair_lanes(pair)] = jnp.concatenate([oa, ob], axis=0).T.astype(o_ref.dtype)

    lam = _lambda_value(lv_ref, lam_init)
    for hd in range(DIFF_HEADS):
        o1, o2 = outs[2 * N_PAIRS + 2 * hd], outs[2 * N_PAIRS + 2 * hd + 1]
        d = o1 - lam * o2
        ms = jnp.mean(d * d, axis=0, keepdims=True)
        dn = (d * lax.rsqrt(ms + RMS_EPS)).T * g_ref[...] * (1.0 - lam_init)
        o_ref[0, :, WIDTH + hd * PAIR:WIDTH + (hd + 1) * PAIR] = dn.astype(o_ref.dtype)


def _attn_prompt(fq_t, fk, e, fv_t, dq_t, dk, dv_t, cq_t, lambda_vecs, subln_g, lam_init):
    B, T, _ = fk.shape
    tq = KV_TILE
    nt = T // tq
    qspec = pl.BlockSpec((1, 1, WIDTH, tq), lambda b, i: (b, i, 0, 0))
    qnext = pl.BlockSpec((1, 1, WIDTH, tq), lambda b, i: (b, jnp.minimum(i + 1, nt - 1), 0, 0))
    vspec = pl.BlockSpec((1, nt, WIDTH, tq), lambda b, i: (b, 0, 0, 0))
    kspec = pl.BlockSpec((1, T, WIDTH), lambda b, i: (b, 0, 0))
    slot = jnp.arange(PAIR)[None, :, None]
    sel = (slot % FOX_HEADS == jnp.arange(FOX_HEADS)[:, None, None]) & (slot < 3 * FOX_HEADS)
    sel = jnp.broadcast_to(sel, (FOX_HEADS, PAIR, tq)).astype(BF16)
    return pl.pallas_call(
        functools.partial(_attn_prompt_kernel, lam_init=lam_init),
        grid=(B, nt),
        in_specs=[qspec, qnext, kspec, pl.BlockSpec((1, T, LANES), lambda b, i: (b, 0, 0)), vspec,
                  qspec, qnext, kspec, vspec,
                  pl.BlockSpec((1, FOX_HEADS, tq), lambda b, i: (b, 0, i)),
                  _resident(sel.shape),
                  pl.BlockSpec(lambda_vecs.shape, lambda b, i: (0, 0)),
                  pl.BlockSpec(subln_g.shape, lambda b, i: (0, 0))],
        out_specs=pl.BlockSpec((1, tq, 2 * WIDTH), lambda b, i: (b, i, 0)),
        out_shape=jax.ShapeDtypeStruct((B, T, 2 * WIDTH), BF16),
        scratch_shapes=[pltpu.VMEM((N_CHAINS, tq, tq), F32),
                        pltpu.VMEM((N_CHAINS, 1, tq), F32),
                        pltpu.VMEM((N_CHAINS, 2 * PAIR, tq), BF16),
                        pltpu.VMEM((N_CHAINS, PAIR + SUM_ROWS, tq), F32)],
        compiler_params=_params("arbitrary", "arbitrary"),
        name="attn_prompt",
    )(fq_t, fq_t, fk, e, fv_t, dq_t, dq_t, dk, dv_t, cq_t, sel, lambda_vecs, subln_g)


def _attn_sample_kernel(fq_ref, fk_ref, fv_ref, dq_ref, dk_ref, dv_ref,
                        pfk_ref, pfv_ref, pdk_ref, pdv_ref, et_ref, cq_ref,
                        lv_ref, g_ref, o_ref, s_ref, *, lam_init):
    T = fq_ref.shape[1]
    P = pfk_ref.shape[2]
    tk = KV_TILE
    n_past = P // tk
    nrow = FOX_HEADS * T
    prow = 2 * T
    f_idx, d_idx = 0, 1
    nt_dims = (((1,), (1,)), ((), ()))

    row = lambda shape: lax.broadcasted_iota(jnp.int32, shape, 0)
    col = lambda shape: lax.broadcasted_iota(jnp.int32, shape, 1)

    own_head = row((nrow, WIDTH)) // T == col((nrow, WIDTH)) // HEAD_DIM
    select_t = jnp.where((col((nrow, LANES)) % FOX_HEADS == row((nrow, LANES)) // T)
                         & (col((nrow, LANES)) < 3 * FOX_HEADS), 1.0, 0.0).astype(BF16)

    def block_diag(q_ref):
        q = q_ref[0]
        return jnp.where(own_head, jnp.concatenate([q] * FOX_HEADS, axis=0), jnp.zeros((nrow, WIDTH), BF16))

    def pad_rows(x):
        return jnp.concatenate([x, jnp.zeros((LANES - T, x.shape[1]), x.dtype)], axis=0)

    def lane_halves(x, op):
        return op(x[:, :LANES], x[:, LANES:])

    qf = block_diag(fq_ref)
    qd = block_diag(dq_ref)
    lhs_f = jnp.concatenate([qf, select_t], axis=1)

    mf = md = jnp.full((nrow, LANES), NEG_BIG, F32)
    for j in range(n_past):
        cols = slice(j * tk, (j + 1) * tk)
        kf = jnp.concatenate([pfk_ref[0, :, cols].astype(BF16), et_ref[0, :, cols]], axis=0)
        sf = jnp.dot(lhs_f, kf, preferred_element_type=F32)
        sd = jnp.dot(qd, pdk_ref[0, :, cols].astype(BF16), preferred_element_type=F32)
        s_ref[f_idx, j] = sf
        s_ref[d_idx, j] = sd
        mf = jnp.maximum(mf, lane_halves(sf, jnp.maximum))
        md = jnp.maximum(md, lane_halves(sd, jnp.maximum))

    key = col((nrow, LANES))
    qpos = row((nrow, LANES)) % T
    valid = key < T
    causal = valid & (key <= qpos)
    chunk_causal = valid & ((key + P) // CHUNK <= (qpos + P) // CHUNK)
    sf_new = (lax.dot_general(qf, pad_rows(fk_ref[0]), nt_dims, preferred_element_type=F32)
              + jnp.dot(select_t, et_ref[0, :, P:P + LANES], preferred_element_type=F32))
    sf_new = jnp.where(causal, sf_new, NEG_BIG)
    sd_new = jnp.where(chunk_causal,
                       lax.dot_general(qd, pad_rows(dk_ref[0]), nt_dims, preferred_element_type=F32), NEG_BIG)
    cq = cq_ref[0]
    m_f = jnp.max(jnp.maximum(mf, sf_new), axis=1, keepdims=True) + cq
    shift_f = jnp.broadcast_to(m_f - cq, (nrow, LANES))
    shift_d = jnp.broadcast_to(jnp.max(jnp.maximum(md, sd_new), axis=1, keepdims=True), (nrow, LANES))

    def probs(s, shift):
        return jnp.exp2(s - jnp.concatenate([shift] * (s.shape[1] // LANES), axis=1))

    lf = ld = jnp.zeros((nrow, LANES), F32)
    acc_f = [jnp.zeros((prow, PAIR), F32) for _ in range(N_PAIRS)]
    acc_d = [jnp.zeros((prow, PAIR), F32) for _ in range(DIFF_HEADS)]
    for j in range(n_past):
        cols = slice(j * tk, (j + 1) * tk)
        pf = probs(s_ref[f_idx, j], shift_f)
        pd = probs(s_ref[d_idx, j], shift_d)
        lf = lf + lane_halves(pf, jnp.add)
        ld = ld + lane_halves(pd, jnp.add)
        pf, pd = pf.astype(BF16), pd.astype(BF16)
        for u in range(N_PAIRS):
            rows = slice(u * prow, (u + 1) * prow)
            acc_f[u] = acc_f[u] + lax.dot_general(
                pf[rows], pfv_ref[0, u * PAIR:(u + 1) * PAIR, cols].astype(BF16), nt_dims,
                preferred_element_type=F32)
            v = pdv_ref[0, pl.ds(j * tk * DIFF_HEADS + u, tk, stride=DIFF_HEADS), :].astype(BF16)
            acc_d[u] = acc_d[u] + jnp.dot(pd[rows], v, preferred_element_type=F32)
    pf = jnp.exp2(sf_new - shift_f)
    pd = jnp.exp2(sd_new - shift_d)
    lf = jnp.sum(lf + pf, axis=1, keepdims=True)
    ld = jnp.sum(ld + pd, axis=1, keepdims=True)
    pf, pd = pf.astype(BF16), pd.astype(BF16)
    fv_new, dv_new = pad_rows(fv_ref[0]), pad_rows(dv_ref[0])
    low = col((T, PAIR)) < HEAD_DIM
    lam = _lambda_value(lv_ref, lam_init)
    for u in range(N_PAIRS):
        rows = slice(u * prow, (u + 1) * prow)
        lanes = slice(u * PAIR, (u + 1) * PAIR)
        a = (acc_f[u] + jnp.dot(pf[rows], fv_new[:, lanes], preferred_element_type=F32)) / lf[rows]
        o_ref[0, :, lanes] = jnp.where(low, a[:T], a[T:]).astype(o_ref.dtype)
        a = (acc_d[u] + jnp.dot(pd[rows], dv_new[:, lanes], preferred_element_type=F32)) / ld[rows]
        d = a[:T] - lam * a[T:]
        ms = jnp.mean(d * d, axis=-1, keepdims=True)
        dn = d * lax.rsqrt(ms + RMS_EPS) * g_ref[...] * (1.0 - lam_init)
        o_ref[0, :, WIDTH + u * PAIR:WIDTH + (u + 1) * PAIR] = dn.astype(o_ref.dtype)


def _attn_sample(fq, fk, fv, dq, dk, dv, pfk_t, pfv_t, pdk_t, pdv_rows, e_t, cq, lambda_vecs, subln_g,
                 lam_init):
    B, T, _ = fq.shape
    P = pfk_t.shape[2]
    tk = KV_TILE
    nrow = FOX_HEADS * T
    assert N_PAIRS == DIFF_HEADS and T <= LANES and P % tk == 0 and e_t.shape[2] >= P + LANES
    new = pl.BlockSpec((1, T, WIDTH), lambda b: (b, 0, 0))
    cached = pl.BlockSpec((1, WIDTH, P), lambda b: (b, 0, 0))
    return pl.pallas_call(
        functools.partial(_attn_sample_kernel, lam_init=lam_init),
        grid=(B,),
        in_specs=[new] * 6 + [cached, cached, cached,
                              pl.BlockSpec((1, P * DIFF_HEADS, PAIR), lambda b: (b, 0, 0)),
                              pl.BlockSpec((1,) + e_t.shape[1:], lambda b: (b, 0, 0)),
                              pl.BlockSpec((1, nrow, 1), lambda b: (b, 0, 0)),
                              pl.BlockSpec(lambda_vecs.shape, lambda b: (0, 0)),
                              pl.BlockSpec(subln_g.shape, lambda b: (0, 0))],
        out_specs=pl.BlockSpec((1, T, 2 * WIDTH), lambda b: (b, 0, 0)),
        out_shape=jax.ShapeDtypeStruct((B, T, 2 * WIDTH), BF16),
        scratch_shapes=[pltpu.VMEM((2, P // tk, nrow, tk), F32)],
        compiler_params=_params("arbitrary"),
        name="attn_sample",
    )(fq, fk, fv, dq, dk, dv, pfk_t, pfv_t, pdk_t, pdv_rows, e_t, cq, lambda_vecs, subln_g)


def _post_norm(x, h, gate, g, b, alpha):
    y = alpha * x + gate * h
    mu = jnp.mean(y, axis=-1, keepdims=True)
    yc = y - mu
    var = jnp.mean(yc * yc, axis=-1, keepdims=True)
    return yc * lax.rsqrt(var + LN_EPS) * g + b


def _post_kernel(x_ref, o_ref, mod_ref, prev_ref, wo_ref, ln1g_ref, ln1b_ref,
                 wup_ref, cw_ref, cb_ref, wdn_ref, ln2g_ref, ln2b_ref,
                 y_ref, conv_ref, carry_ref, hid_ref, *, alpha):
    bb, tt, d = x_ref.shape
    rows = bb * tt
    keep = CONV_WIDTH - 1
    t = pl.program_id(1)

    @pl.when(t == 0)
    def _():
        carry_ref[...] = prev_ref[...]

    m = mod_ref[...]

    def rows3(v):
        return jnp.broadcast_to(v, (bb, tt, v.shape[-1])).reshape(rows, v.shape[-1])

    x = x_ref[...].reshape(rows, d)
    h = jnp.dot(o_ref[...].reshape(rows, d), wo_ref[...], preferred_element_type=F32)
    x1 = _post_norm(x, h, rows3(m[:, 2:3, :]), ln1g_ref[...], ln1b_ref[...], alpha)
    u2 = (x1 * (1.0 + rows3(m[:, 4:5, :])) + rows3(m[:, 3:4, :])).astype(BF16)

    tpos = lax.broadcasted_iota(jnp.int32, (bb, tt, 1), 1).reshape(rows, 1)
    for c in range(D_FF // FF_TILE):
        cols = slice(c * FF_TILE, (c + 1) * FF_TILE)
        a = jnp.dot(u2, wup_ref[:, cols], preferred_element_type=F32)
        g = jnp.dot(u2, wup_ref[:, D_FF + c * FF_TILE:D_FF + (c + 1) * FF_TILE],
                    preferred_element_type=F32)
        prev = carry_ref[:, :, cols]
        p2 = rows3(prev[:, 0:1, :])
        p1 = rows3(prev[:, 1:2, :])
        am1 = jnp.where(tpos == 0, p1, pltpu.roll(a, 1, 0))
        am2 = jnp.where(tpos == 0, p2, jnp.where(tpos == 1, p1, pltpu.roll(a, 2, 0)))
        cw = cw_ref[:, cols]
        conv = cb_ref[:, cols] + am2 * cw[0:1] + am1 * cw[1:2] + a * cw[2:3]
        hid = conv * (1.0 / (1.0 + jnp.exp(-conv))) * g
        hid_ref[:, cols] = hid.astype(BF16)
        last = a.reshape(bb, tt, FF_TILE)[:, tt - keep:, :]
        carry_ref[:, :, cols] = last
        conv_ref[:, :, cols] = last

    f_out = jnp.dot(hid_ref[...], wdn_ref[...], preferred_element_type=F32)
    y = _post_norm(x1, f_out, rows3(m[:, 5:6, :]), ln2g_ref[...], ln2b_ref[...], alpha)
    y_ref[...] = y.reshape(bb, tt, d)


def _post(x, o, mod, conv_prev, wo, ln1g, ln1b, wup, cw, cb, wdn, ln2g, ln2b, bb, tt, alpha):
    B, T, d = x.shape
    keep = CONV_WIDTH - 1
    tok = pl.BlockSpec((bb, tt, d), lambda b, t: (b, t, 0))
    per_b = lambda r, w: pl.BlockSpec((bb, r, w), lambda b, t: (b, 0, 0))
    return pl.pallas_call(
        functools.partial(_post_kernel, alpha=alpha),
        grid=(B // bb, T // tt),
        in_specs=[tok, tok, per_b(6, d), per_b(keep, D_FF),
                  _resident(wo.shape), _resident(ln1g.shape), _resident(ln1b.shape),
                  _resident(wup.shape), _resident(cw.shape), _resident(cb.shape),
                  _resident(wdn.shape), _resident(ln2g.shape), _resident(ln2b.shape)],
        out_specs=[tok, per_b(keep, D_FF)],
        out_shape=[jax.ShapeDtypeStruct((B, T, d), F32),
                   jax.ShapeDtypeStruct((B, keep, D_FF), F32)],
        scratch_shapes=[pltpu.VMEM((bb, keep, D_FF), F32),
                        pltpu.VMEM((bb * tt, D_FF), BF16)],
        compiler_params=_params("arbitrary", "arbitrary"),
        name="post",
    )(x, o, mod, conv_prev, wo, ln1g, ln1b, wup, cw, cb, wdn, ln2g, ln2b)


def _rope_tables(pos0, T):
    inv = ROPE_THETA ** (-jnp.arange(0, HEAD_DIM, 2, dtype=F32) / HEAD_DIM)
    ang = (pos0 + jnp.arange(T)).astype(F32)[:, None] * inv[None, :]
    cos, sin = jnp.cos(ang), jnp.sin(ang)
    reps = WIDTH // HEAD_DIM
    return (jnp.tile(jnp.concatenate([cos, cos], axis=1), (1, reps)),
            jnp.tile(jnp.concatenate([-sin, sin], axis=1), (1, reps)))


def _round_up(n, k):
    return -(-n // k) * k


def _layer(x, mod, past, w, lam_init, alpha):
    B, T, d = x.shape
    P = 0 if past is None else past[0].shape[1]
    if T % TOKEN_TILE == 0:
        bb, tt = 1, TOKEN_TILE
    else:
        tt = T
        bb = math.gcd(B, max(1, TOKEN_TILE // T))
    cos_t, sin_t = _rope_tables(P, T)
    if bb > 1:
        cos_t, sin_t = jnp.tile(cos_t, (bb, 1)), jnp.tile(sin_t, (bb, 1))
    (fk, fv, lf_2d, dk, dv, fqb, fkb, fvb, dqb, dkb, dvb) = _inproj(
        x, mod, w["w_in"], w["b_f"], cos_t, sin_t, bb, tt, transposed=past is None)

    lf_t = jnp.swapaxes(lf_2d.reshape(FOX_HEADS, B, T), 0, 1)
    lf = jnp.swapaxes(lf_t, 1, 2)
    if past is None:
        cq_t, e = _cum(lf_t, "rows")
        o = _attn_prompt(fqb, fkb, e, fvb, dqb, dkb, dvb, cq_t,
                         w["lambda_vecs"], w["subln_g"], lam_init)
        conv_prev = jnp.zeros((B, CONV_WIDTH - 1, D_FF), F32)
    else:
        pfk, pfv, plf, pdk, pdv, conv_prev = past
        S = _round_up(P + LANES, MXU_DIM)
        lf_all = jnp.concatenate(
            [jnp.swapaxes(plf, 1, 2), lf_t, jnp.zeros((B, FOX_HEADS, S - P - T), F32)], axis=2)
        cq_t, e_t = _cum(lf_all, "lanes")
        cq = cq_t[:, :, P:P + T].reshape(B, FOX_HEADS * T, 1)
        keys_on_lanes = lambda c: jnp.transpose(c, (0, 2, 3, 1)).reshape(B, WIDTH, P)
        o = _attn_sample(fqb, fkb, fvb, dqb, dkb, dvb,
                         keys_on_lanes(pfk), keys_on_lanes(pfv), keys_on_lanes(pdk),
                         pdv.reshape(B, P * DIFF_HEADS, PAIR),
                         e_t, cq, w["lambda_vecs"], w["subln_g"], lam_init)

    y, conv = _post(x, o, mod, conv_prev, w["w_o"], w["ln1_g"], w["ln1_b"], w["w_up"],
                    w["conv_w"], w["conv_b"], w["w_down"], w["ln2_g"], w["ln2_b"], bb, tt, alpha)
    state = (fk.reshape(B, T, FOX_HEADS, HEAD_DIM), fv.reshape(B, T, FOX_HEADS, HEAD_DIM), lf,
             dk.reshape(B, T, 2 * DIFF_HEADS, HEAD_DIM), jnp.swapaxes(dv, 1, 2), conv)
    return y, state


def kernel(x_prompt, x_sample, c_prompt, c_sample, cache_fox_k, cache_fox_v, cache_fox_logf, cache_diff_k, cache_diff_v, state_ffn_conv, w_ada, b_ada, w_in, b_f, lambda_vecs, subln_g, w_o, ln1_g, ln1_b, w_up, conv_w, conv_b, w_down, ln2_g, ln2_b):
    depth = w_ada.shape[0]
    alpha = (2 * depth) ** 0.25
    nb = c_prompt.shape[0]
    yp, ys = x_prompt, x_sample
    c_all = jnp.concatenate([c_prompt, c_sample], axis=0)
    p_states, s_states = [], []
    for l in range(depth):
        lam_init = 0.8 - 0.6 * math.exp(-0.3 * l)
        w_in_l = w_in[l]
        gate_w = jnp.pad(w_in_l[:, OFF_FF:OFF_FF + FOX_HEADS], ((0, 0), (0, GATE_COLS - FOX_HEADS)))
        w = {
            "w_in": jnp.concatenate(
                [w_in_l[:, :OFF_FF], gate_w, w_in_l[:, OFF_FF + FOX_HEADS:]], axis=1).astype(BF16),
            "b_f": jnp.pad(b_f[l], (0, GATE_COLS - FOX_HEADS)).reshape(1, GATE_COLS),
            "lambda_vecs": lambda_vecs[l],
            "subln_g": subln_g[l].reshape(1, PAIR),
            "w_o": w_o[l].astype(BF16),
            "ln1_g": ln1_g[l].reshape(1, D_MODEL), "ln1_b": ln1_b[l].reshape(1, D_MODEL),
            "w_up": w_up[l].astype(BF16),
            "conv_w": conv_w[l], "conv_b": conv_b[l].reshape(1, D_FF),
            "w_down": w_down[l].astype(BF16),
            "ln2_g": ln2_g[l].reshape(1, D_MODEL), "ln2_b": ln2_b[l].reshape(1, D_MODEL),
        }
        mod = _ada(c_all, w_ada[l], b_ada[l]).reshape(c_all.shape[0], 6, D_MODEL)
        yp, st_p = _layer(yp, mod[:nb], None, w, lam_init, alpha)
        past = (cache_fox_k[l], cache_fox_v[l], cache_fox_logf[l], cache_diff_k[l], cache_diff_v[l],
                state_ffn_conv[l])
        ys, st_s = _layer(ys, mod[nb:], past, w, lam_init, alpha)
        p_states.append(st_p)
        s_states.append(st_s)
    p_out = [jnp.stack(a, axis=0) for a in zip(*p_states)]
    s_out = [jnp.stack(a, axis=0) for a in zip(*s_states)]
    return (yp, ys, *p_out, *s_out)
```

```python
import functools
import math

import jax
import jax.numpy as jnp
from jax import lax
from jax.experimental import pallas as pl
from jax.experimental.pallas import tpu as pltpu

F32 = jnp.float32
BF16 = jnp.bfloat16

D_MODEL = 1024
CHUNK = 64
FOX_HEADS = 8
DIFF_HEADS = 4
HEAD_DIM = 64
WIDTH = 512
PAIR = 2 * HEAD_DIM
N_PAIRS = WIDTH // PAIR
D_FF = 2816
CONV_WIDTH = 3
ROPE_THETA = 10000.0
LN_EPS = 1e-5
RMS_EPS = 1e-6
NEG_BIG = -1e30
Q_SCALE = HEAD_DIM ** -0.5
LOG2E = math.log2(math.e)

LANES = 128
MXU_DIM = 256
VMEM_LIMIT_BYTES = 56 * 1024 * 1024

GATE_COLS = LANES
OFF_FQ, OFF_FK, OFF_FV = 0, WIDTH, 2 * WIDTH
OFF_FF = 3 * WIDTH
OFF_DQ = OFF_FF + GATE_COLS
OFF_DK, OFF_DV = OFF_DQ + WIDTH, OFF_DQ + 2 * WIDTH

KV_TILE = 256
TOKEN_TILE = 512
FF_TILE = 256
SUM_ROWS = 16
N_CHAINS = FOX_HEADS + 2 * DIFF_HEADS
CUM_SEQS = 4


def _params(*sem):
    return pltpu.CompilerParams(dimension_semantics=sem, vmem_limit_bytes=VMEM_LIMIT_BYTES)


def _resident(shape):
    nd = len(shape)
    return pl.BlockSpec(shape, lambda *_: (0,) * nd, pipeline_mode=pl.Buffered(1))


def _ada_kernel(c_ref, w_ref, b_ref, o_ref):
    c = c_ref[...]
    s = c * (1.0 / (1.0 + jnp.exp(-c)))
    o_ref[...] = jnp.dot(s.astype(BF16), w_ref[...].astype(BF16),
                         preferred_element_type=F32) + b_ref[...]


def _ada(c, w_ada, b_ada):
    n, d = c.shape
    cols = w_ada.shape[1]
    tn = cols // 4
    return pl.pallas_call(
        _ada_kernel,
        grid=(cols // tn,),
        in_specs=[pl.BlockSpec((n, d), lambda j: (0, 0)),
                  pl.BlockSpec((d, tn), lambda j: (0, j)),
                  pl.BlockSpec((1, tn), lambda j: (0, j))],
        out_specs=pl.BlockSpec((n, tn), lambda j: (0, j)),
        out_shape=jax.ShapeDtypeStruct((n, cols), F32),
        compiler_params=_params("arbitrary"),
        name="ada",
    )(c, w_ada, b_ada.reshape(1, cols))


def _inproj_kernel(x_ref, mod_ref, w_ref, bf_ref, cos_ref, sin_ref,
                   fk_o, fv_o, lf_o, dk_o, dv_o,
                   fqb_o, fkb_o, fvb_o, dqb_o, dkb_o, dvb_o, *, transposed):
    bb, tt, d = x_ref.shape
    m = mod_ref[...]
    u = x_ref[...] * (1.0 + m[:, 1:2, :]) + m[:, 0:1, :]
    u = u.reshape(bb * tt, d).astype(BF16)
    q_scale = Q_SCALE * LOG2E

    def proj(off, width):
        return jnp.dot(u, w_ref[:, off:off + width], preferred_element_type=F32)

    def put(o_ref, v):
        o_ref[...] = v.reshape(o_ref.shape).astype(o_ref.dtype)

    def put_qv(o_ref, v):
        if transposed:
            for c in range(tt // KV_TILE):
                o_ref[0, c] = v[c * KV_TILE:(c + 1) * KV_TILE, :].T.astype(o_ref.dtype)
        else:
            put(o_ref, v)

    fq = proj(OFF_FQ, WIDTH)
    put_qv(fqb_o, fq * q_scale)
    fk = proj(OFF_FK, WIDTH)
    put(fk_o, fk)
    put(fkb_o, fk)
    fv = proj(OFF_FV, WIDTH)
    put(fv_o, fv)
    put_qv(fvb_o, fv)

    zf = proj(OFF_FF, GATE_COLS) + bf_ref[...]
    lf = jnp.minimum(zf, 0.0) - jnp.log1p(jnp.exp(-jnp.abs(zf)))
    lf_o[...] = lf.T[:FOX_HEADS, :]

    cos = cos_ref[...]
    sin = sin_ref[...]
    lane = lax.broadcasted_iota(jnp.int32, (1, WIDTH), 1)
    first_half = (lane % HEAD_DIM) < (HEAD_DIM // 2)

    def rope(v):
        partner = jnp.where(first_half, pltpu.roll(v, WIDTH - HEAD_DIM // 2, 1),
                            pltpu.roll(v, HEAD_DIM // 2, 1))
        return v * cos + partner * sin

    dq = rope(proj(OFF_DQ, WIDTH))
    put_qv(dqb_o, dq * q_scale)
    dk = rope(proj(OFF_DK, WIDTH))
    put(dk_o, dk)
    put(dkb_o, dk)
    dv = proj(OFF_DV, WIDTH)
    for hd in range(DIFF_HEADS):
        dv_o[:, hd] = dv[:, hd * PAIR:(hd + 1) * PAIR].reshape(bb, tt, PAIR)
    put_qv(dvb_o, dv)


def _inproj(x, mod, w_pad, bf_pad, cos_t, sin_t, bb, tt, transposed):
    B, T, d = x.shape
    rows = bb * tt
    grid = (B // bb, T // tt)
    assert not transposed or (bb == 1 and tt % KV_TILE == 0)
    tok = lambda w: pl.BlockSpec((bb, tt, w), lambda b, t: (b, t, 0))
    tab = pl.BlockSpec((rows, WIDTH), (lambda b, t: (t, 0)) if bb == 1 else (lambda b, t: (0, 0)))
    sds = lambda w, dt: jax.ShapeDtypeStruct((B, T, w), dt)
    if transposed:
        qv = pl.BlockSpec((1, tt // KV_TILE, WIDTH, KV_TILE), lambda b, t: (b, t, 0, 0))
        qv_sds = jax.ShapeDtypeStruct((B, T // KV_TILE, WIDTH, KV_TILE), BF16)
    else:
        qv, qv_sds = tok(WIDTH), sds(WIDTH, BF16)
    return pl.pallas_call(
        functools.partial(_inproj_kernel, transposed=transposed),
        grid=grid,
        in_specs=[tok(d),
                  pl.BlockSpec((bb, 6, d), lambda b, t: (b, 0, 0)),
                  _resident(w_pad.shape),
                  _resident(bf_pad.shape),
                  tab, tab],
        out_specs=[tok(WIDTH), tok(WIDTH),
                   pl.BlockSpec((FOX_HEADS, rows), lambda b, t: (0, b * (T // tt) + t)),
                   tok(WIDTH),
                   pl.BlockSpec((bb, DIFF_HEADS, tt, PAIR), lambda b, t: (b, 0, t, 0)),
                   qv, tok(WIDTH), qv, qv, tok(WIDTH), qv],
        out_shape=[sds(WIDTH, F32), sds(WIDTH, F32), jax.ShapeDtypeStruct((FOX_HEADS, B * T), F32),
                   sds(WIDTH, F32),
                   jax.ShapeDtypeStruct((B, DIFF_HEADS, T, PAIR), F32),
                   qv_sds, sds(WIDTH, BF16), qv_sds, qv_sds, sds(WIDTH, BF16), qv_sds],
        compiler_params=_params("arbitrary", "arbitrary"),
        name="inproj",
    )(x, mod, w_pad, bf_pad, cos_t, sin_t)


def _split3(x):
    hi = x.astype(BF16).astype(F32)
    r1 = x - hi
    mid = r1.astype(BF16).astype(F32)
    lo = (r1 - mid).astype(BF16).astype(F32)
    return hi, mid, lo


def _cum_kernel(x_ref, o_ref, e_ref, *, keys_on_lanes):
    bb, _, S = x_ref.shape
    nblk = S // MXU_DIM
    r = lax.broadcasted_iota(jnp.int32, (MXU_DIM, MXU_DIM), 0)
    c = lax.broadcasted_iota(jnp.int32, (MXU_DIM, MXU_DIM), 1)
    tri = jnp.where(r <= c, 1.0, 0.0).astype(BF16)
    pieces = []
    for b in range(bb):
        for k in range(nblk):
            pieces += list(_split3(x_ref[b, :, k * MXU_DIM:(k + 1) * MXU_DIM]))
    local = jnp.dot(jnp.concatenate(pieces, axis=0).astype(BF16), tri, preferred_element_type=F32)
    pad = jnp.zeros((LANES - 3 * FOX_HEADS, MXU_DIM), F32)
    for b in range(bb):
        carry = jnp.zeros((FOX_HEADS, 1), F32)
        for k in range(nblk):
            sl = slice(k * MXU_DIM, (k + 1) * MXU_DIM)
            r0 = 3 * FOX_HEADS * (b * nblk + k)
            block = (local[r0:r0 + FOX_HEADS] + local[r0 + FOX_HEADS:r0 + 2 * FOX_HEADS]
                     + local[r0 + 2 * FOX_HEADS:r0 + 3 * FOX_HEADS])
            part = block + carry
            carry = carry + block[:, MXU_DIM - 1:MXU_DIM]
            scaled = part * LOG2E
            o_ref[b, :, sl] = scaled
            bias = jnp.concatenate(_split3(-scaled) + (pad,), axis=0)
            if keys_on_lanes:
                e_ref[b, :, sl] = bias.astype(BF16)
            else:
                e_ref[b, sl, :] = bias.T.astype(BF16)


def _cum(lf_t, bias_layout):
    B, H, S = lf_t.shape
    bb = math.gcd(B, CUM_SEQS)
    spec = pl.BlockSpec((bb, H, S), lambda b: (b, 0, 0))
    e_shape = (B, LANES, S) if bias_layout == "lanes" else (B, S, LANES)
    return pl.pallas_call(
        functools.partial(_cum_kernel, keys_on_lanes=bias_layout == "lanes"),
        grid=(B // bb,), in_specs=[spec],
        out_specs=[spec, pl.BlockSpec((bb,) + e_shape[1:], lambda b: (b, 0, 0))],
        out_shape=[jax.ShapeDtypeStruct((B, H, S), F32), jax.ShapeDtypeStruct(e_shape, BF16)],
        compiler_params=_params("arbitrary"),
        name="cum",
    )(lf_t)


def _lambda_value(lv_ref, lam_init):
    lv = lv_ref[...]
    a = jnp.sum(lv[0:1] * lv[1:2], axis=1, keepdims=True)
    b = jnp.sum(lv[2:3] * lv[3:4], axis=1, keepdims=True)
    return jnp.exp(a) - jnp.exp(b) + lam_init


def _attn_prompt_kernel(fq_ref, fqn_ref, fk_ref, e_ref, fv_ref, dq_ref, dqn_ref, dk_ref, dv_ref, cq_ref,
                        sel_ref, lv_ref, g_ref, o_ref, s_ref, mb_ref, w_ref, acc_ref, *, lam_init):
    tq = tk = KV_TILE
    i = pl.program_id(1)
    srow = lax.broadcasted_iota(jnp.int32, (PAIR, tq), 0)
    first = srow < HEAD_DIM
    half = tq // 2
    krow = lax.broadcasted_iota(jnp.int32, (half, half), 0)
    qcol = lax.broadcasted_iota(jnp.int32, (half, half), 1)
    causal = krow <= qcol
    chunk_causal = (krow // CHUNK) <= (qcol // CHUNK)
    zero = jnp.zeros((PAIR, tq), BF16)
    ones = jnp.ones((SUM_ROWS, tk), BF16)

    def rows_of(j):
        return pl.ds(pl.multiple_of(j * tk, tk), tk)

    def pair_lanes(p):
        return slice(p * PAIR, (p + 1) * PAIR)

    def weights(fq_src, dq_src):
        out = []
        for pair in range(N_PAIRS):
            qt = fq_src[0, 0, pair_lanes(pair), :]
            out += [jnp.concatenate([jnp.where(first, qt, zero), sel_ref[2 * pair]], axis=0),
                    jnp.concatenate([jnp.where(first, zero, qt), sel_ref[2 * pair + 1]], axis=0)]
        for hd in range(DIFF_HEADS):
            qt = dq_src[0, 0, pair_lanes(hd), :]
            out += [jnp.where(first, qt, zero), jnp.where(first, zero, qt)]
        return out

    chains = []
    for pair in range(N_PAIRS):
        lanes = pair_lanes(pair)
        keys = lambda j, lanes=lanes: jnp.concatenate([fk_ref[0, rows_of(j), lanes], e_ref[0, rows_of(j), :]], axis=1)
        for sub in range(2):
            head = 2 * pair + sub
            vt = lambda j, r0=head * HEAD_DIM: fv_ref[0, j, r0:r0 + HEAD_DIM, :]
            chains.append((2 * PAIR, cq_ref[0, head:head + 1, :], keys, vt, causal, HEAD_DIM))
    for hd in range(DIFF_HEADS):
        lanes = pair_lanes(hd)
        keys = lambda j, lanes=lanes: dk_ref[0, rows_of(j), lanes]
        vt = lambda j, lanes=lanes: dv_ref[0, j, lanes, :]
        chains += [(PAIR, None, keys, vt, chunk_causal, PAIR)] * 2
    n = len(chains)

    def score(c, j, w):
        return jnp.dot(chains[c][2](j), w, preferred_element_type=F32)

    def keep(c, s):
        s_ref[c] = s
        return jnp.max(s, axis=0, keepdims=True)

    def consume(c, j, mb, m):
        _, cq, _, vt, _, rows = chains[c]
        used = rows + SUM_ROWS
        s = s_ref[c]
        m_new = jnp.maximum(m, mb if cq is None else mb + cq)
        p = jnp.exp2(s - (m_new if cq is None else m_new - cq))
        alpha = jnp.exp2(m - m_new)
        acc_ref[c, :used] = alpha * acc_ref[c, :used] + jnp.dot(
            jnp.concatenate([vt(j), ones], axis=0), p.astype(BF16), preferred_element_type=F32)
        return m_new

    def consume_diagonal(c, m):
        _, cq, _, vt, mask, rows = chains[c]
        used = rows + SUM_ROWS
        ps, alphas = [], []
        for h in range(2):
            cols = slice(h * half, (h + 1) * half)
            s = jnp.where(mask, s_ref[c, h * half:(h + 1) * half, cols], NEG_BIG)
            if h:
                s = jnp.concatenate([s_ref[c, :half, cols], s], axis=0)
            mb = jnp.max(s, axis=0, keepdims=True)
            m_new = jnp.maximum(m[:, cols], mb if cq is None else mb + cq[:, cols])
            p = jnp.exp2(s - (m_new if cq is None else m_new - cq[:, cols])).astype(BF16)
            ps.append(p if h else jnp.concatenate([p, jnp.zeros((half, half), BF16)], axis=0))
            alphas.append(jnp.exp2(m[:, cols] - m_new))
        acc_ref[c, :used] = jnp.concatenate(alphas, axis=1) * acc_ref[c, :used] + jnp.dot(
            jnp.concatenate([vt(i), ones], axis=0), jnp.concatenate(ps, axis=1), preferred_element_type=F32)

    @pl.when(i == 0)
    def _():
        for c, w in enumerate(weights(fq_ref, dq_ref)):
            w_ref[c, :chains[c][0]] = w
            mb_ref[c] = keep(c, score(c, 0, w))

    def body(j, carry):
        ms, mbs = list(carry[0]), list(carry[1])
        for c in range(n):
            s_next = score(c, j + 1, w_ref[c, :chains[c][0]])
            ms[c] = consume(c, j, mbs[c], ms[c])
            mbs[c] = keep(c, s_next)
        return tuple(ms), tuple(mbs)

    acc_ref[...] = jnp.zeros_like(acc_ref)
    def body4(jj, carry):
        for k in range(4):
            carry = body(4 * jj + k, carry)
        return carry

    carry = lax.fori_loop(
        0, i // 4, body4,
        (tuple(jnp.full((1, tq), NEG_BIG, F32) for _ in range(n)), tuple(mb_ref[c] for c in range(n))))
    base = (i // 4) * 4
    carry = lax.cond(i % 4 >= 2, lambda carry: body(base + 1, body(base, carry)), lambda carry: carry, carry)
    ms, _ = lax.cond(i % 2 == 1, lambda carry: body(i - 1, carry), lambda carry: carry, carry)
    outs = []
    for c, w_next in enumerate(weights(fqn_ref, dqn_ref)):
        rows = chains[c][5]
        s_next = score(c, 0, w_next)
        consume_diagonal(c, ms[c])
        w_ref[c, :chains[c][0]] = w_next
        mb_ref[c] = keep(c, s_next)
        outs.append(acc_ref[c, :rows] / acc_ref[c, rows:rows + 1])

    for pair in range(N_PAIRS):
        oa, ob = outs[2 * pair], outs[2 * pair + 1]
        o_ref[0, :, pair_lanes(pair)] = jnp.concatenate([oa, ob], axis=0).T.astype(o_ref.dtype)

    lam = _lambda_value(lv_ref, lam_init)
    for hd in range(DIFF_HEADS):
        o1, o2 = outs[2 * N_PAIRS + 2 * hd], outs[2 * N_PAIRS + 2 * hd + 1]
        d = o1 - lam * o2
        ms = jnp.mean(d * d, axis=0, keepdims=True)
        dn = (d * lax.rsqrt(ms + RMS_EPS)).T * g_ref[...] * (1.0 - lam_init)
        o_ref[0, :, WIDTH + hd * PAIR:WIDTH + (hd + 1) * PAIR] = dn.astype(o_ref.dtype)


def _attn_prompt(fq_t, fk, e, fv_t, dq_t, dk, dv_t, cq_t, lambda_vecs, subln_g, lam_init):
    B, T, _ = fk.shape
    tq = KV_TILE
    nt = T // tq
    qspec = pl.BlockSpec((1, 1, WIDTH, tq), lambda b, i: (b, i, 0, 0))
    qnext = pl.BlockSpec((1, 1, WIDTH, tq), lambda b, i: (b, jnp.minimum(i + 1, nt - 1), 0, 0))
    vspec = pl.BlockSpec((1, nt, WIDTH, tq), lambda b, i: (b, 0, 0, 0))
    kspec = pl.BlockSpec((1, T, WIDTH), lambda b, i: (b, 0, 0))
    slot = jnp.arange(PAIR)[None, :, None]
    sel = (slot % FOX_HEADS == jnp.arange(FOX_HEADS)[:, None, None]) & (slot < 3 * FOX_HEADS)
    sel = jnp.broadcast_to(sel, (FOX_HEADS, PAIR, tq)).astype(BF16)
    return pl.pallas_call(
        functools.partial(_attn_prompt_kernel, lam_init=lam_init),
        grid=(B, nt),
        in_specs=[qspec, qnext, kspec, pl.BlockSpec((1, T, LANES), lambda b, i: (b, 0, 0)), vspec,
                  qspec, qnext, kspec, vspec,
                  pl.BlockSpec((1, FOX_HEADS, tq), lambda b, i: (b, 0, i)),
                  _resident(sel.shape),
                  pl.BlockSpec(lambda_vecs.shape, lambda b, i: (0, 0)),
                  pl.BlockSpec(subln_g.shape, lambda b, i: (0, 0))],
        out_specs=pl.BlockSpec((1, tq, 2 * WIDTH), lambda b, i: (b, i, 0)),
        out_shape=jax.ShapeDtypeStruct((B, T, 2 * WIDTH), BF16),
        scratch_shapes=[pltpu.VMEM((N_CHAINS, tq, tq), F32),
                        pltpu.VMEM((N_CHAINS, 1, tq), F32),
                        pltpu.VMEM((N_CHAINS, 2 * PAIR, tq), BF16),
                        pltpu.VMEM((N_CHAINS, PAIR + SUM_ROWS, tq), F32)],
        compiler_params=_params("arbitrary", "arbitrary"),
        name="attn_prompt",
    )(fq_t, fq_t, fk, e, fv_t, dq_t, dq_t, dk, dv_t, cq_t, sel, lambda_vecs, subln_g)


def _attn_sample_kernel(fq_ref, fk_ref, fv_ref, dq_ref, dk_ref, dv_ref,
                        pfk_ref, pfv_ref, pdk_ref, pdv_ref, et_ref, cq_ref,
                        lv_ref, g_ref, o_ref, s_ref, *, lam_init):
    T = fq_ref.shape[1]
    P = pfk_ref.shape[2]
    tk = KV_TILE
    n_past = P // tk
    nrow = FOX_HEADS * T
    prow = 2 * T
    f_idx, d_idx = 0, 1
    nt_dims = (((1,), (1,)), ((), ()))

    row = lambda shape: lax.broadcasted_iota(jnp.int32, shape, 0)
    col = lambda shape: lax.broadcasted_iota(jnp.int32, shape, 1)

    own_head = row((nrow, WIDTH)) // T == col((nrow, WIDTH)) // HEAD_DIM
    select_t = jnp.where((col((nrow, LANES)) % FOX_HEADS == row((nrow, LANES)) // T)
                         & (col((nrow, LANES)) < 3 * FOX_HEADS), 1.0, 0.0).astype(BF16)

    def block_diag(q_ref):
        q = q_ref[0]
        return jnp.where(own_head, jnp.concatenate([q] * FOX_HEADS, axis=0), jnp.zeros((nrow, WIDTH), BF16))

    def pad_rows(x):
        return jnp.concatenate([x, jnp.zeros((LANES - T, x.shape[1]), x.dtype)], axis=0)

    def lane_halves(x, op):
        return op(x[:, :LANES], x[:, LANES:])

    qf = block_diag(fq_ref)
    qd = block_diag(dq_ref)
    lhs_f = jnp.concatenate([qf, select_t], axis=1)

    mf = md = jnp.full((nrow, LANES), NEG_BIG, F32)
    for j in range(n_past):
        cols = slice(j * tk, (j + 1) * tk)
        kf = jnp.concatenate([pfk_ref[0, :, cols].astype(BF16), et_ref[0, :, cols]], axis=0)
        sf = jnp.dot(lhs_f, kf, preferred_element_type=F32)
        sd = jnp.dot(qd, pdk_ref[0, :, cols].astype(BF16), preferred_element_type=F32)
        s_ref[f_idx, j] = sf
        s_ref[d_idx, j] = sd
        mf = jnp.maximum(mf, lane_halves(sf, jnp.maximum))
        md = jnp.maximum(md, lane_halves(sd, jnp.maximum))

    key = col((nrow, LANES))
    qpos = row((nrow, LANES)) % T
    valid = key < T
    causal = valid & (key <= qpos)
    chunk_causal = valid & ((key + P) // CHUNK <= (qpos + P) // CHUNK)
    sf_new = (lax.dot_general(qf, pad_rows(fk_ref[0]), nt_dims, preferred_element_type=F32)
              + jnp.dot(select_t, et_ref[0, :, P:P + LANES], preferred_element_type=F32))
    sf_new = jnp.where(causal, sf_new, NEG_BIG)
    sd_new = jnp.where(chunk_causal,
                       lax.dot_general(qd, pad_rows(dk_ref[0]), nt_dims, preferred_element_type=F32), NEG_BIG)
    cq = cq_ref[0]
    m_f = jnp.max(jnp.maximum(mf, sf_new), axis=1, keepdims=True) + cq
    shift_f = jnp.broadcast_to(m_f - cq, (nrow, LANES))
    shift_d = jnp.broadcast_to(jnp.max(jnp.maximum(md, sd_new), axis=1, keepdims=True), (nrow, LANES))

    def probs(s, shift):
        return jnp.exp2(s - jnp.concatenate([shift] * (s.shape[1] // LANES), axis=1))

    lf = ld = jnp.zeros((nrow, LANES), F32)
    acc_f = [jnp.zeros((prow, PAIR), F32) for _ in range(N_PAIRS)]
    acc_d = [jnp.zeros((prow, PAIR), F32) for _ in range(DIFF_HEADS)]
    for j in range(n_past):
        cols = slice(j * tk, (j + 1) * tk)
        pf = probs(s_ref[f_idx, j], shift_f)
        pd = probs(s_ref[d_idx, j], shift_d)
        lf = lf + lane_halves(pf, jnp.add)
        ld = ld + lane_halves(pd, jnp.add)
        pf, pd = pf.astype(BF16), pd.astype(BF16)
        for u in range(N_PAIRS):
            rows = slice(u * prow, (u + 1) * prow)
            acc_f[u] = acc_f[u] + lax.dot_general(
                pf[rows], pfv_ref[0, u * PAIR:(u + 1) * PAIR, cols].astype(BF16), nt_dims,
                preferred_element_type=F32)
            v = pdv_ref[0, pl.ds(j * tk * DIFF_HEADS + u, tk, stride=DIFF_HEADS), :].astype(BF16)
            acc_d[u] = acc_d[u] + jnp.dot(pd[rows], v, preferred_element_type=F32)
    pf = jnp.exp2(sf_new - shift_f)
    pd = jnp.exp2(sd_new - shift_d)
    lf = jnp.sum(lf + pf, axis=1, keepdims=True)
    ld = jnp.sum(ld + pd, axis=1, keepdims=True)
    pf, pd = pf.astype(BF16), pd.astype(BF16)
    fv_new, dv_new = pad_rows(fv_ref[0]), pad_rows(dv_ref[0])
    low = col((T, PAIR)) < HEAD_DIM
    lam = _lambda_value(lv_ref, lam_init)
    for u in range(N_PAIRS):
        rows = slice(u * prow, (u + 1) * prow)
        lanes = slice(u * PAIR, (u + 1) * PAIR)
        a = (acc_f[u] + jnp.dot(pf[rows], fv_new[:, lanes], preferred_element_type=F32)) / lf[rows]
        o_ref[0, :, lanes] = jnp.where(low, a[:T], a[T:]).astype(o_ref.dtype)
        a = (acc_d[u] + jnp.dot(pd[rows], dv_new[:, lanes], preferred_element_type=F32)) / ld[rows]
        d = a[:T] - lam * a[T:]
        ms = jnp.mean(d * d, axis=-1, keepdims=True)
        dn = d * lax.rsqrt(ms + RMS_EPS) * g_ref[...] * (1.0 - lam_init)
        o_ref[0, :, WIDTH + u * PAIR:WIDTH + (u + 1) * PAIR] = dn.astype(o_ref.dtype)


def _attn_sample(fq, fk, fv, dq, dk, dv, pfk_t, pfv_t, pdk_t, pdv_rows, e_t, cq, lambda_vecs, subln_g,
                 lam_init):
    B, T, _ = fq.shape
    P = pfk_t.shape[2]
    tk = KV_TILE
    nrow = FOX_HEADS * T
    assert N_PAIRS == DIFF_HEADS and T <= LANES and P % tk == 0 and e_t.shape[2] >= P + LANES
    new = pl.BlockSpec((1, T, WIDTH), lambda b: (b, 0, 0))
    cached = pl.BlockSpec((1, WIDTH, P), lambda b: (b, 0, 0))
    return pl.pallas_call(
        functools.partial(_attn_sample_kernel, lam_init=lam_init),
        grid=(B,),
        in_specs=[new] * 6 + [cached, cached, cached,
                              pl.BlockSpec((1, P * DIFF_HEADS, PAIR), lambda b: (b, 0, 0)),
                              pl.BlockSpec((1,) + e_t.shape[1:], lambda b: (b, 0, 0)),
                              pl.BlockSpec((1, nrow, 1), lambda b: (b, 0, 0)),
                              pl.BlockSpec(lambda_vecs.shape, lambda b: (0, 0)),
                              pl.BlockSpec(subln_g.shape, lambda b: (0, 0))],
        out_specs=pl.BlockSpec((1, T, 2 * WIDTH), lambda b: (b, 0, 0)),
        out_shape=jax.ShapeDtypeStruct((B, T, 2 * WIDTH), BF16),
        scratch_shapes=[pltpu.VMEM((2, P // tk, nrow, tk), F32)],
        compiler_params=_params("arbitrary"),
        name="attn_sample",
    )(fq, fk, fv, dq, dk, dv, pfk_t, pfv_t, pdk_t, pdv_rows, e_t, cq, lambda_vecs, subln_g)


def _post_norm(x, h, gate, g, b, alpha):
    y = alpha * x + gate * h
    mu = jnp.mean(y, axis=-1, keepdims=True)
    yc = y - mu
    var = jnp.mean(yc * yc, axis=-1, keepdims=True)
    return yc * lax.rsqrt(var + LN_EPS) * g + b


def _post_kernel(x_ref, o_ref, mod_ref, prev_ref, wo_ref, ln1g_ref, ln1b_ref,
                 wup_ref, cw_ref, cb_ref, wdn_ref, ln2g_ref, ln2b_ref,
                 y_ref, conv_ref, carry_ref, hid_ref, *, alpha):
    bb, tt, d = x_ref.shape
    rows = bb * tt
    keep = CONV_WIDTH - 1
    t = pl.program_id(1)

    @pl.when(t == 0)
    def _():
        carry_ref[...] = prev_ref[...]

    m = mod_ref[...]

    def rows3(v):
        return jnp.broadcast_to(v, (bb, tt, v.shape[-1])).reshape(rows, v.shape[-1])

    x = x_ref[...].reshape(rows, d)
    h = jnp.dot(o_ref[...].reshape(rows, d), wo_ref[...], preferred_element_type=F32)
    x1 = _post_norm(x, h, rows3(m[:, 2:3, :]), ln1g_ref[...], ln1b_ref[...], alpha)
    u2 = (x1 * (1.0 + rows3(m[:, 4:5, :])) + rows3(m[:, 3:4, :])).astype(BF16)

    tpos = lax.broadcasted_iota(jnp.int32, (bb, tt, 1), 1).reshape(rows, 1)
    for c in range(D_FF // FF_TILE):
        cols = slice(c * FF_TILE, (c + 1) * FF_TILE)
        a = jnp.dot(u2, wup_ref[:, cols], preferred_element_type=F32)
        g = jnp.dot(u2, wup_ref[:, D_FF + c * FF_TILE:D_FF + (c + 1) * FF_TILE],
                    preferred_element_type=F32)
        prev = carry_ref[:, :, cols]
        p2 = rows3(prev[:, 0:1, :])
        p1 = rows3(prev[:, 1:2, :])
        am1 = jnp.where(tpos == 0, p1, pltpu.roll(a, 1, 0))
        am2 = jnp.where(tpos == 0, p2, jnp.where(tpos == 1, p1, pltpu.roll(a, 2, 0)))
        cw = cw_ref[:, cols]
        conv = cb_ref[:, cols] + am2 * cw[0:1] + am1 * cw[1:2] + a * cw[2:3]
        hid = conv * (1.0 / (1.0 + jnp.exp(-conv))) * g
        hid_ref[:, cols] = hid.astype(BF16)
        last = a.reshape(bb, tt, FF_TILE)[:, tt - keep:, :]
        carry_ref[:, :, cols] = last
        conv_ref[:, :, cols] = last

    f_out = jnp.dot(hid_ref[...], wdn_ref[...], preferred_element_type=F32)
    y = _post_norm(x1, f_out, rows3(m[:, 5:6, :]), ln2g_ref[...], ln2b_ref[...], alpha)
    y_ref[...] = y.reshape(bb, tt, d)


def _post(x, o, mod, conv_prev, wo, ln1g, ln1b, wup, cw, cb, wdn, ln2g, ln2b, bb, tt, alpha):
    B, T, d = x.shape
    keep = CONV_WIDTH - 1
    tok = pl.BlockSpec((bb, tt, d), lambda b, t: (b, t, 0))
    per_b = lambda r, w: pl.BlockSpec((bb, r, w), lambda b, t: (b, 0, 0))
    return pl.pallas_call(
        functools.partial(_post_kernel, alpha=alpha),
        grid=(B // bb, T // tt),
        in_specs=[tok, tok, per_b(6, d), per_b(keep, D_FF),
                  _resident(wo.shape), _resident(ln1g.shape), _resident(ln1b.shape),
                  _resident(wup.shape), _resident(cw.shape), _resident(cb.shape),
                  _resident(wdn.shape), _resident(ln2g.shape), _resident(ln2b.shape)],
        out_specs=[tok, per_b(keep, D_FF)],
        out_shape=[jax.ShapeDtypeStruct((B, T, d), F32),
                   jax.ShapeDtypeStruct((B, keep, D_FF), F32)],
        scratch_shapes=[pltpu.VMEM((bb, keep, D_FF), F32),
                        pltpu.VMEM((bb * tt, D_FF), BF16)],
        compiler_params=_params("arbitrary", "arbitrary"),
        name="post",
    )(x, o, mod, conv_prev, wo, ln1g, ln1b, wup, cw, cb, wdn, ln2g, ln2b)


def _rope_tables(pos0, T):
    inv = ROPE_THETA ** (-jnp.arange(0, HEAD_DIM, 2, dtype=F32) / HEAD_DIM)
    ang = (pos0 + jnp.arange(T)).astype(F32)[:, None] * inv[None, :]
    cos, sin = jnp.cos(ang), jnp.sin(ang)
    reps = WIDTH // HEAD_DIM
    return (jnp.tile(jnp.concatenate([cos, cos], axis=1), (1, reps)),
            jnp.tile(jnp.concatenate([-sin, sin], axis=1), (1, reps)))


def _round_up(n, k):
    return -(-n // k) * k


def _layer(x, mod, past, w, lam_init, alpha):
    B, T, d = x.shape
    P = 0 if past is None else past[0].shape[1]
    if T % TOKEN_TILE == 0:
        bb, tt = 1, TOKEN_TILE
    else:
        tt = T
        bb = math.gcd(B, max(1, TOKEN_TILE // T))
    cos_t, sin_t = _rope_tables(P, T)
    if bb > 1:
        cos_t, sin_t = jnp.tile(cos_t, (bb, 1)), jnp.tile(sin_t, (bb, 1))
    (fk, fv, lf_2d, dk, dv, fqb, fkb, fvb, dqb, dkb, dvb) = _inproj(
        x, mod, w["w_in"], w["b_f"], cos_t, sin_t, bb, tt, transposed=past is None)

    lf_t = jnp.swapaxes(lf_2d.reshape(FOX_HEADS, B, T), 0, 1)
    lf = jnp.swapaxes(lf_t, 1, 2)
    if past is None:
        cq_t, e = _cum(lf_t, "rows")
        o = _attn_prompt(fqb, fkb, e, fvb, dqb, dkb, dvb, cq_t,
                         w["lambda_vecs"], w["subln_g"], lam_init)
        conv_prev = jnp.zeros((B, CONV_WIDTH - 1, D_FF), F32)
    else:
        pfk, pfv, plf, pdk, pdv, conv_prev = past
        S = _round_up(P + LANES, MXU_DIM)
        lf_all = jnp.concatenate(
            [jnp.swapaxes(plf, 1, 2), lf_t, jnp.zeros((B, FOX_HEADS, S - P - T), F32)], axis=2)
        cq_t, e_t = _cum(lf_all, "lanes")
        cq = cq_t[:, :, P:P + T].reshape(B, FOX_HEADS * T, 1)
        keys_on_lanes = lambda c: jnp.transpose(c, (0, 2, 3, 1)).reshape(B, WIDTH, P)
        o = _attn_sample(fqb, fkb, fvb, dqb, dkb, dvb,
                         keys_on_lanes(pfk), keys_on_lanes(pfv), keys_on_lanes(pdk),
                         pdv.reshape(B, P * DIFF_HEADS, PAIR),
                         e_t, cq, w["lambda_vecs"], w["subln_g"], lam_init)

    y, conv = _post(x, o, mod, conv_prev, w["w_o"], w["ln1_g"], w["ln1_b"], w["w_up"],
                    w["conv_w"], w["conv_b"], w["w_down"], w["ln2_g"], w["ln2_b"], bb, tt, alpha)
    state = (fk.reshape(B, T, FOX_HEADS, HEAD_DIM), fv.reshape(B, T, FOX_HEADS, HEAD_DIM), lf,
             dk.reshape(B, T, 2 * DIFF_HEADS, HEAD_DIM), jnp.swapaxes(dv, 1, 2), conv)
    return y, state


def kernel(x_prompt, x_sample, c_prompt, c_sample, cache_fox_k, cache_fox_v, cache_fox_logf, cache_diff_k, cache_diff_v, state_ffn_conv, w_ada, b_ada, w_in, b_f, lambda_vecs, subln_g, w_o, ln1_g, ln1_b, w_up, conv_w, conv_b, w_down, ln2_g, ln2_b):
    depth = w_ada.shape[0]
    alpha = (2 * depth) ** 0.25
    nb = c_prompt.shape[0]
    yp, ys = x_prompt, x_sample
    c_all = jnp.concatenate([c_prompt, c_sample], axis=0)
    p_states, s_states = [], []
    for l in range(depth):
        lam_init = 0.8 - 0.6 * math.exp(-0.3 * l)
        w_in_l = w_in[l]
        gate_w = jnp.pad(w_in_l[:, OFF_FF:OFF_FF + FOX_HEADS], ((0, 0), (0, GATE_COLS - FOX_HEADS)))
        w = {
            "w_in": jnp.concatenate(
                [w_in_l[:, :OFF_FF], gate_w, w_in_l[:, OFF_FF + FOX_HEADS:]], axis=1).astype(BF16),
            "b_f": jnp.pad(b_f[l], (0, GATE_COLS - FOX_HEADS)).reshape(1, GATE_COLS),
            "lambda_vecs": lambda_vecs[l],
            "subln_g": subln_g[l].reshape(1, PAIR),
            "w_o": w_o[l].astype(BF16),
            "ln1_g": ln1_g[l].reshape(1, D_MODEL), "ln1_b": ln1_b[l].reshape(1, D_MODEL),
            "w_up": w_up[l].astype(BF16),
            "conv_w": conv_w[l], "conv_b": conv_b[l].reshape(1, D_FF),
            "w_down": w_down[l].astype(BF16),
            "ln2_g": ln2_g[l].reshape(1, D_MODEL), "ln2_b": ln2_b[l].reshape(1, D_MODEL),
        }
        mod = _ada(c_all, w_ada[l], b_ada[l]).reshape(c_all.shape[0], 6, D_MODEL)
        yp, st_p = _layer(yp, mod[:nb], None, w, lam_init, alpha)
        past = (cache_fox_k[l], cache_fox_v[l], cache_fox_logf[l], cache_diff_k[l], cache_diff_v[l],
                state_ffn_conv[l])
        ys, st_s = _layer(ys, mod[nb:], past, w, lam_init, alpha)
        p_states.append(st_p)
        s_states.append(st_s)
    p_out = [jnp.stack(a, axis=0) for a in zip(*p_states)]
    s_out = [jnp.stack(a, axis=0) for a in zip(*s_states)]
    return (yp, ys, *p_out, *s_out)
```

```python
import functools
import math

import jax
import jax.numpy as jnp
from jax import lax
from jax.experimental import pallas as pl
from jax.experimental.pallas import tpu as pltpu

F32 = jnp.float32
BF16 = jnp.bfloat16

D_MODEL = 1024
CHUNK = 64
FOX_HEADS = 8
DIFF_HEADS = 4
HEAD_DIM = 64
WIDTH = 512
PAIR = 2 * HEAD_DIM
N_PAIRS = WIDTH // PAIR
D_FF = 2816
CONV_WIDTH = 3
ROPE_THETA = 10000.0
LN_EPS = 1e-5
RMS_EPS = 1e-6
NEG_BIG = -1e30
Q_SCALE = HEAD_DIM ** -0.5
LOG2E = math.log2(math.e)

LANES = 128
MXU_DIM = 256
VMEM_LIMIT_BYTES = 56 * 1024 * 1024

GATE_COLS = LANES
OFF_FQ, OFF_FK, OFF_FV = 0, WIDTH, 2 * WIDTH
OFF_FF = 3 * WIDTH
OFF_DQ = OFF_FF + GATE_COLS
OFF_DK, OFF_DV = OFF_DQ + WIDTH, OFF_DQ + 2 * WIDTH

KV_TILE = 256
TOKEN_TILE = 512
FF_TILE = 256
SUM_ROWS = 16
N_CHAINS = FOX_HEADS + 2 * DIFF_HEADS
CUM_SEQS = 4


def _params(*sem):
    return pltpu.CompilerParams(dimension_semantics=sem, vmem_limit_bytes=VMEM_LIMIT_BYTES)


def _resident(shape):
    nd = len(shape)
    return pl.BlockSpec(shape, lambda *_: (0,) * nd, pipeline_mode=pl.Buffered(1))


def _ada_kernel(c_ref, w_ref, b_ref, o_ref):
    c = c_ref[...]
    s = c * (1.0 / (1.0 + jnp.exp(-c)))
    o_ref[...] = jnp.dot(s.astype(BF16), w_ref[...].astype(BF16),
                         preferred_element_type=F32) + b_ref[...]


def _ada(c, w_ada, b_ada):
    n, d = c.shape
    cols = w_ada.shape[1]
    tn = cols // 4
    return pl.pallas_call(
        _ada_kernel,
        grid=(cols // tn,),
        in_specs=[pl.BlockSpec((n, d), lambda j: (0, 0)),
                  pl.BlockSpec((d, tn), lambda j: (0, j)),
                  pl.BlockSpec((1, tn), lambda j: (0, j))],
        out_specs=pl.BlockSpec((n, tn), lambda j: (0, j)),
        out_shape=jax.ShapeDtypeStruct((n, cols), F32),
        compiler_params=_params("arbitrary"),
        name="ada",
    )(c, w_ada, b_ada.reshape(1, cols))


def _inproj_kernel(x_ref, mod_ref, w_ref, bf_ref, cos_ref, sin_ref,
                   fk_o, fv_o, lf_o, dk_o, dv_o,
                   fqb_o, fkb_o, fvb_o, dqb_o, dkb_o, dvb_o, *, transposed):
    bb, tt, d = x_ref.shape
    m = mod_ref[...]
    u = x_ref[...] * (1.0 + m[:, 1:2, :]) + m[:, 0:1, :]
    u = u.reshape(bb * tt, d).astype(BF16)
    q_scale = Q_SCALE * LOG2E

    def proj(off, width):
        return jnp.dot(u, w_ref[:, off:off + width], preferred_element_type=F32)

    def put(o_ref, v):
        o_ref[...] = v.reshape(o_ref.shape).astype(o_ref.dtype)

    def put_qv(o_ref, v):
        if transposed:
            for c in range(tt // KV_TILE):
                o_ref[0, c] = v[c * KV_TILE:(c + 1) * KV_TILE, :].T.astype(o_ref.dtype)
        else:
            put(o_ref, v)

    fq = proj(OFF_FQ, WIDTH)
    put_qv(fqb_o, fq * q_scale)
    fk = proj(OFF_FK, WIDTH)
    put(fk_o, fk)
    put(fkb_o, fk)
    fv = proj(OFF_FV, WIDTH)
    put(fv_o, fv)
    put_qv(fvb_o, fv)

    zf = proj(OFF_FF, GATE_COLS) + bf_ref[...]
    lf = jnp.minimum(zf, 0.0) - jnp.log1p(jnp.exp(-jnp.abs(zf)))
    lf_o[...] = lf.T[:FOX_HEADS, :]

    cos = cos_ref[...]
    sin = sin_ref[...]
    lane = lax.broadcasted_iota(jnp.int32, (1, WIDTH), 1)
    first_half = (lane % HEAD_DIM) < (HEAD_DIM // 2)

    def rope(v):
        partner = jnp.where(first_half, pltpu.roll(v, WIDTH - HEAD_DIM // 2, 1),
                            pltpu.roll(v, HEAD_DIM // 2, 1))
        return v * cos + partner * sin

    dq = rope(proj(OFF_DQ, WIDTH))
    put_qv(dqb_o, dq * q_scale)
    dk = rope(proj(OFF_DK, WIDTH))
    put(dk_o, dk)
    put(dkb_o, dk)
    dv = proj(OFF_DV, WIDTH)
    for hd in range(DIFF_HEADS):
        dv_o[:, hd] = dv[:, hd * PAIR:(hd + 1) * PAIR].reshape(bb, tt, PAIR)
    put_qv(dvb_o, dv)


def _inproj(x, mod, w_pad, bf_pad, cos_t, sin_t, bb, tt, transposed):
    B, T, d = x.shape
    rows = bb * tt
    grid = (B // bb, T // tt)
    assert not transposed or (bb == 1 and tt % KV_TILE == 0)
    tok = lambda w: pl.BlockSpec((bb, tt, w), lambda b, t: (b, t, 0))
    tab = pl.BlockSpec((rows, WIDTH), (lambda b, t: (t, 0)) if bb == 1 else (lambda b, t: (0, 0)))
    sds = lambda w, dt: jax.ShapeDtypeStruct((B, T, w), dt)
    if transposed:
        qv = pl.BlockSpec((1, tt // KV_TILE, WIDTH, KV_TILE), lambda b, t: (b, t, 0, 0))
        qv_sds = jax.ShapeDtypeStruct((B, T // KV_TILE, WIDTH, KV_TILE), BF16)
    else:
        qv, qv_sds = tok(WIDTH), sds(WIDTH, BF16)
    return pl.pallas_call(
        functools.partial(_inproj_kernel, transposed=transposed),
        grid=grid,
        in_specs=[tok(d),
                  pl.BlockSpec((bb, 6, d), lambda b, t: (b, 0, 0)),
                  _resident(w_pad.shape),
                  _resident(bf_pad.shape),
                  tab, tab],
        out_specs=[tok(WIDTH), tok(WIDTH),
                   pl.BlockSpec((FOX_HEADS, rows), lambda b, t: (0, b * (T // tt) + t)),
                   tok(WIDTH),
                   pl.BlockSpec((bb, DIFF_HEADS, tt, PAIR), lambda b, t: (b, 0, t, 0)),
                   qv, tok(WIDTH), qv, qv, tok(WIDTH), qv],
        out_shape=[sds(WIDTH, F32), sds(WIDTH, F32), jax.ShapeDtypeStruct((FOX_HEADS, B * T), F32),
                   sds(WIDTH, F32),
                   jax.ShapeDtypeStruct((B, DIFF_HEADS, T, PAIR), F32),
                   qv_sds, sds(WIDTH, BF16), qv_sds, qv_sds, sds(WIDTH, BF16), qv_sds],
        compiler_params=_params("arbitrary", "arbitrary"),
        name="inproj",
    )(x, mod, w_pad, bf_pad, cos_t, sin_t)


def _split3(x):
    hi = x.astype(BF16).astype(F32)
    r1 = x - hi
    mid = r1.astype(BF16).astype(F32)
    lo = (r1 - mid).astype(BF16).astype(F32)
    return hi, mid, lo


def _cum_kernel(x_ref, o_ref, e_ref, *, keys_on_lanes):
    bb, _, S = x_ref.shape
    nblk = S // MXU_DIM
    r = lax.broadcasted_iota(jnp.int32, (MXU_DIM, MXU_DIM), 0)
    c = lax.broadcasted_iota(jnp.int32, (MXU_DIM, MXU_DIM), 1)
    tri = jnp.where(r <= c, 1.0, 0.0).astype(BF16)
    pieces = []
    for b in range(bb):
        for k in range(nblk):
            pieces += list(_split3(x_ref[b, :, k * MXU_DIM:(k + 1) * MXU_DIM]))
    local = jnp.dot(jnp.concatenate(pieces, axis=0).astype(BF16), tri, preferred_element_type=F32)
    pad = jnp.zeros((LANES - 3 * FOX_HEADS, MXU_DIM), F32)
    for b in range(bb):
        carry = jnp.zeros((FOX_HEADS, 1), F32)
        for k in range(nblk):
            sl = slice(k * MXU_DIM, (k + 1) * MXU_DIM)
            r0 = 3 * FOX_HEADS * (b * nblk + k)
            block = (local[r0:r0 + FOX_HEADS] + local[r0 + FOX_HEADS:r0 + 2 * FOX_HEADS]
                     + local[r0 + 2 * FOX_HEADS:r0 + 3 * FOX_HEADS])
            part = block + carry
            carry = carry + block[:, MXU_DIM - 1:MXU_DIM]
            scaled = part * LOG2E
            o_ref[b, :, sl] = scaled
            bias = jnp.concatenate(_split3(-scaled) + (pad,), axis=0)
            if keys_on_lanes:
                e_ref[b, :, sl] = bias.astype(BF16)
            else:
                e_ref[b, sl, :] = bias.T.astype(BF16)


def _cum(lf_t, bias_layout):
    B, H, S = lf_t.shape
    bb = math.gcd(B, CUM_SEQS)
    spec = pl.BlockSpec((bb, H, S), lambda b: (b, 0, 0))
    e_shape = (B, LANES, S) if bias_layout == "lanes" else (B, S, LANES)
    return pl.pallas_call(
        functools.partial(_cum_kernel, keys_on_lanes=bias_layout == "lanes"),
        grid=(B // bb,), in_specs=[spec],
        out_specs=[spec, pl.BlockSpec((bb,) + e_shape[1:], lambda b: (b, 0, 0))],
        out_shape=[jax.ShapeDtypeStruct((B, H, S), F32), jax.ShapeDtypeStruct(e_shape, BF16)],
        compiler_params=_params("arbitrary"),
        name="cum",
    )(lf_t)


def _lambda_value(lv_ref, lam_init):
    lv = lv_ref[...]
    a = jnp.sum(lv[0:1] * lv[1:2], axis=1, keepdims=True)
    b = jnp.sum(lv[2:3] * lv[3:4], axis=1, keepdims=True)
    return jnp.exp(a) - jnp.exp(b) + lam_init


def _attn_prompt_kernel(fq_ref, fqn_ref, fk_ref, e_ref, fv_ref, dq_ref, dqn_ref, dk_ref, dv_ref, cq_ref,
                        sel_ref, lv_ref, g_ref, o_ref, s_ref, mb_ref, w_ref, acc_ref, *, lam_init):
    tq = tk = KV_TILE
    i = pl.program_id(1)
    srow = lax.broadcasted_iota(jnp.int32, (PAIR, tq), 0)
    first = srow < HEAD_DIM
    half = tq // 2
    krow = lax.broadcasted_iota(jnp.int32, (half, half), 0)
    qcol = lax.broadcasted_iota(jnp.int32, (half, half), 1)
    causal = krow <= qcol
    chunk_causal = (krow // CHUNK) <= (qcol // CHUNK)
    zero = jnp.zeros((PAIR, tq), BF16)
    ones = jnp.ones((SUM_ROWS, tk), BF16)

    def rows_of(j):
        return pl.ds(pl.multiple_of(j * tk, tk), tk)

    def pair_lanes(p):
        return slice(p * PAIR, (p + 1) * PAIR)

    def weights(fq_src, dq_src):
        out = []
        for pair in range(N_PAIRS):
            qt = fq_src[0, 0, pair_lanes(pair), :]
            out += [jnp.concatenate([jnp.where(first, qt, zero), sel_ref[2 * pair]], axis=0),
                    jnp.concatenate([jnp.where(first, zero, qt), sel_ref[2 * pair + 1]], axis=0)]
        for hd in range(DIFF_HEADS):
            qt = dq_src[0, 0, pair_lanes(hd), :]
            out += [jnp.where(first, qt, zero), jnp.where(first, zero, qt)]
        return out

    chains = []
    for pair in range(N_PAIRS):
        lanes = pair_lanes(pair)
        keys = lambda j, lanes=lanes: jnp.concatenate([fk_ref[0, rows_of(j), lanes], e_ref[0, rows_of(j), :]], axis=1)
        for sub in range(2):
            head = 2 * pair + sub
            vt = lambda j, r0=head * HEAD_DIM: fv_ref[0, j, r0:r0 + HEAD_DIM, :]
            chains.append((2 * PAIR, cq_ref[0, head:head + 1, :], keys, vt, causal, HEAD_DIM))
    for hd in range(DIFF_HEADS):
        lanes = pair_lanes(hd)
        keys = lambda j, lanes=lanes: dk_ref[0, rows_of(j), lanes]
        vt = lambda j, lanes=lanes: dv_ref[0, j, lanes, :]
        chains += [(PAIR, None, keys, vt, chunk_causal, PAIR)] * 2
    n = len(chains)

    def score(c, j, w):
        return jnp.dot(chains[c][2](j), w, preferred_element_type=F32)

    def keep(c, s):
        s_ref[c] = s
        return jnp.max(s, axis=0, keepdims=True)

    def consume(c, j, mb, m):
        _, cq, _, vt, _, rows = chains[c]
        used = rows + SUM_ROWS
        s = s_ref[c]
        m_new = jnp.maximum(m, mb if cq is None else mb + cq)
        p = jnp.exp2(s - (m_new if cq is None else m_new - cq))
        alpha = jnp.exp2(m - m_new)
        acc_ref[c, :used] = alpha * acc_ref[c, :used] + jnp.dot(
            jnp.concatenate([vt(j), ones], axis=0), p.astype(BF16), preferred_element_type=F32)
        return m_new

    def consume_diagonal(c, m):
        _, cq, _, vt, mask, rows = chains[c]
        used = rows + SUM_ROWS
        ps, alphas = [], []
        for h in range(2):
            cols = slice(h * half, (h + 1) * half)
            s = jnp.where(mask, s_ref[c, h * half:(h + 1) * half, cols], NEG_BIG)
            if h:
                s = jnp.concatenate([s_ref[c, :half, cols], s], axis=0)
            mb = jnp.max(s, axis=0, keepdims=True)
            m_new = jnp.maximum(m[:, cols], mb if cq is None else mb + cq[:, cols])
            p = jnp.exp2(s - (m_new if cq is None else m_new - cq[:, cols])).astype(BF16)
            ps.append(p if h else jnp.concatenate([p, jnp.zeros((half, half), BF16)], axis=0))
            alphas.append(jnp.exp2(m[:, cols] - m_new))
        acc_ref[c, :used] = jnp.concatenate(alphas, axis=1) * acc_ref[c, :used] + jnp.dot(
            jnp.concatenate([vt(i), ones], axis=0), jnp.concatenate(ps, axis=1), preferred_element_type=F32)

    @pl.when(i == 0)
    def _():
        for c, w in enumerate(weights(fq_ref, dq_ref)):
            w_ref[c, :chains[c][0]] = w
            mb_ref[c] = keep(c, score(c, 0, w))

    def body(j, carry):
        ms, mbs = list(carry[0]), list(carry[1])
        for c in range(n):
            s_next = score(c, j + 1, w_ref[c, :chains[c][0]])
            ms[c] = consume(c, j, mbs[c], ms[c])
            mbs[c] = keep(c, s_next)
        return tuple(ms), tuple(mbs)

    acc_ref[...] = jnp.zeros_like(acc_ref)
    def body4(jj, carry):
        for k in range(4):
            carry = body(4 * jj + k, carry)
        return carry

    carry = lax.fori_loop(
        0, i // 4, body4,
        (tuple(jnp.full((1, tq), NEG_BIG, F32) for _ in range(n)), tuple(mb_ref[c] for c in range(n))))
    base = (i // 4) * 4
    carry = lax.cond(i % 4 >= 2, lambda carry: body(base + 1, body(base, carry)), lambda carry: carry, carry)
    ms, _ = lax.cond(i % 2 == 1, lambda carry: body(i - 1, carry), lambda carry: carry, carry)
    outs = []
    for c, w_next in enumerate(weights(fqn_ref, dqn_ref)):
        rows = chains[c][5]
        s_next = score(c, 0, w_next)
        consume_diagonal(c, ms[c])
        w_ref[c, :chains[c][0]] = w_next
        mb_ref[c] = keep(c, s_next)
        outs.append(acc_ref[c, :rows] / acc_ref[c, rows:rows + 1])

    for pair in range(N_PAIRS):
        oa, ob = outs[2 * pair], outs[2 * pair + 1]
        o_ref[0, :, pair_lanes(pair)] = jnp.concatenate([oa, ob], axis=0).T.astype(o_ref.dtype)

    lam = _lambda_value(lv_ref, lam_init)
    for hd in range(DIFF_HEADS):
        o1, o2 = outs[2 * N_PAIRS + 2 * hd], outs[2 * N_PAIRS + 2 * hd + 1]
        d = o1 - lam * o2
        ms = jnp.mean(d * d, axis=0, keepdims=True)
        dn = (d * lax.rsqrt(ms + RMS_EPS)).T * g_ref[...] * (1.0 - lam_init)
        o_ref[0, :, WIDTH + hd * PAIR:WIDTH + (hd + 1) * PAIR] = dn.astype(o_ref.dtype)


def _attn_prompt(fq_t, fk, e, fv_t, dq_t, dk, dv_t, cq_t, lambda_vecs, subln_g, lam_init):
    B, T, _ = fk.shape
    tq = KV_TILE
    nt = T // tq
    qspec = pl.BlockSpec((1, 1, WIDTH, tq), lambda b, i: (b, i, 0, 0))
    qnext = pl.BlockSpec((1, 1, WIDTH, tq), lambda b, i: (b, jnp.minimum(i + 1, nt - 1), 0, 0))
    vspec = pl.BlockSpec((1, nt, WIDTH, tq), lambda b, i: (b, 0, 0, 0))
    kspec = pl.BlockSpec((1, T, WIDTH), lambda b, i: (b, 0, 0))
    slot = jnp.arange(PAIR)[None, :, None]
    sel = (slot % FOX_HEADS == jnp.arange(FOX_HEADS)[:, None, None]) & (slot < 3 * FOX_HEADS)
    sel = jnp.broadcast_to(sel, (FOX_HEADS, PAIR, tq)).astype(BF16)
    return pl.pallas_call(
        functools.partial(_attn_prompt_kernel, lam_init=lam_init),
        grid=(B, nt),
        in_specs=[qspec, qnext, kspec, pl.BlockSpec((1, T, LANES), lambda b, i: (b, 0, 0)), vspec,
                  qspec, qnext, kspec, vspec,
                  pl.BlockSpec((1, FOX_HEADS, tq), lambda b, i: (b, 0, i)),
                  _resident(sel.shape),
                  pl.BlockSpec(lambda_vecs.shape, lambda b, i: (0, 0)),
                  pl.BlockSpec(subln_g.shape, lambda b, i: (0, 0))],
        out_specs=pl.BlockSpec((1, tq, 2 * WIDTH), lambda b, i: (b, i, 0)),
        out_shape=jax.ShapeDtypeStruct((B, T, 2 * WIDTH), BF16),
        scratch_shapes=[pltpu.VMEM((N_CHAINS, tq, tq), F32),
                        pltpu.VMEM((N_CHAINS, 1, tq), F32),
                        pltpu.VMEM((N_CHAINS, 2 * PAIR, tq), BF16),
                        pltpu.VMEM((N_CHAINS, PAIR + SUM_ROWS, tq), F32)],
        compiler_params=_params("arbitrary", "arbitrary"),
        name="attn_prompt",
    )(fq_t, fq_t, fk, e, fv_t, dq_t, dq_t, dk, dv_t, cq_t, sel, lambda_vecs, subln_g)


def _attn_sample_kernel(fq_ref, fk_ref, fv_ref, dq_ref, dk_ref, dv_ref,
                        pfk_ref, pfv_ref, pdk_ref, pdv_ref, et_ref, cq_ref,
                        lv_ref, g_ref, o_ref, s_ref, *, lam_init):
    T = fq_ref.shape[1]
    P = pfk_ref.shape[2]
    tk = KV_TILE
    n_past = P // tk
    nrow = FOX_HEADS * T
    prow = 2 * T
    f_idx, d_idx = 0, 1
    nt_dims = (((1,), (1,)), ((), ()))

    row = lambda shape: lax.broadcasted_iota(jnp.int32, shape, 0)
    col = lambda shape: lax.broadcasted_iota(jnp.int32, shape, 1)

    own_head = row((nrow, WIDTH)) // T == col((nrow, WIDTH)) // HEAD_DIM
    select_t = jnp.where((col((nrow, LANES)) % FOX_HEADS == row((nrow, LANES)) // T)
                         & (col((nrow, LANES)) < 3 * FOX_HEADS), 1.0, 0.0).astype(BF16)

    def block_diag(q_ref):
        q = q_ref[0]
        return jnp.where(own_head, jnp.concatenate([q] * FOX_HEADS, axis=0), jnp.zeros((nrow, WIDTH), BF16))

    def pad_rows(x):
        return jnp.concatenate([x, jnp.zeros((LANES - T, x.shape[1]), x.dtype)], axis=0)

    def lane_halves(x, op):
        return op(x[:, :LANES], x[:, LANES:])

    qf = block_diag(fq_ref)
    qd = block_diag(dq_ref)
    lhs_f = jnp.concatenate([qf, select_t], axis=1)

    mf = md = jnp.full((nrow, LANES), NEG_BIG, F32)
    for j in range(n_past):
        cols = slice(j * tk, (j + 1) * tk)
        kf = jnp.concatenate([pfk_ref[0, :, cols].astype(BF16), et_ref[0, :, cols]], axis=0)
        sf = jnp.dot(lhs_f, kf, preferred_element_type=F32)
        sd = jnp.dot(qd, pdk_ref[0, :, cols].astype(BF16), preferred_element_type=F32)
        s_ref[f_idx, j] = sf
        s_ref[d_idx, j] = sd
        mf = jnp.maximum(mf, lane_halves(sf, jnp.maximum))
        md = jnp.maximum(md, lane_halves(sd, jnp.maximum))

    key = col((nrow, LANES))
    qpos = row((nrow, LANES)) % T
    valid = key < T
    causal = valid & (key <= qpos)
    chunk_causal = valid & ((key + P) // CHUNK <= (qpos + P) // CHUNK)
    sf_new = (lax.dot_general(qf, pad_rows(fk_ref[0]), nt_dims, preferred_element_type=F32)
              + jnp.dot(select_t, et_ref[0, :, P:P + LANES], preferred_element_type=F32))
    sf_new = jnp.where(causal, sf_new, NEG_BIG)
    sd_new = jnp.where(chunk_causal,
                       lax.dot_general(qd, pad_rows(dk_ref[0]), nt_dims, preferred_element_type=F32), NEG_BIG)
    cq = cq_ref[0]
    m_f = jnp.max(jnp.maximum(mf, sf_new), axis=1, keepdims=True) + cq
    shift_f = jnp.broadcast_to(m_f - cq, (nrow, LANES))
    shift_d = jnp.broadcast_to(jnp.max(jnp.maximum(md, sd_new), axis=1, keepdims=True), (nrow, LANES))

    def probs(s, shift):
        return jnp.exp2(s - jnp.concatenate([shift] * (s.shape[1] // LANES), axis=1))

    lf = ld = jnp.zeros((nrow, LANES), F32)
    acc_f = [jnp.zeros((prow, PAIR), F32) for _ in range(N_PAIRS)]
    acc_d = [jnp.zeros((prow, PAIR), F32) for _ in range(DIFF_HEADS)]
    for j in range(n_past):
        cols = slice(j * tk, (j + 1) * tk)
        pf = probs(s_ref[f_idx, j], shift_f)
        pd = probs(s_ref[d_idx, j], shift_d)
        lf = lf + lane_halves(pf, jnp.add)
        ld = ld + lane_halves(pd, jnp.add)
        pf, pd = pf.astype(BF16), pd.astype(BF16)
        for u in range(N_PAIRS):
            rows = slice(u * prow, (u + 1) * prow)
            acc_f[u] = acc_f[u] + lax.dot_general(
                pf[rows], pfv_ref[0, u * PAIR:(u + 1) * PAIR, cols].astype(BF16), nt_dims,
                preferred_element_type=F32)
            v = pdv_ref[0, pl.ds(j * tk * DIFF_HEADS + u, tk, stride=DIFF_HEADS), :].astype(BF16)
            acc_d[u] = acc_d[u] + jnp.dot(pd[rows], v, preferred_element_type=F32)
    pf = jnp.exp2(sf_new - shift_f)
    pd = jnp.exp2(sd_new - shift_d)
    lf = jnp.sum(lf + pf, axis=1, keepdims=True)
    ld = jnp.sum(ld + pd, axis=1, keepdims=True)
    pf, pd = pf.astype(BF16), pd.astype(BF16)
    fv_new, dv_new = pad_rows(fv_ref[0]), pad_rows(dv_ref[0])
    low = col((T, PAIR)) < HEAD_DIM
    lam = _lambda_value(lv_ref, lam_init)
    for u in range(N_PAIRS):
        rows = slice(u * prow, (u + 1) * prow)
        lanes = slice(u * PAIR, (u + 1) * PAIR)
        a = (acc_f[u] + jnp.dot(pf[rows], fv_new[:, lanes], preferred_element_type=F32)) / lf[rows]
        o_ref[0, :, lanes] = jnp.where(low, a[:T], a[T:]).astype(o_ref.dtype)
        a = (acc_d[u] + jnp.dot(pd[rows], dv_new[:, lanes], preferred_element_type=F32)) / ld[rows]
        d = a[:T] - lam * a[T:]
        ms = jnp.mean(d * d, axis=-1, keepdims=True)
        dn = d * lax.rsqrt(ms + RMS_EPS) * g_ref[...] * (1.0 - lam_init)
        o_ref[0, :, WIDTH + u * PAIR:WIDTH + (u + 1) * PAIR] = dn.astype(o_ref.dtype)


def _attn_sample(fq, fk, fv, dq, dk, dv, pfk_t, pfv_t, pdk_t, pdv_rows, e_t, cq, lambda_vecs, subln_g,
                 lam_init):
    B, T, _ = fq.shape
    P = pfk_t.shape[2]
    tk = KV_TILE
    nrow = FOX_HEADS * T
    assert N_PAIRS == DIFF_HEADS and T <= LANES and P % tk == 0 and e_t.shape[2] >= P + LANES
    new = pl.BlockSpec((1, T, WIDTH), lambda b: (b, 0, 0))
    cached = pl.BlockSpec((1, WIDTH, P), lambda b: (b, 0, 0))
    return pl.pallas_call(
        functools.partial(_attn_sample_kernel, lam_init=lam_init),
        grid=(B,),
        in_specs=[new] * 6 + [cached, cached, cached,
                              pl.BlockSpec((1, P * DIFF_HEADS, PAIR), lambda b: (b, 0, 0)),
                              pl.BlockSpec((1,) + e_t.shape[1:], lambda b: (b, 0, 0)),
                              pl.BlockSpec((1, nrow, 1), lambda b: (b, 0, 0)),
                              pl.BlockSpec(lambda_vecs.shape, lambda b: (0, 0)),
                              pl.BlockSpec(subln_g.shape, lambda b: (0, 0))],
        out_specs=pl.BlockSpec((1, T, 2 * WIDTH), lambda b: (b, 0, 0)),
        out_shape=jax.ShapeDtypeStruct((B, T, 2 * WIDTH), BF16),
        scratch_shapes=[pltpu.VMEM((2, P // tk, nrow, tk), F32)],
        compiler_params=_params("arbitrary"),
        name="attn_sample",
    )(fq, fk, fv, dq, dk, dv, pfk_t, pfv_t, pdk_t, pdv_rows, e_t, cq, lambda_vecs, subln_g)


def _post_norm(x, h, gate, g, b, alpha):
    y = alpha * x + gate * h
    mu = jnp.mean(y, axis=-1, keepdims=True)
    yc = y - mu
    var = jnp.mean(yc * yc, axis=-1, keepdims=True)
    return yc * lax.rsqrt(var + LN_EPS) * g + b


def _post_kernel(x_ref, o_ref, mod_ref, prev_ref, wo_ref, ln1g_ref, ln1b_ref,
                 wup_ref, cw_ref, cb_ref, wdn_ref, ln2g_ref, ln2b_ref,
                 y_ref, conv_ref, carry_ref, hid_ref, *, alpha):
    bb, tt, d = x_ref.shape
    rows = bb * tt
    keep = CONV_WIDTH - 1
    t = pl.program_id(1)

    @pl.when(t == 0)
    def _():
        carry_ref[...] = prev_ref[...]

    m = mod_ref[...]

    def rows3(v):
        return jnp.broadcast_to(v, (bb, tt, v.shape[-1])).reshape(rows, v.shape[-1])

    x = x_ref[...].reshape(rows, d)
    h = jnp.dot(o_ref[...].reshape(rows, d), wo_ref[...], preferred_element_type=F32)
    y1 = alpha * x + rows3(m[:, 2:3, :]) * h
    yc = y1 - jnp.mean(y1, axis=-1, keepdims=True)
    normed = yc * lax.rsqrt(jnp.mean(yc * yc, axis=-1, keepdims=True) + LN_EPS)
    x1 = normed * ln1g_ref[...] + ln1b_ref[...]
    scale1 = 1.0 + m[:, 4:5, :]
    u2 = (normed * rows3(ln1g_ref[...] * scale1) + rows3(ln1b_ref[...] * scale1 + m[:, 3:4, :])).astype(BF16)

    tpos = lax.broadcasted_iota(jnp.int32, (bb, tt, 1), 1).reshape(rows, 1)
    for c in range(D_FF // FF_TILE):
        cols = slice(c * FF_TILE, (c + 1) * FF_TILE)
        a = jnp.dot(u2, wup_ref[:, cols], preferred_element_type=F32)
        g = jnp.dot(u2, wup_ref[:, D_FF + c * FF_TILE:D_FF + (c + 1) * FF_TILE],
                    preferred_element_type=F32)
        prev = carry_ref[:, :, cols]
        p2 = rows3(prev[:, 0:1, :])
        p1 = rows3(prev[:, 1:2, :])
        am1 = jnp.where(tpos == 0, p1, pltpu.roll(a, 1, 0))
        am2 = jnp.where(tpos == 0, p2, jnp.where(tpos == 1, p1, pltpu.roll(a, 2, 0)))
        cw = cw_ref[:, cols]
        conv = cb_ref[:, cols] + am2 * cw[0:1] + am1 * cw[1:2] + a * cw[2:3]
        hid = conv * (1.0 / (1.0 + jnp.exp(-conv))) * g
        hid_ref[:, cols] = hid.astype(BF16)
        last = a.reshape(bb, tt, FF_TILE)[:, tt - keep:, :]
        carry_ref[:, :, cols] = last
        conv_ref[:, :, cols] = last

    f_out = jnp.dot(hid_ref[...], wdn_ref[...], preferred_element_type=F32)
    y = _post_norm(x1, f_out, rows3(m[:, 5:6, :]), ln2g_ref[...], ln2b_ref[...], alpha)
    y_ref[...] = y.reshape(bb, tt, d)


def _post(x, o, mod, conv_prev, wo, ln1g, ln1b, wup, cw, cb, wdn, ln2g, ln2b, bb, tt, alpha):
    B, T, d = x.shape
    keep = CONV_WIDTH - 1
    tok = pl.BlockSpec((bb, tt, d), lambda b, t: (b, t, 0))
    per_b = lambda r, w: pl.BlockSpec((bb, r, w), lambda b, t: (b, 0, 0))
    return pl.pallas_call(
        functools.partial(_post_kernel, alpha=alpha),
        grid=(B // bb, T // tt),
        in_specs=[tok, tok, per_b(6, d), per_b(keep, D_FF),
                  _resident(wo.shape), _resident(ln1g.shape), _resident(ln1b.shape),
                  _resident(wup.shape), _resident(cw.shape), _resident(cb.shape),
                  _resident(wdn.shape), _resident(ln2g.shape), _resident(ln2b.shape)],
        out_specs=[tok, per_b(keep, D_FF)],
        out_shape=[jax.ShapeDtypeStruct((B, T, d), F32),
                   jax.ShapeDtypeStruct((B, keep, D_FF), F32)],
        scratch_shapes=[pltpu.VMEM((bb, keep, D_FF), F32),
                        pltpu.VMEM((bb * tt, D_FF), BF16)],
        compiler_params=_params("arbitrary", "arbitrary"),
        name="post",
    )(x, o, mod, conv_prev, wo, ln1g, ln1b, wup, cw, cb, wdn, ln2g, ln2b)


def _rope_tables(pos0, T):
    inv = ROPE_THETA ** (-jnp.arange(0, HEAD_DIM, 2, dtype=F32) / HEAD_DIM)
    ang = (pos0 + jnp.arange(T)).astype(F32)[:, None] * inv[None, :]
    cos, sin = jnp.cos(ang), jnp.sin(ang)
    reps = WIDTH // HEAD_DIM
    return (jnp.tile(jnp.concatenate([cos, cos], axis=1), (1, reps)),
            jnp.tile(jnp.concatenate([-sin, sin], axis=1), (1, reps)))


def _round_up(n, k):
    return -(-n // k) * k


def _layer(x, mod, past, w, lam_init, alpha):
    B, T, d = x.shape
    P = 0 if past is None else past[0].shape[1]
    if T % TOKEN_TILE == 0:
        bb, tt = 1, TOKEN_TILE
    else:
        tt = T
        bb = math.gcd(B, max(1, TOKEN_TILE // T))
    cos_t, sin_t = _rope_tables(P, T)
    if bb > 1:
        cos_t, sin_t = jnp.tile(cos_t, (bb, 1)), jnp.tile(sin_t, (bb, 1))
    (fk, fv, lf_2d, dk, dv, fqb, fkb, fvb, dqb, dkb, dvb) = _inproj(
        x, mod, w["w_in"], w["b_f"], cos_t, sin_t, bb, tt, transposed=past is None)

    lf_t = jnp.swapaxes(lf_2d.reshape(FOX_HEADS, B, T), 0, 1)
    lf = jnp.swapaxes(lf_t, 1, 2)
    if past is None:
        cq_t, e = _cum(lf_t, "rows")
        o = _attn_prompt(fqb, fkb, e, fvb, dqb, dkb, dvb, cq_t,
                         w["lambda_vecs"], w["subln_g"], lam_init)
        conv_prev = jnp.zeros((B, CONV_WIDTH - 1, D_FF), F32)
    else:
        pfk, pfv, plf, pdk, pdv, conv_prev = past
        S = _round_up(P + LANES, MXU_DIM)
        lf_all = jnp.concatenate(
            [jnp.swapaxes(plf, 1, 2), lf_t, jnp.zeros((B, FOX_HEADS, S - P - T), F32)], axis=2)
        cq_t, e_t = _cum(lf_all, "lanes")
        cq = cq_t[:, :, P:P + T].reshape(B, FOX_HEADS * T, 1)
        keys_on_lanes = lambda c: jnp.transpose(c, (0, 2, 3, 1)).reshape(B, WIDTH, P)
        o = _attn_sample(fqb, fkb, fvb, dqb, dkb, dvb,
                         keys_on_lanes(pfk), keys_on_lanes(pfv), keys_on_lanes(pdk),
                         pdv.reshape(B, P * DIFF_HEADS, PAIR),
                         e_t, cq, w["lambda_vecs"], w["subln_g"], lam_init)

    y, conv = _post(x, o, mod, conv_prev, w["w_o"], w["ln1_g"], w["ln1_b"], w["w_up"],
                    w["conv_w"], w["conv_b"], w["w_down"], w["ln2_g"], w["ln2_b"], bb, tt, alpha)
    state = (fk.reshape(B, T, FOX_HEADS, HEAD_DIM), fv.reshape(B, T, FOX_HEADS, HEAD_DIM), lf,
             dk.reshape(B, T, 2 * DIFF_HEADS, HEAD_DIM), jnp.swapaxes(dv, 1, 2), conv)
    return y, state


def kernel(x_prompt, x_sample, c_prompt, c_sample, cache_fox_k, cache_fox_v, cache_fox_logf, cache_diff_k, cache_diff_v, state_ffn_conv, w_ada, b_ada, w_in, b_f, lambda_vecs, subln_g, w_o, ln1_g, ln1_b, w_up, conv_w, conv_b, w_down, ln2_g, ln2_b):
    depth = w_ada.shape[0]
    alpha = (2 * depth) ** 0.25
    nb = c_prompt.shape[0]
    yp, ys = x_prompt, x_sample
    c_all = jnp.concatenate([c_prompt, c_sample], axis=0)
    p_states, s_states = [], []
    for l in range(depth):
        lam_init = 0.8 - 0.6 * math.exp(-0.3 * l)
        w_in_l = w_in[l]
        gate_w = jnp.pad(w_in_l[:, OFF_FF:OFF_FF + FOX_HEADS], ((0, 0), (0, GATE_COLS - FOX_HEADS)))
        w = {
            "w_in": jnp.concatenate(
                [w_in_l[:, :OFF_FF], gate_w, w_in_l[:, OFF_FF + FOX_HEADS:]], axis=1).astype(BF16),
            "b_f": jnp.pad(b_f[l], (0, GATE_COLS - FOX_HEADS)).reshape(1, GATE_COLS),
            "lambda_vecs": lambda_vecs[l],
            "subln_g": subln_g[l].reshape(1, PAIR),
            "w_o": w_o[l].astype(BF16),
            "ln1_g": ln1_g[l].reshape(1, D_MODEL), "ln1_b": ln1_b[l].reshape(1, D_MODEL),
            "w_up": w_up[l].astype(BF16),
            "conv_w": conv_w[l], "conv_b": conv_b[l].reshape(1, D_FF),
            "w_down": w_down[l].astype(BF16),
            "ln2_g": ln2_g[l].reshape(1, D_MODEL), "ln2_b": ln2_b[l].reshape(1, D_MODEL),
        }
        mod = _ada(c_all, w_ada[l], b_ada[l]).reshape(c_all.shape[0], 6, D_MODEL)
        yp, st_p = _layer(yp, mod[:nb], None, w, lam_init, alpha)
        past = (cache_fox_k[l], cache_fox_v[l], cache_fox_logf[l], cache_diff_k[l], cache_diff_v[l],
                state_ffn_conv[l])
        ys, st_s = _layer(ys, mod[nb:], past, w, lam_init, alpha)
        p_states.append(st_p)
        s_states.append(st_s)
    p_out = [jnp.stack(a, axis=0) for a in zip(*p_states)]
    s_out = [jnp.stack(a, axis=0) for a in zip(*s_states)]
    return (yp, ys, *p_out, *s_out)
```
